```python
import jax, jax.numpy as jnp
from jax import lax
import numpy as np

D_MODEL = 2048
BATCH = 8
SEQ = 8192
DEPTH = 4

CHUNK = 64
RET_HEADS = 8
RET_DK = 128
RET_DV = 256
RET_QK = RET_HEADS * RET_DK
RET_V = RET_HEADS * RET_DV
SGU_GROUPS = 8
SGU_LEN = 128
SGU_WIDTH = D_MODEL
SGU_CH = SGU_WIDTH // SGU_GROUPS
D_FF = -(-(8 * D_MODEL) // (3 * 256)) * 256
ROPE_BASE = 10000.0
EPS = 1e-6
SPLITS = [RET_QK, 2 * RET_QK, 2 * RET_QK + RET_V, 2 * RET_QK + 2 * RET_V,
          2 * RET_QK + 2 * RET_V + SGU_WIDTH, 2 * RET_QK + 2 * RET_V + 2 * SGU_WIDTH,
          2 * RET_QK + 2 * RET_V + 2 * SGU_WIDTH + D_MODEL]
IN_COLS = 2 * RET_QK + 2 * RET_V + 2 * SGU_WIDTH + 2 * D_MODEL

kernel_name = "hybrid_retention_sgu_griffin_merge"


def rms_norm(x, w):
    xf = x.astype(jnp.float32)
    y = xf * lax.rsqrt(jnp.mean(xf * xf, axis=-1, keepdims=True) + EPS)
    return (y * w.astype(jnp.float32)).astype(x.dtype)


def layer_norm(x, w, b):
    xf = x.astype(jnp.float32)
    mu = jnp.mean(xf, axis=-1, keepdims=True)
    var = jnp.mean(jnp.square(xf - mu), axis=-1, keepdims=True)
    y = (xf - mu) * lax.rsqrt(var + EPS) * w.astype(jnp.float32) + b.astype(jnp.float32)
    return y.astype(x.dtype)


def head_group_norm(o, w):
    B, S, H, dv = o.shape
    of = o.astype(jnp.float32)
    mu = jnp.mean(of, axis=-1, keepdims=True)
    var = jnp.mean(jnp.square(of - mu), axis=-1, keepdims=True)
    y = ((of - mu) * lax.rsqrt(var + EPS)).reshape(B, S, H * dv) * w.astype(jnp.float32)
    return y.astype(o.dtype)


def rotary(x, pos):
    half = x.shape[-1] // 2
    inv = ROPE_BASE ** (-jnp.arange(half, dtype=jnp.float32) / half)
    ang = pos.astype(jnp.float32)[:, None] * inv[None, :]
    cos = jnp.cos(ang)[None, :, None, :]
    sin = jnp.sin(ang)[None, :, None, :]
    xf = x.astype(jnp.float32)
    x1, x2 = xf[..., :half], xf[..., half:]
    return jnp.concatenate([x1 * cos - x2 * sin, x2 * cos + x1 * sin], axis=-1).astype(x.dtype)


def retention(q, k, v):
    B, S, H, dk = q.shape
    dv = v.shape[-1]
    nc = S // CHUNK
    dt = q.dtype
    log_g = jnp.log1p(-(2.0 ** (-5.0 - jnp.arange(H, dtype=jnp.float32))))
    idx = jnp.arange(CHUNK, dtype=jnp.float32)
    intra_decay = jnp.exp(log_g[:, None, None] * jnp.abs(idx[:, None] - idx[None, :]))
    q_decay = jnp.exp(log_g[None, :] * (idx[:, None] + 1.0))
    k_decay = jnp.exp(log_g[None, :] * (CHUNK - 1.0 - idx[:, None]))
    chunk_decay = jnp.exp(log_g * CHUNK)

    q = q.reshape(B, nc, CHUNK, H, dk) * (dk ** -0.5)
    k = k.reshape(B, nc, CHUNK, H, dk)
    v = v.reshape(B, nc, CHUNK, H, dv)

    scores = jnp.einsum('bnihd,bnjhd->bnhij', q, k) * intra_decay.astype(dt)
    o_intra = jnp.einsum('bnhij,bnjhe->bnihe', scores, v)

    qs = q * q_decay.astype(dt)[:, :, None]
    ks = k * k_decay.astype(dt)[:, :, None]
    cdec = chunk_decay.astype(dt)[None, :, None, None]

    def step(state, xs):
        qc, kc, vc = xs
        o = jnp.einsum('bihd,bhde->bihe', qc, state)
        state = state * cdec + jnp.einsum('bjhd,bjhe->bhde', kc, vc)
        return state, o

    s0 = jnp.zeros((B, H, dk, dv), dtype=v.dtype)
    _, o_inter = lax.scan(step, s0, (jnp.swapaxes(qs, 0, 1), jnp.swapaxes(ks, 0, 1),
                                     jnp.swapaxes(v, 0, 1)))
    o = o_intra + jnp.swapaxes(o_inter, 0, 1)
    return o.reshape(B, S, H, dv)


def spatial_gating(u, v, ln_w, ln_b, w_s, b_s):
    B, S, W = v.shape
    ng = S // SGU_LEN
    vn = layer_norm(v, ln_w, ln_b).reshape(B, ng, SGU_LEN, SGU_GROUPS, SGU_CH)
    pos = jnp.arange(SGU_LEN)
    mask = (pos[None, :] // CHUNK) <= (pos[:, None] // CHUNK)
    w = jnp.where(mask[None], w_s, jnp.zeros_like(w_s))
    mixed = jnp.einsum('gij,bnjgc->bnigc', w, vn) + b_s.T[None, None, :, :, None]
    return u * mixed.reshape(B, S, W)


def hybrid_layer(x, pos, norm_mix_w, w_in, ret_gn_w, ret_proj, sgu_ln_w, sgu_ln_b,
                 sgu_w_s, sgu_b_s, sgu_proj, w_out, norm_ffn_w, w_ffn_in, w_ffn_out):
    B, S, _ = x.shape
    h = rms_norm(x, norm_mix_w)
    z = h @ w_in
    q, k, v, g, su, sv, gate_a, gate_b = jnp.split(z, SPLITS, axis=-1)

    q = rotary(q.reshape(B, S, RET_HEADS, RET_DK), pos)
    k = rotary(k.reshape(B, S, RET_HEADS, RET_DK), pos)
    v = v.reshape(B, S, RET_HEADS, RET_DV)
    ret = head_group_norm(retention(q, k, v), ret_gn_w)
    branch_a = (jax.nn.silu(g) * ret) @ ret_proj

    zu = jax.nn.gelu(su, approximate=False)
    zv = jax.nn.gelu(sv, approximate=False)
    branch_b = spatial_gating(zu, zv, sgu_ln_w, sgu_ln_b, sgu_w_s, sgu_b_s) @ sgu_proj

    merged = jax.nn.sigmoid(gate_a) * branch_a + jax.nn.sigmoid(gate_b) * branch_b
    x = x + merged @ w_out

    h = rms_norm(x, norm_ffn_w)
    a, c = jnp.split(h @ w_ffn_in, 2, axis=-1)
    x = x + (jax.nn.silu(a) * c) @ w_ffn_out
    return x


def _fwd_setup_inputs(seed: int = 0) -> dict:
    key = jax.random.key(seed)
    ks = jax.random.split(key, 16)
    f32 = jnp.float32
    nrm = lambda k, shape, scale: jax.random.normal(k, shape, f32) * scale
    return {
        "x": jax.random.normal(ks[0], (BATCH, SEQ, D_MODEL), f32),
        "norm_mix_w": 1.0 + nrm(ks[1], (DEPTH, D_MODEL), 0.02),
        "w_in": nrm(ks[2], (DEPTH, D_MODEL, IN_COLS), D_MODEL ** -0.5),
        "ret_gn_w": 1.0 + nrm(ks[3], (DEPTH, RET_V), 0.02),
        "ret_proj": nrm(ks[4], (DEPTH, RET_V, D_MODEL), RET_V ** -0.5),
        "sgu_ln_w": 1.0 + nrm(ks[5], (DEPTH, SGU_WIDTH), 0.02),
        "sgu_ln_b": nrm(ks[6], (DEPTH, SGU_WIDTH), 0.02),
        "sgu_w_s": nrm(ks[7], (DEPTH, SGU_GROUPS, SGU_LEN, SGU_LEN), SGU_LEN ** -0.5),
        "sgu_b_s": 1.0 + nrm(ks[8], (DEPTH, SGU_GROUPS, SGU_LEN), 0.02),
        "sgu_proj": nrm(ks[9], (DEPTH, SGU_WIDTH, D_MODEL), SGU_WIDTH ** -0.5),
        "w_out": nrm(ks[10], (DEPTH, D_MODEL, D_MODEL), D_MODEL ** -0.5),
        "norm_ffn_w": 1.0 + nrm(ks[11], (DEPTH, D_MODEL), 0.02),
        "w_ffn_in": nrm(ks[12], (DEPTH, D_MODEL, 2 * D_FF), D_MODEL ** -0.5),
        "w_ffn_out": nrm(ks[13], (DEPTH, D_FF, D_MODEL), D_FF ** -0.5),
        "final_norm_w": 1.0 + nrm(ks[14], (D_MODEL,), 0.02),
    }


def _fwd_reference(x, norm_mix_w, w_in, ret_gn_w, ret_proj, sgu_ln_w, sgu_ln_b, sgu_w_s, sgu_b_s,
              sgu_proj, w_out, norm_ffn_w, w_ffn_in, w_ffn_out, final_norm_w):
    pos = jnp.arange(x.shape[1], dtype=jnp.int32)
    for l in range(DEPTH):
        x = hybrid_layer(x, pos, norm_mix_w[l], w_in[l], ret_gn_w[l], ret_proj[l], sgu_ln_w[l],
                         sgu_ln_b[l], sgu_w_s[l], sgu_b_s[l], sgu_proj[l], w_out[l],
                         norm_ffn_w[l], w_ffn_in[l], w_ffn_out[l])
    return rms_norm(x, final_norm_w)


import jax as _jax
import jax.numpy as _jnp

TWIN_FORMAT = 'train_step'
FWD_PARAMS = ['x', 'norm_mix_w', 'w_in', 'ret_gn_w', 'ret_proj', 'sgu_ln_w', 'sgu_ln_b', 'sgu_w_s', 'sgu_b_s', 'sgu_proj', 'w_out', 'norm_ffn_w', 'w_ffn_in', 'w_ffn_out', 'final_norm_w']
TWIN_WEIGHTS = ['norm_mix_w', 'w_in', 'ret_gn_w', 'ret_proj', 'sgu_ln_w', 'sgu_ln_b', 'sgu_w_s', 'sgu_b_s', 'sgu_proj', 'w_out', 'norm_ffn_w', 'w_ffn_in', 'w_ffn_out', 'final_norm_w']
TWIN_DIFF_INPUT = 'x'
TWIN_INPUTS = ['x', 'norm_mix_w', 'w_in', 'ret_gn_w', 'ret_proj', 'sgu_ln_w', 'sgu_ln_b', 'sgu_w_s', 'sgu_b_s', 'sgu_proj', 'w_out', 'norm_ffn_w', 'w_ffn_in', 'w_ffn_out', 'final_norm_w', 'loss_target', 'm_norm_mix_w', 'm_w_in', 'm_ret_gn_w', 'm_ret_proj', 'm_sgu_ln_w', 'm_sgu_ln_b', 'm_sgu_w_s', 'm_sgu_b_s', 'm_sgu_proj', 'm_w_out', 'm_norm_ffn_w', 'm_w_ffn_in', 'm_w_ffn_out', 'm_final_norm_w', 'v_norm_mix_w', 'v_w_in', 'v_ret_gn_w', 'v_ret_proj', 'v_sgu_ln_w', 'v_sgu_ln_b', 'v_sgu_w_s', 'v_sgu_b_s', 'v_sgu_proj', 'v_w_out', 'v_norm_ffn_w', 'v_w_ffn_in', 'v_w_ffn_out', 'v_final_norm_w']
TWIN_OUTPUTS = ['loss', 'grad_x', 'grad_norm_mix_w', 'grad_w_in', 'grad_ret_gn_w', 'grad_ret_proj', 'grad_sgu_ln_w', 'grad_sgu_ln_b', 'grad_sgu_w_s', 'grad_sgu_b_s', 'grad_sgu_proj', 'grad_w_out', 'grad_norm_ffn_w', 'grad_w_ffn_in', 'grad_w_ffn_out', 'grad_final_norm_w', 'delta_norm_mix_w', 'delta_w_in', 'delta_ret_gn_w', 'delta_ret_proj', 'delta_sgu_ln_w', 'delta_sgu_ln_b', 'delta_sgu_w_s', 'delta_sgu_b_s', 'delta_sgu_proj', 'delta_w_out', 'delta_norm_ffn_w', 'delta_w_ffn_in', 'delta_w_ffn_out', 'delta_final_norm_w', 'new_m_norm_mix_w', 'new_m_w_in', 'new_m_ret_gn_w', 'new_m_ret_proj', 'new_m_sgu_ln_w', 'new_m_sgu_ln_b', 'new_m_sgu_w_s', 'new_m_sgu_b_s', 'new_m_sgu_proj', 'new_m_w_out', 'new_m_norm_ffn_w', 'new_m_w_ffn_in', 'new_m_w_ffn_out', 'new_m_final_norm_w', 'new_v_norm_mix_w', 'new_v_w_in', 'new_v_ret_gn_w', 'new_v_ret_proj', 'new_v_sgu_ln_w', 'new_v_sgu_ln_b', 'new_v_sgu_w_s', 'new_v_sgu_b_s', 'new_v_sgu_proj', 'new_v_w_out', 'new_v_norm_ffn_w', 'new_v_w_ffn_in', 'new_v_w_ffn_out', 'new_v_final_norm_w']
TWIN_LEAF_KINDS = {'loss': 'loss', 'grad_x': 'grad_x', 'grad_norm_mix_w': 'grad_w', 'grad_w_in': 'grad_w', 'grad_ret_gn_w': 'grad_w', 'grad_ret_proj': 'grad_w', 'grad_sgu_ln_w': 'grad_w', 'grad_sgu_ln_b': 'grad_w', 'grad_sgu_w_s': 'grad_w', 'grad_sgu_b_s': 'grad_w', 'grad_sgu_proj': 'grad_w', 'grad_w_out': 'grad_w', 'grad_norm_ffn_w': 'grad_w', 'grad_w_ffn_in': 'grad_w', 'grad_w_ffn_out': 'grad_w', 'grad_final_norm_w': 'grad_w', 'delta_norm_mix_w': 'delta_w', 'delta_w_in': 'delta_w', 'delta_ret_gn_w': 'delta_w', 'delta_ret_proj': 'delta_w', 'delta_sgu_ln_w': 'delta_w', 'delta_sgu_ln_b': 'delta_w', 'delta_sgu_w_s': 'delta_w', 'delta_sgu_b_s': 'delta_w', 'delta_sgu_proj': 'delta_w', 'delta_w_out': 'delta_w', 'delta_norm_ffn_w': 'delta_w', 'delta_w_ffn_in': 'delta_w', 'delta_w_ffn_out': 'delta_w', 'delta_final_norm_w': 'delta_w', 'new_m_norm_mix_w': 'new_m', 'new_m_w_in': 'new_m', 'new_m_ret_gn_w': 'new_m', 'new_m_ret_proj': 'new_m', 'new_m_sgu_ln_w': 'new_m', 'new_m_sgu_ln_b': 'new_m', 'new_m_sgu_w_s': 'new_m', 'new_m_sgu_b_s': 'new_m', 'new_m_sgu_proj': 'new_m', 'new_m_w_out': 'new_m', 'new_m_norm_ffn_w': 'new_m', 'new_m_w_ffn_in': 'new_m', 'new_m_w_ffn_out': 'new_m', 'new_m_final_norm_w': 'new_m', 'new_v_norm_mix_w': 'new_v', 'new_v_w_in': 'new_v', 'new_v_ret_gn_w': 'new_v', 'new_v_ret_proj': 'new_v', 'new_v_sgu_ln_w': 'new_v', 'new_v_sgu_ln_b': 'new_v', 'new_v_sgu_w_s': 'new_v', 'new_v_sgu_b_s': 'new_v', 'new_v_sgu_proj': 'new_v', 'new_v_w_out': 'new_v', 'new_v_norm_ffn_w': 'new_v', 'new_v_w_ffn_in': 'new_v', 'new_v_w_ffn_out': 'new_v', 'new_v_final_norm_w': 'new_v'}


def _forward(args):
    return _fwd_reference(*[args[k] for k in FWD_PARAMS])


def _output_shape():
    def fwd():
        inp = _fwd_setup_inputs(0)
        return _fwd_reference(*[inp[k] for k in FWD_PARAMS])
    out = _jax.eval_shape(fwd)
    return out.shape, out.dtype

N_MICROBATCH = 1
ADAM_LR = 0.001
ADAM_B1 = 0.9
ADAM_B2 = 0.999
ADAM_EPS = 1e-08
ADAM_WD = 0.01
ADAM_STEP = 10
PER_EXAMPLE_BATCH_AXIS = {'x': 0, 'loss_target': 0}
SHARED_INPUTS = []
_WEIGHT_DTYPES = {'norm_mix_w': _jnp.float32, 'w_in': _jnp.float32, 'ret_gn_w': _jnp.float32, 'ret_proj': _jnp.float32, 'sgu_ln_w': _jnp.float32, 'sgu_ln_b': _jnp.float32, 'sgu_w_s': _jnp.float32, 'sgu_b_s': _jnp.float32, 'sgu_proj': _jnp.float32, 'w_out': _jnp.float32, 'norm_ffn_w': _jnp.float32, 'w_ffn_in': _jnp.float32, 'w_ffn_out': _jnp.float32, 'final_norm_w': _jnp.float32}
MOMENT_SCALE = {'norm_mix_w': 1.176547e-01, 'w_in': 4.365958e-02, 'ret_gn_w': 4.132314e-02, 'ret_proj': 4.102738e-02, 'sgu_ln_w': 4.093802e-02, 'sgu_ln_b': 3.981471e-02, 'sgu_w_s': 5.469066e-02, 'sgu_b_s': 6.321652e-02, 'sgu_proj': 6.000988e-02, 'w_out': 7.264063e-02, 'norm_ffn_w': 9.215231e-02, 'w_ffn_in': 3.874340e-02, 'w_ffn_out': 6.324350e-02, 'final_norm_w': 3.194329e+01}


def _to_microbatches(a, axis):
    t = _jnp.moveaxis(a, axis, 0)
    t = t.reshape((N_MICROBATCH, t.shape[0] // N_MICROBATCH) + t.shape[1:])
    return _jnp.moveaxis(t, 1, axis + 1)


def setup_inputs(seed: int = 0) -> dict:
    inp = _fwd_setup_inputs(seed)
    key = _jax.random.fold_in(_jax.random.key(seed), 7919)
    shape, _ = _output_shape()
    out = dict(inp)
    out["loss_target"] = _jax.random.normal(_jax.random.fold_in(key, 0), shape, _jnp.float32)
    for i, name in enumerate(TWIN_WEIGHTS):
        w = inp[name].astype(_jnp.float32)
        if MOMENT_SCALE is None:
            s = _jnp.sqrt(_jnp.mean(_jnp.square(w)) + 1e-30)
        else:
            s = MOMENT_SCALE[name]
        km, kv = _jax.random.split(_jax.random.fold_in(key, i + 1))
        out[name] = w
        out["m_" + name] = s * _jax.random.normal(km, w.shape, _jnp.float32)
        out["v_" + name] = (s * s) * _jax.random.uniform(kv, w.shape, _jnp.float32, 0.5, 1.5)
    if N_MICROBATCH > 1:
        for name, axis in PER_EXAMPLE_BATCH_AXIS.items():
            out[name] = _to_microbatches(out[name], axis)
    return {'x': out['x'], 'norm_mix_w': out['norm_mix_w'], 'w_in': out['w_in'], 'ret_gn_w': out['ret_gn_w'], 'ret_proj': out['ret_proj'], 'sgu_ln_w': out['sgu_ln_w'], 'sgu_ln_b': out['sgu_ln_b'], 'sgu_w_s': out['sgu_w_s'], 'sgu_b_s': out['sgu_b_s'], 'sgu_proj': out['sgu_proj'], 'w_out': out['w_out'], 'norm_ffn_w': out['norm_ffn_w'], 'w_ffn_in': out['w_ffn_in'], 'w_ffn_out': out['w_ffn_out'], 'final_norm_w': out['final_norm_w'], 'loss_target': out['loss_target'], 'm_norm_mix_w': out['m_norm_mix_w'], 'm_w_in': out['m_w_in'], 'm_ret_gn_w': out['m_ret_gn_w'], 'm_ret_proj': out['m_ret_proj'], 'm_sgu_ln_w': out['m_sgu_ln_w'], 'm_sgu_ln_b': out['m_sgu_ln_b'], 'm_sgu_w_s': out['m_sgu_w_s'], 'm_sgu_b_s': out['m_sgu_b_s'], 'm_sgu_proj': out['m_sgu_proj'], 'm_w_out': out['m_w_out'], 'm_norm_ffn_w': out['m_norm_ffn_w'], 'm_w_ffn_in': out['m_w_ffn_in'], 'm_w_ffn_out': out['m_w_ffn_out'], 'm_final_norm_w': out['m_final_norm_w'], 'v_norm_mix_w': out['v_norm_mix_w'], 'v_w_in': out['v_w_in'], 'v_ret_gn_w': out['v_ret_gn_w'], 'v_ret_proj': out['v_ret_proj'], 'v_sgu_ln_w': out['v_sgu_ln_w'], 'v_sgu_ln_b': out['v_sgu_ln_b'], 'v_sgu_w_s': out['v_sgu_w_s'], 'v_sgu_b_s': out['v_sgu_b_s'], 'v_sgu_proj': out['v_sgu_proj'], 'v_w_out': out['v_w_out'], 'v_norm_ffn_w': out['v_norm_ffn_w'], 'v_w_ffn_in': out['v_w_ffn_in'], 'v_w_ffn_out': out['v_w_ffn_out'], 'v_final_norm_w': out['v_final_norm_w']}


def _loss(weights, diff, rest, loss_target):
    with _jax.named_scope("forward"):
        args = {**rest, TWIN_DIFF_INPUT: diff, **{k: w.astype(_WEIGHT_DTYPES[k]) for k, w in weights.items()}}
        y = _forward(args)
    with _jax.named_scope("loss_head"):
        err = _jnp.square(y.astype(_jnp.float32) - loss_target)
        return 0.5 * _jnp.sum(_jnp.mean(err, axis=-1)) if err.ndim else 0.5 * err


def _adamw(w, g, m, v):
    m = ADAM_B1 * m + (1.0 - ADAM_B1) * g
    v = ADAM_B2 * v + (1.0 - ADAM_B2) * _jnp.square(g)
    m_hat = m / (1.0 - ADAM_B1 ** ADAM_STEP)
    v_hat = v / (1.0 - ADAM_B2 ** ADAM_STEP)
    delta = -ADAM_LR * (m_hat / (_jnp.sqrt(v_hat) + ADAM_EPS) + ADAM_WD * w)
    return delta, m, v


def reference(x, norm_mix_w, w_in, ret_gn_w, ret_proj, sgu_ln_w, sgu_ln_b, sgu_w_s, sgu_b_s, sgu_proj, w_out, norm_ffn_w, w_ffn_in, w_ffn_out, final_norm_w, loss_target, m_norm_mix_w, m_w_in, m_ret_gn_w, m_ret_proj, m_sgu_ln_w, m_sgu_ln_b, m_sgu_w_s, m_sgu_b_s, m_sgu_proj, m_w_out, m_norm_ffn_w, m_w_ffn_in, m_w_ffn_out, m_final_norm_w, v_norm_mix_w, v_w_in, v_ret_gn_w, v_ret_proj, v_sgu_ln_w, v_sgu_ln_b, v_sgu_w_s, v_sgu_b_s, v_sgu_proj, v_w_out, v_norm_ffn_w, v_w_ffn_in, v_w_ffn_out, v_final_norm_w):
    given = dict(x=x, norm_mix_w=norm_mix_w, w_in=w_in, ret_gn_w=ret_gn_w, ret_proj=ret_proj, sgu_ln_w=sgu_ln_w, sgu_ln_b=sgu_ln_b, sgu_w_s=sgu_w_s, sgu_b_s=sgu_b_s, sgu_proj=sgu_proj, w_out=w_out, norm_ffn_w=norm_ffn_w, w_ffn_in=w_ffn_in, w_ffn_out=w_ffn_out, final_norm_w=final_norm_w, loss_target=loss_target, m_norm_mix_w=m_norm_mix_w, m_w_in=m_w_in, m_ret_gn_w=m_ret_gn_w, m_ret_proj=m_ret_proj, m_sgu_ln_w=m_sgu_ln_w, m_sgu_ln_b=m_sgu_ln_b, m_sgu_w_s=m_sgu_w_s, m_sgu_b_s=m_sgu_b_s, m_sgu_proj=m_sgu_proj, m_w_out=m_w_out, m_norm_ffn_w=m_norm_ffn_w, m_w_ffn_in=m_w_ffn_in, m_w_ffn_out=m_w_ffn_out, m_final_norm_w=m_final_norm_w, v_norm_mix_w=v_norm_mix_w, v_w_in=v_w_in, v_ret_gn_w=v_ret_gn_w, v_ret_proj=v_ret_proj, v_sgu_ln_w=v_sgu_ln_w, v_sgu_ln_b=v_sgu_ln_b, v_sgu_w_s=v_sgu_w_s, v_sgu_b_s=v_sgu_b_s, v_sgu_proj=v_sgu_proj, v_w_out=v_w_out, v_norm_ffn_w=v_norm_ffn_w, v_w_ffn_in=v_w_ffn_in, v_w_ffn_out=v_w_ffn_out, v_final_norm_w=v_final_norm_w)
    weights = {n: given[n] for n in TWIN_WEIGHTS}
    shared = {n: given[n] for n in SHARED_INPUTS}
    per_example = {n: given[n] for n in ['x']}
    grad_fn = _jax.value_and_grad(_loss, argnums=(0, 1))

    def one_microbatch(ex, loss_target):
        ex = dict(ex)
        diff = ex.pop(TWIN_DIFF_INPUT)
        return grad_fn(weights, diff, {**shared, **ex}, loss_target)

    if N_MICROBATCH == 1:
        loss, (grad_w, grad_x) = one_microbatch(per_example, given["loss_target"])
    else:
        def body(carry, xs):
            loss_sum, grad_sum = carry
            l_k, (gw_k, gx_k) = one_microbatch(xs[0], xs[1])
            with _jax.named_scope("update"):
                return (loss_sum + l_k, _jax.tree.map(_jnp.add, grad_sum, gw_k)), gx_k

        init = (_jnp.zeros((), _jnp.float32), _jax.tree.map(_jnp.zeros_like, weights))
        (loss, grad_w), grad_x = _jax.lax.scan(body, init, (per_example, given["loss_target"]))
    with _jax.named_scope("update"):
        delta_w, new_m, new_v = {}, {}, {}
        for n in TWIN_WEIGHTS:
            delta_w[n], new_m[n], new_v[n] = _adamw(weights[n], grad_w[n], given["m_" + n], given["v_" + n])
    return (loss, grad_x, *[grad_w[n] for n in TWIN_WEIGHTS], *[delta_w[n] for n in TWIN_WEIGHTS],
            *[new_m[n] for n in TWIN_WEIGHTS], *[new_v[n] for n in TWIN_WEIGHTS])
```

```python
import functools
import math

import jax
import jax.numpy as jnp
from jax import lax
from jax.experimental import pallas as pl
from jax.experimental.pallas import tpu as pltpu

F32, BF16 = jnp.float32, jnp.bfloat16

D = 2048
HEADS, DK, DV = 8, 128, 256
GROUPS, SLEN = 8, 128
SCH = D // GROUPS
DFF = 5632
CHUNK = 64
IN_COLS = 14336
NDEV = 8
SH_IN = IN_COLS // NDEV
SH_FF = 2 * DFF // NDEV
EPS = 1e-6
ROPE_BASE = 10000.0
RL = 256
Z_G, Z_SU, Z_SV, Z_GA, Z_GB = 2, 3, 4, 5, 6

ADAM_TILE_ELEMS = 256 * 1024
ADAM_LR, ADAM_B1, ADAM_B2, ADAM_EPS, ADAM_WD, ADAM_STEP = 0.001, 0.9, 0.999, 1e-08, 0.01, 10

VMEM_BYTES_V7X = 64 << 20
MESH_T = pl.DeviceIdType.MESH
ANY = pl.BlockSpec(memory_space=pl.ANY)


def _cparams(sem, vmem_mb):
    assert (vmem_mb << 20) < VMEM_BYTES_V7X
    return pltpu.CompilerParams(dimension_semantics=sem, vmem_limit_bytes=vmem_mb << 20)


def _mm(name, a, b, *, grid, a_spec, b_spec, o_spec, o_block, out_shape, out_dtype, contract, res=None, vmem_mb=48):
    nk = grid[2]
    dn = (contract, ((), ()))
    has_res = res is not None

    def body(*refs):
        if has_res:
            a_ref, b_ref, r_ref, o_ref, acc_ref = refs
        else:
            a_ref, b_ref, o_ref, acc_ref = refs
        part = lax.dot_general(a_ref[...].astype(BF16), b_ref[...].astype(BF16), dn, preferred_element_type=F32)

        def finish(val):
            if has_res:
                val = val + r_ref[...].astype(F32)
            o_ref[...] = val.astype(o_ref.dtype)

        if nk == 1:
            finish(part)
        else:
            k = pl.program_id(2)

            @pl.when(k == 0)
            def _():
                acc_ref[...] = part

            @pl.when(k > 0)
            def _():
                acc_ref[...] += part

            @pl.when(k == nk - 1)
            def _():
                finish(acc_ref[...])

    ins = [a, b] + ([res] if has_res else [])
    in_specs = [a_spec, b_spec] + ([o_spec] if has_res else [])
    return pl.pallas_call(
        body, name=name, grid=grid, in_specs=in_specs, out_specs=o_spec,
        out_shape=jax.ShapeDtypeStruct(out_shape, out_dtype),
        scratch_shapes=[pltpu.VMEM(o_block if nk > 1 else (8, 128), F32)],
        compiler_params=_cparams(("parallel", "parallel", "arbitrary"), vmem_mb),
    )(*ins)


def _tile(n, want):
    t = min(n, want)
    assert n % t == 0, (n, want)
    return t


def _mm_nn(name, a, b, *, out_dtype, tm=1024, tn=1024, tk=None, res=None):
    (m, k), n = a.shape, b.shape[1]
    tm, tn, tk = _tile(m, tm), _tile(n, tn), k if tk is None else tk
    return _mm(name, a, b, grid=(m // tm, n // tn, k // tk),
               a_spec=pl.BlockSpec((tm, tk), lambda i, j, kk: (i, kk)),
               b_spec=pl.BlockSpec((tk, tn), lambda i, j, kk: (kk, j)),
               o_spec=pl.BlockSpec((tm, tn), lambda i, j, kk: (i, j)), o_block=(tm, tn),
               out_shape=(m, n), out_dtype=out_dtype, contract=((1,), (0,)), res=res)


def _mm_nt(name, a, b, *, out_dtype, tm=1024, tn=1024, tk=None):
    (m, k), n = a.shape, b.shape[0]
    tm, tn, tk = _tile(m, tm), _tile(n, tn), k if tk is None else tk
    return _mm(name, a, b, grid=(m // tm, n // tn, k // tk),
               a_spec=pl.BlockSpec((tm, tk), lambda i, j, kk: (i, kk)),
               b_spec=pl.BlockSpec((tn, tk), lambda i, j, kk: (j, kk)),
               o_spec=pl.BlockSpec((tm, tn), lambda i, j, kk: (i, j)), o_block=(tm, tn),
               out_shape=(m, n), out_dtype=out_dtype, contract=((1,), (1,)))


def _mm_tn(name, a, b, *, out_dtype, tm=1024, tn=1024, tk=1024):
    (k, m), n = a.shape, b.shape[1]
    tm, tn, tk = _tile(m, tm), _tile(n, tn), _tile(k, tk)
    return _mm(name, a, b, grid=(m // tm, n // tn, k // tk),
               a_spec=pl.BlockSpec((tk, tm), lambda i, j, kk: (kk, i)),
               b_spec=pl.BlockSpec((tk, tn), lambda i, j, kk: (kk, j)),
               o_spec=pl.BlockSpec((tm, tn), lambda i, j, kk: (i, j)), o_block=(tm, tn),
               out_shape=(m, n), out_dtype=out_dtype, contract=((0,), (0,)))


def _mm_nn_sh(name, a, bsh, *, out_dtype, tm=1024):
    (m, k), ns = a.shape, bsh.shape[2]
    tm = _tile(m, tm)
    return _mm(name, a, bsh, grid=(m // tm, NDEV, 1),
               a_spec=pl.BlockSpec((tm, k), lambda i, j, kk: (i, 0)),
               b_spec=pl.BlockSpec((None, k, ns), lambda i, j, kk: (j, 0, 0)),
               o_spec=pl.BlockSpec((tm, ns), lambda i, j, kk: (i, j)), o_block=(tm, ns),
               out_shape=(m, NDEV * ns), out_dtype=out_dtype, contract=((1,), (0,)), vmem_mb=56)


def _mm_nt_sh(name, a, bsh, *, out_dtype, tm=1024, tn=1024):
    m, (_, n, ns) = a.shape[0], bsh.shape
    tm, tn = _tile(m, tm), _tile(n, tn)
    return _mm(name, a, bsh, grid=(m // tm, n // tn, NDEV),
               a_spec=pl.BlockSpec((tm, ns), lambda i, j, kk: (i, kk)),
               b_spec=pl.BlockSpec((None, tn, ns), lambda i, j, kk: (kk, j, 0)),
               o_spec=pl.BlockSpec((tm, tn), lambda i, j, kk: (i, j)), o_block=(tm, tn),
               out_shape=(m, n), out_dtype=out_dtype, contract=((1,), (1,)))


def _mm_tn_sh(name, a, b, ns, *, out_dtype, tm=1024, tk=1024):
    k, m = a.shape
    tm, tk = _tile(m, tm), _tile(k, tk)
    return _mm(name, a, b, grid=(m // tm, NDEV, k // tk),
               a_spec=pl.BlockSpec((tk, tm), lambda i, j, kk: (kk, i)),
               b_spec=pl.BlockSpec((tk, ns), lambda i, j, kk: (kk, j)),
               o_spec=pl.BlockSpec((None, tm, ns), lambda i, j, kk: (j, i, 0)), o_block=(tm, ns),
               out_shape=(NDEV, m, ns), out_dtype=out_dtype, contract=((0,), (0,)))


def _tilemap(name, fn, tiled_ins, full_ins, tiled_outs, acc_shapes, *, rows, tm, ncol=1, vmem_mb=48):
    n_ti, n_fi, n_to, n_ao = len(tiled_ins), len(full_ins), len(tiled_outs), len(acc_shapes)
    assert ncol == 1 or n_ao == 0

    def body(*refs):
        t_in, f_in = refs[:n_ti], refs[n_ti:n_ti + n_fi]
        t_out, a_out = refs[n_ti + n_fi:n_ti + n_fi + n_to], refs[n_ti + n_fi + n_to:]
        outs = fn(*[r[...] for r in t_in], *[r[...] for r in f_in])
        for r, v in zip(t_out, outs[:n_to]):
            r[...] = v.astype(r.dtype)
        if n_ao:
            i = pl.program_id(1)

            @pl.when(i == 0)
            def _():
                for r, v in zip(a_out, outs[n_to:]):
                    r[...] = v

            @pl.when(i > 0)
            def _():
                for r, v in zip(a_out, outs[n_to:]):
                    r[...] += v

    in_specs = [pl.BlockSpec((tm, bw), lambda j, i, base=base: (i, base + j)) for (_, bw, base) in tiled_ins]
    in_specs += [pl.BlockSpec(a.shape, lambda j, i, nd=a.ndim: (0,) * nd) for a in full_ins]
    out_specs = [pl.BlockSpec((tm, bw), lambda j, i: (i, j)) for (_, bw, _) in tiled_outs]
    out_specs += [pl.BlockSpec(s, lambda j, i: (0, 0)) for s in acc_shapes]
    out_shape = [jax.ShapeDtypeStruct((rows, w), dt) for (w, _, dt) in tiled_outs]
    out_shape += [jax.ShapeDtypeStruct(s, F32) for s in acc_shapes]
    return pl.pallas_call(
        body, name=name, grid=(ncol, rows // tm), in_specs=in_specs, out_specs=out_specs, out_shape=out_shape,
        compiler_params=_cparams(("parallel", "arbitrary"), vmem_mb),
    )(*[t[0] for t in tiled_ins], *full_ins)


def _f_rms(x, w):
    return x * lax.rsqrt(jnp.mean(x * x, axis=-1, keepdims=True) + EPS) * w


def _rms_fwd(name, x, w):
    rows = x.shape[0]
    return _tilemap(name, lambda xv, wv: (_f_rms(xv, wv),), [(x, D, 0)], [w], [(D, D, BF16)], [],
                    rows=rows, tm=_tile(rows, 256))[0]


def _rms_bwd(name, x, w, dh, dres):
    rows = x.shape[0]

    def fn(xv, dhv, drv, wv):
        _, vjp = jax.vjp(_f_rms, xv, wv)
        dx, dw = vjp(dhv.astype(F32))
        return dx + drv, dw

    return _tilemap(name, fn, [(x, D, 0), (dh, D, 0), (dres, D, 0)], [w], [(D, D, F32)], [(1, D)],
                    rows=rows, tm=_tile(rows, 256))


def _f_mixa_head(g, o, gw):
    mu = jnp.mean(o, axis=-1, keepdims=True)
    d = o - mu
    var = jnp.mean(d * d, axis=-1, keepdims=True)
    return g * jax.nn.sigmoid(g) * (d * lax.rsqrt(var + EPS) * gw)


def _heads(width):
    return [slice(h * width, (h + 1) * width) for h in range(HEADS)]


def _mixa_fwd(name, z, o, gw):
    rows = z.shape[0]

    def fn(g, ov, gwv):
        g, ov = g.astype(F32), ov.astype(F32)
        return (jnp.concatenate([_f_mixa_head(g[:, s], ov[:, s], gwv[:, s]) for s in _heads(DV)], axis=1),)

    return _tilemap(name, fn, [(z, D, Z_G), (o, D, 0)], [gw], [(D, D, BF16)], [], rows=rows, tm=_tile(rows, 256))[0]


def _mixa_bwd(name, z, o, gw, da):
    rows = z.shape[0]

    def fn(g, ov, dav, gwv):
        g, ov, dav = g.astype(F32), ov.astype(F32), dav.astype(F32)
        parts = []
        for s in _heads(DV):
            _, vjp = jax.vjp(_f_mixa_head, g[:, s], ov[:, s], gwv[:, s])
            parts.append(vjp(dav[:, s]))
        return tuple(jnp.concatenate([p[i] for p in parts], axis=1) for i in range(3))

    return _tilemap(name, fn, [(z, D, Z_G), (o, D, 0), (da, D, 0)], [gw], [(D, D, BF16), (D, D, BF16)], [(1, D)],
                    rows=rows, tm=_tile(rows, 256))


def _f_merge(ga, gb, a, b):
    return jax.nn.sigmoid(ga) * a + jax.nn.sigmoid(gb) * b


def _merge_fwd(name, z, a, b):
    rows, bw = z.shape[0], 1024

    def fn(ga, gb, av, bv):
        return (_f_merge(ga.astype(F32), gb.astype(F32), av.astype(F32), bv.astype(F32)),)

    return _tilemap(name, fn, [(z, bw, Z_GA * 2), (z, bw, Z_GB * 2), (a, bw, 0), (b, bw, 0)], [], [(D, bw, BF16)], [],
                    rows=rows, tm=_tile(rows, 512), ncol=D // bw)[0]


def _merge_bwd(name, z, a, b, dm):
    rows, bw = z.shape[0], 1024

    def fn(ga, gb, av, bv, dmv):
        _, vjp = jax.vjp(_f_merge, ga.astype(F32), gb.astype(F32), av.astype(F32), bv.astype(F32))
        return vjp(dmv.astype(F32))

    return _tilemap(name, fn, [(z, bw, Z_GA * 2), (z, bw, Z_GB * 2), (a, bw, 0), (b, bw, 0), (dm, bw, 0)], [],
                    [(D, bw, BF16)] * 4, [], rows=rows, tm=_tile(rows, 512), ncol=D // bw)


def _f_swiglu(a, c):
    return a * jax.nn.sigmoid(a) * c


def _swiglu_fwd(name, ac):
    rows = ac.shape[0]

    def fn(a, c):
        return (_f_swiglu(a.astype(F32), c.astype(F32)),)

    return _tilemap(name, fn, [(ac, SH_FF, 0), (ac, SH_FF, 4)], [], [(DFF, SH_FF, BF16)], [],
                    rows=rows, tm=_tile(rows, 512), ncol=4)[0]


def _swiglu_bwd(name, ac, df):
    rows = ac.shape[0]

    def fn(a, c, dfv):
        _, vjp = jax.vjp(_f_swiglu, a.astype(F32), c.astype(F32))
        return vjp(dfv.astype(F32))

    return _tilemap(name, fn, [(ac, SH_FF, 0), (ac, SH_FF, 4), (df, SH_FF, 0)], [], [(DFF, SH_FF, BF16)] * 2, [],
                    rows=rows, tm=_tile(rows, 512), ncol=4)


def _loss_bwd(name, x, w, target):
    rows = x.shape[0]

    def fn(xv, tv, wv):
        y, vjp = jax.vjp(_f_rms, xv, wv)
        e = y - tv
        dx, dw = vjp(e * (1.0 / D))
        per_row = jnp.mean(e * e, axis=-1, keepdims=True)
        return dx, dw, jnp.broadcast_to(0.5 * jnp.sum(per_row, axis=0, keepdims=True), (1, 128))

    return _tilemap(name, fn, [(x, D, 0), (target, D, 0)], [w], [(D, D, F32)], [(1, D), (1, 128)],
                    rows=rows, tm=_tile(rows, 256))


def _gelu(v):
    return 0.5 * v * (1.0 + lax.erf(v * (1.0 / math.sqrt(2.0))))


def _f_sgu_pre(su, sv, lw, lb):
    zv = _gelu(sv)
    mu = jnp.mean(zv, axis=-1, keepdims=True)
    d = zv - mu
    var = jnp.mean(d * d, axis=-1, keepdims=True)
    return _gelu(su), d * lax.rsqrt(var + EPS) * lw + lb


def _sgu_blocks(tm):
    return [slice(r * SLEN, (r + 1) * SLEN) for r in range(tm // SLEN)]


def _sgu_mix(wm, vnb, bexp, tm):
    rows = []
    for r in _sgu_blocks(tm):
        cols = [jnp.dot(wm[g], vnb[r, g * SCH:(g + 1) * SCH], preferred_element_type=F32) for g in range(GROUPS)]
        rows.append(jnp.concatenate(cols, axis=1) + bexp)
    return jnp.concatenate(rows, axis=0)


def _sgu_fwd(name, z, lw, lb, wm, bexp):
    rows = z.shape[0]
    tm = _tile(rows, 256)

    def fn(su, sv, lwv, lbv, wmv, bev):
        zu, vn = _f_sgu_pre(su.astype(F32), sv.astype(F32), lwv, lbv)
        return (zu * _sgu_mix(wmv, vn.astype(BF16), bev, tm),)

    return _tilemap(name, fn, [(z, D, Z_SU), (z, D, Z_SV)], [lw, lb, wm, bexp], [(D, D, BF16)], [], rows=rows, tm=tm)[0]


def _sgu_bwd(name, z, lw, lb, wm, wmt, bexp, maskf, db):
    rows = z.shape[0]
    tm = _tile(rows, 256)

    def fn(su, sv, dbv, lwv, lbv, wmv, wmtv, bev, mkv):
        (zu, vn), vjp = jax.vjp(_f_sgu_pre, su.astype(F32), sv.astype(F32), lwv, lbv)
        vnb = vn.astype(BF16)
        dbv = dbv.astype(F32)
        dmixed = dbv * zu
        dzu = dbv * _sgu_mix(wmv, vnb, bev, tm)
        dmb = dmixed.astype(BF16)
        dvn_rows, dws, dbf = [], [None] * GROUPS, None
        for r in _sgu_blocks(tm):
            cols = []
            for g in range(GROUPS):
                cs = slice(g * SCH, (g + 1) * SCH)
                cols.append(jnp.dot(wmtv[g], dmb[r, cs], preferred_element_type=F32))
                dw = lax.dot_general(dmb[r, cs], vnb[r, cs], (((1,), (1,)), ((), ())), preferred_element_type=F32)
                dws[g] = dw if dws[g] is None else dws[g] + dw
            dvn_rows.append(jnp.concatenate(cols, axis=1))
            dbf = dmixed[r] if dbf is None else dbf + dmixed[r]
        dsu, dsv, dlw, dlb = vjp((dzu, jnp.concatenate(dvn_rows, axis=0)))
        dws_all = jnp.concatenate([dw * mkv for dw in dws], axis=0)
        return dsu, dsv, dws_all, dbf, dlw, dlb

    return _tilemap(name, fn, [(z, D, Z_SU), (z, D, Z_SV), (db, D, 0)], [lw, lb, wm, wmt, bexp, maskf],
                    [(D, D, BF16), (D, D, BF16)], [(GROUPS * SLEN, SLEN), (SLEN, D), (1, D), (1, D)], rows=rows, tm=tm)


def _group_sums(name, dbf):
    def body(e_ref, x_ref, o_ref):
        o_ref[...] = lax.dot_general(e_ref[...], x_ref[...], (((1,), (1,)), ((), ())),
                                     precision=lax.Precision.HIGHEST, preferred_element_type=F32)

    ind = (jnp.arange(D)[None, :] // SCH == jnp.arange(GROUPS)[:, None]).astype(F32)
    return pl.pallas_call(body, name=name, out_shape=jax.ShapeDtypeStruct((GROUPS, SLEN), F32))(ind, dbf)


def _ret_consts(rows):
    hh = jnp.arange(HEADS, dtype=F32)
    log_g = jnp.log1p(-(2.0 ** (-5.0 - hh)))
    idx = jnp.arange(RL, dtype=F32)
    ci = jnp.arange(RL) // CHUNK
    mask = ci[None, :] <= ci[:, None]
    dm = jnp.where(mask[None], jnp.exp(log_g[:, None, None] * jnp.abs(idx[:, None] - idx[None, :])), 0.0)
    qd = jnp.broadcast_to(jnp.exp(log_g[:, None] * (idx[None, :] + 1.0))[:, :, None], (HEADS, RL, DK))
    kd = jnp.broadcast_to(jnp.exp(log_g[:, None] * (RL - 1.0 - idx[None, :]))[:, :, None], (HEADS, RL, DK))
    cd = jnp.broadcast_to(jnp.exp(log_g * RL)[:, None, None], (HEADS, DK, DV))
    half = DK // 2
    inv = ROPE_BASE ** (-jnp.arange(half, dtype=F32) / half)
    ang = jnp.arange(rows, dtype=jnp.int32).astype(F32)[:, None] * inv[None, :]
    cos, sin = jnp.cos(ang), jnp.sin(ang)
    return dict(dm=dm.astype(F32), qd=qd, kd=kd, cd=cd, cos=jnp.concatenate([cos, cos], axis=1),
                sin=jnp.concatenate([-sin, sin], axis=1))


def _rope(x, cos, sin_signed):
    return x * cos + pltpu.roll(x, DK // 2, 1) * sin_signed


def _rope_t(d, cos, sin_signed):
    return d * cos + pltpu.roll(d * sin_signed, DK // 2, 1)


def _dot(a, b, ca, cb):
    return lax.dot_general(a, b, (((ca,), (cb,)), ((), ())), preferred_element_type=F32)


def _ret_specs(nb, rev):
    blk = (lambda n: nb - 1 - n) if rev else (lambda n: n)
    return dict(
        q=pl.BlockSpec((RL, DK), lambda h, n: (blk(n), h)),
        k=pl.BlockSpec((RL, DK), lambda h, n: (blk(n), HEADS + h)),
        v=pl.BlockSpec((RL, DV), lambda h, n: (blk(n), HEADS + h)),
        o=pl.BlockSpec((RL, DV), lambda h, n: (blk(n), h)),
        rope=pl.BlockSpec((RL, DK), lambda h, n: (blk(n), 0)),
        dm=pl.BlockSpec((None, RL, RL), lambda h, n: (h, 0, 0)),
        qd=pl.BlockSpec((None, RL, DK), lambda h, n: (h, 0, 0)),
        cd=pl.BlockSpec((None, DK, DV), lambda h, n: (h, 0, 0)),
        st=pl.BlockSpec((None, None, DK, DV), lambda h, n: (blk(n), h, 0, 0)),
    )


def _ret_fwd(name, z, rc):
    rows = z.shape[0]
    nb = rows // RL
    sp = _ret_specs(nb, False)

    def body(q_ref, k_ref, v_ref, cos_ref, sin_ref, dm_ref, qd_ref, kd_ref, cd_ref, o_ref, st_ref, state):
        @pl.when(pl.program_id(1) == 0)
        def _():
            state[...] = jnp.zeros_like(state)

        cos, sin = cos_ref[...], sin_ref[...]
        qf = _rope(q_ref[...].astype(F32), cos, sin) * (DK ** -0.5)
        kf = _rope(k_ref[...].astype(F32), cos, sin)
        vb = v_ref[...]
        s0 = state[...]
        s0b = s0.astype(BF16)
        st_ref[...] = s0b
        sc = _dot(qf.astype(BF16), kf.astype(BF16), 1, 1) * dm_ref[...]
        o = _dot(sc.astype(BF16), vb, 1, 0) + _dot((qf * qd_ref[...]).astype(BF16), s0b, 1, 0)
        o_ref[...] = o.astype(o_ref.dtype)
        state[...] = cd_ref[...] * s0 + _dot((kf * kd_ref[...]).astype(BF16), vb, 0, 0)

    return pl.pallas_call(
        body, name=name, grid=(HEADS, nb),
        in_specs=[sp["q"], sp["k"], sp["v"], sp["rope"], sp["rope"], sp["dm"], sp["qd"], sp["qd"], sp["cd"]],
        out_specs=[sp["o"], sp["st"]],
        out_shape=[jax.ShapeDtypeStruct((rows, HEADS * DV), BF16), jax.ShapeDtypeStruct((nb, HEADS, DK, DV), BF16)],
        scratch_shapes=[pltpu.VMEM((DK, DV), F32)],
        compiler_params=_cparams(("parallel", "arbitrary"), 32),
    )(z, z, z, rc["cos"], rc["sin"], rc["dm"], rc["qd"], rc["kd"], rc["cd"])


def _ret_bwd(name, z, rc, states, do):
    rows = z.shape[0]
    nb = rows // RL
    sp = _ret_specs(nb, True)

    def body(q_ref, k_ref, v_ref, cos_ref, sin_ref, dm_ref, qd_ref, kd_ref, cd_ref, st_ref, do_ref,
             dq_ref, dk_ref, dv_ref, dstate):
        @pl.when(pl.program_id(1) == 0)
        def _():
            dstate[...] = jnp.zeros_like(dstate)

        cos, sin, dm, qd, kd = cos_ref[...], sin_ref[...], dm_ref[...], qd_ref[...], kd_ref[...]
        qf = _rope(q_ref[...].astype(F32), cos, sin) * (DK ** -0.5)
        kf = _rope(k_ref[...].astype(F32), cos, sin)
        qb, kb, vb, dob, s0b = qf.astype(BF16), kf.astype(BF16), v_ref[...], do_ref[...], st_ref[...]
        qsb, ksb = (qf * qd).astype(BF16), (kf * kd).astype(BF16)
        ds1 = dstate[...]
        ds1b = ds1.astype(BF16)
        scb = (_dot(qb, kb, 1, 1) * dm).astype(BF16)
        dscb = (_dot(dob, vb, 1, 1) * dm).astype(BF16)
        dq = _dot(dscb, kb, 1, 0) + _dot(dob, s0b, 1, 1) * qd
        dk = _dot(dscb, qb, 0, 0) + _dot(vb, ds1b, 1, 1) * kd
        dv = _dot(scb, dob, 0, 0) + _dot(ksb, ds1b, 1, 0)
        dq_ref[...] = _rope_t(dq * (DK ** -0.5), cos, sin).astype(dq_ref.dtype)
        dk_ref[...] = _rope_t(dk, cos, sin).astype(dk_ref.dtype)
        dv_ref[...] = dv.astype(dv_ref.dtype)
        dstate[...] = cd_ref[...] * ds1 + _dot(qsb, dob, 0, 0)

    dqk = pl.BlockSpec((RL, DK), lambda h, n: (nb - 1 - n, h))
    return pl.pallas_call(
        body, name=name, grid=(HEADS, nb),
        in_specs=[sp["q"], sp["k"], sp["v"], sp["rope"], sp["rope"], sp["dm"], sp["qd"], sp["qd"], sp["cd"], sp["st"], sp["o"]],
        out_specs=[dqk, dqk, sp["o"]],
        out_shape=[jax.ShapeDtypeStruct((rows, HEADS * DK), BF16)] * 2 + [jax.ShapeDtypeStruct((rows, HEADS * DV), BF16)],
        scratch_shapes=[pltpu.VMEM((DK, DV), F32)],
        compiler_params=_cparams(("parallel", "arbitrary"), 32),
    )(z, z, z, rc["cos"], rc["sin"], rc["dm"], rc["qd"], rc["kd"], rc["cd"], states, do)


def _layer_fwd(l, x, p, rc):
    n = f"l{l}_"
    h = _rms_fwd(n + "rms1", x, p["norm_mix_w"])
    z = _mm_nn_sh(n + "in_proj", h, p["w_in"], out_dtype=BF16)
    o, states = _ret_fwd(n + "ret", z, rc)
    a_in = _mixa_fwd(n + "mixa", z, o, p["ret_gn_w"])
    b_in = _sgu_fwd(n + "sgu", z, p["sgu_ln_w"], p["sgu_ln_b"], p["wm"], p["bexp"])
    br_a = _mm_nn(n + "ret_proj", a_in, p["ret_proj"], out_dtype=BF16)
    br_b = _mm_nn(n + "sgu_proj", b_in, p["sgu_proj"], out_dtype=BF16)
    merged = _merge_fwd(n + "merge", z, br_a, br_b)
    x1 = _mm_nn(n + "out_proj", merged, p["w_out"], out_dtype=F32, res=x)
    h2 = _rms_fwd(n + "rms2", x1, p["norm_ffn_w"])
    ac = _mm_nn_sh(n + "ffn_in", h2, p["w_ffn_in"], out_dtype=BF16)
    f = _swiglu_fwd(n + "swiglu", ac)
    x2 = _mm_nn(n + "ffn_out", f, p["w_ffn_out"], out_dtype=F32, res=x1, tk=SH_FF)
    saved = dict(x=x, h=h, z=z, o=o, states=states, a_in=a_in, b_in=b_in, br_a=br_a, br_b=br_b, merged=merged,
                 x1=x1, h2=h2, ac=ac, f=f)
    return x2, saved


def _layer_bwd(l, dx2, p, s, rc):
    n = f"l{l}b_"
    df = _mm_nt(n + "d_f", dx2, p["w_ffn_out"], out_dtype=BF16, tn=SH_FF)
    g_ffn_out = _mm_tn(n + "g_ffn_out", s["f"], dx2, out_dtype=BF16, tm=SH_FF)
    da, dc = _swiglu_bwd(n + "swiglu", s["ac"], df)
    dac = jnp.concatenate([da, dc], axis=1)
    dh2 = _mm_nt_sh(n + "d_h2", dac, p["w_ffn_in"], out_dtype=BF16)
    g_ffn_in = _mm_tn_sh(n + "g_ffn_in", s["h2"], dac, SH_FF, out_dtype=BF16)
    dx1, g_nffn = _rms_bwd(n + "rms2", s["x1"], p["norm_ffn_w"], dh2, dx2)
    dmerged = _mm_nt(n + "d_merged", dx1, p["w_out"], out_dtype=BF16)
    g_out = _mm_tn(n + "g_out", s["merged"], dx1, out_dtype=BF16)
    dga, dgb, dbr_a, dbr_b = _merge_bwd(n + "merge", s["z"], s["br_a"], s["br_b"], dmerged)
    da_in = _mm_nt(n + "d_a_in", dbr_a, p["ret_proj"], out_dtype=BF16)
    g_ret_proj = _mm_tn(n + "g_ret_proj", s["a_in"], dbr_a, out_dtype=BF16)
    db_in = _mm_nt(n + "d_b_in", dbr_b, p["sgu_proj"], out_dtype=BF16)
    g_sgu_proj = _mm_tn(n + "g_sgu_proj", s["b_in"], dbr_b, out_dtype=BF16)
    dg, do, g_gn = _mixa_bwd(n + "mixa", s["z"], s["o"], p["ret_gn_w"], da_in)
    dq, dk, dv = _ret_bwd(n + "ret", s["z"], rc, s["states"], do)
    dsu, dsv, g_ws, dbf, g_lw, g_lb = _sgu_bwd(n + "sgu", s["z"], p["sgu_ln_w"], p["sgu_ln_b"], p["wm"], p["wmt"],
                                                 p["bexp"], p["maskf"], db_in)
    g_bs = _group_sums(n + "g_bs", dbf)
    dz = jnp.concatenate([dq, dk, dv, dg, dsu, dsv, dga, dgb], axis=1)
    dh = _mm_nt_sh(n + "d_h", dz, p["w_in"], out_dtype=BF16)
    g_in = _mm_tn_sh(n + "g_in", s["h"], dz, SH_IN, out_dtype=BF16)
    dx, g_nmix = _rms_bwd(n + "rms1", s["x"], p["norm_mix_w"], dh, dx1)
    big = dict(w_in=g_in, ret_proj=g_ret_proj.reshape(NDEV, -1, D), sgu_proj=g_sgu_proj.reshape(NDEV, -1, D),
               w_out=g_out.reshape(NDEV, -1, D), w_ffn_in=g_ffn_in, w_ffn_out=g_ffn_out.reshape(NDEV, -1, D))
    small = dict(norm_mix_w=g_nmix, ret_gn_w=g_gn, sgu_ln_w=g_lw, sgu_ln_b=g_lb, sgu_w_s=g_ws, sgu_b_s=g_bs,
                 norm_ffn_w=g_nffn)
    return dx, big, small


def _slot(ref, p):
    return ref.at[4 * p[0] + 2 * p[1] + p[2]]


def _allgather(name, shards):
    n = len(shards)

    def body(*refs):
        ins, outs = refs[:n], refs[n:2 * n]
        send_sems, recv_sems, local_sems = refs[2 * n:]
        x, y, c = lax.axis_index("x"), lax.axis_index("y"), lax.axis_index("c")
        me, sibling = (x, y, c), (x, y, 1 - c)
        chips = [(1 - x, y), (x, 1 - y), (1 - x, 1 - y)]

        def copy(a, k, block, to, src=None):
            return pltpu.make_async_remote_copy(
                src_ref=_slot(outs[a], block) if src is None else src, dst_ref=_slot(outs[a], block),
                send_sem=send_sems.at[a, k], recv_sem=recv_sems.at[a, k], device_id=to, device_id_type=MESH_T)

        mine = [pltpu.make_async_copy(ins[a], _slot(outs[a], me), local_sems.at[a]) for a in range(n)]
        for cp in mine:
            cp.start()
        first = []
        for a in range(n):
            first.append(copy(a, 0, me, sibling, src=ins[a]))
            first += [copy(a, 1 + j, me, (*chip, c), src=ins[a]) for j, chip in enumerate(chips)]
        for cp in first:
            cp.start()
        passed = []
        for j, chip in enumerate(chips):
            for a in range(n):
                copy(a, 1 + j, (*chip, c), me).wait_recv()
                cp = copy(a, 4 + j, (*chip, c), sibling)
                cp.start()
                passed.append(cp)
        for a in range(n):
            copy(a, 0, sibling, me).wait_recv()
            for j, chip in enumerate(chips):
                copy(a, 4 + j, (*chip, 1 - c), me).wait_recv()
        for cp in first + passed:
            cp.wait_send()
        for cp in mine:
            cp.wait()

    return pl.pallas_call(
        body, name=name, in_specs=[ANY] * n, out_specs=[ANY] * n,
        out_shape=[jax.ShapeDtypeStruct((NDEV,) + s.shape, s.dtype) for s in shards],
        scratch_shapes=[pltpu.SemaphoreType.DMA((n, 7)), pltpu.SemaphoreType.DMA((n, 7)), pltpu.SemaphoreType.DMA((n,))],
    )(*shards)


def _scatter_to_owners(name, parts):
    n = len(parts)

    def body(*refs):
        ins, outs = refs[:n], refs[n:2 * n]
        send_sems, recv_sems, local_sems = refs[2 * n:]
        x, y, c = lax.axis_index("x"), lax.axis_index("y"), lax.axis_index("c")
        me = (x, y, c)
        peers = [((1 - x) if fx else x, (1 - y) if fy else y, (1 - c) if fc else c)
                 for fx in (0, 1) for fy in (0, 1) for fc in (0, 1) if fx + fy + fc]

        def copy(a, k, peer):
            return pltpu.make_async_remote_copy(
                src_ref=_slot(ins[a], peer), dst_ref=_slot(outs[a], me),
                send_sem=send_sems.at[a, k], recv_sem=recv_sems.at[a, k], device_id=peer, device_id_type=MESH_T)

        def arrival(a, k, peer):
            return pltpu.make_async_remote_copy(
                src_ref=_slot(ins[a], peer), dst_ref=_slot(outs[a], peer),
                send_sem=send_sems.at[a, k], recv_sem=recv_sems.at[a, k], device_id=peer, device_id_type=MESH_T)

        mine = [pltpu.make_async_copy(_slot(ins[a], me), _slot(outs[a], me), local_sems.at[a]) for a in range(n)]
        for cp in mine:
            cp.start()
        sent = [copy(a, k, peer) for a in range(n) for k, peer in enumerate(peers)]
        for cp in sent:
            cp.start()
        for a in range(n):
            for k, peer in enumerate(peers):
                arrival(a, k, peer).wait_recv()
        for cp in sent:
            cp.wait_send()
        for cp in mine:
            cp.wait()

    return pl.pallas_call(
        body, name=name, in_specs=[ANY] * n, out_specs=[ANY] * n,
        out_shape=[jax.ShapeDtypeStruct(s.shape, s.dtype) for s in parts],
        scratch_shapes=[pltpu.SemaphoreType.DMA((n, 7)), pltpu.SemaphoreType.DMA((n, 7)), pltpu.SemaphoreType.DMA((n,))],
    )(*parts)


def _adamw(w, g, m, v):
    m = ADAM_B1 * m + (1.0 - ADAM_B1) * g
    v = ADAM_B2 * v + (1.0 - ADAM_B2) * (g * g)
    m_hat = m / (1.0 - ADAM_B1 ** ADAM_STEP)
    v_hat = v / (1.0 - ADAM_B2 ** ADAM_STEP)
    return -ADAM_LR * (m_hat / (jnp.sqrt(v_hat) + ADAM_EPS) + ADAM_WD * w), m, v


def _sum_and_adamw(name, recv, w, m, v, layer, prev):
    _, r, c = recv.shape
    tr = max(t for t in range(16, r + 1, 16) if r % t == 0 and t * c <= ADAM_TILE_ELEMS)

    def body(recv_ref, w_ref, m_ref, v_ref, *rest):
        g_ref, d_ref, m_out, v_out = rest[4:]
        g = recv_ref[0].astype(F32)
        for s in range(1, NDEV):
            g = g + recv_ref[s].astype(F32)
        delta, m_new, v_new = _adamw(w_ref[...], g, m_ref[...], v_ref[...])
        g_ref[...] = g
        d_ref[...] = delta
        m_out[...] = m_new
        v_out[...] = v_new

    lay = pl.BlockSpec((None, tr, c), lambda i: (layer, i, 0))
    return pl.pallas_call(
        body, name=name, grid=(r // tr,),
        in_specs=[pl.BlockSpec((NDEV, tr, c), lambda i: (0, i, 0)), lay, lay, lay] + [ANY] * 4,
        out_specs=[lay] * 4, out_shape=[jax.ShapeDtypeStruct(w.shape, F32)] * 4,
        input_output_aliases={4: 0, 5: 1, 6: 2, 7: 3},
        compiler_params=_cparams(("arbitrary",), 48),
    )(recv, w, m, v, *prev)


def _small_sum_and_adamw(name, recv, w, m, v):
    r = w.shape[0]

    def body(recv_ref, w_ref, m_ref, v_ref, g_ref, d_ref, m_out, v_out):
        g = recv_ref[0]
        for s in range(1, NDEV):
            g = g + recv_ref[s]
        delta, m_new, v_new = _adamw(w_ref[...], g, m_ref[...], v_ref[...])
        g_ref[...] = g
        d_ref[...] = delta
        m_out[...] = m_new
        v_out[...] = v_new

    return pl.pallas_call(body, name=name, out_shape=[jax.ShapeDtypeStruct((r, 128), F32)] * 4,
                          compiler_params=_cparams(None, 48))(recv, w, m, v)


BIG = ("w_in", "ret_proj", "sgu_proj", "w_out", "w_ffn_in", "w_ffn_out")
SMALL = ("norm_mix_w", "ret_gn_w", "sgu_ln_w", "sgu_ln_b", "sgu_w_s", "sgu_b_s", "norm_ffn_w", "final_norm_w")
WEIGHTS = ("norm_mix_w", "w_in", "ret_gn_w", "ret_proj", "sgu_ln_w", "sgu_ln_b", "sgu_w_s", "sgu_b_s", "sgu_proj",
           "w_out", "norm_ffn_w", "w_ffn_in", "w_ffn_out", "final_norm_w")


def _pack_small(d):
    return jnp.concatenate([d[k].reshape(-1, 128) for k in SMALL], axis=0)


def _unpack_small(packed, like):
    out, r0 = {}, 0
    for k in SMALL:
        r = like[k].size // 128
        out[k] = packed[r0:r0 + r].reshape(like[k].shape)
        r0 += r
    return out


def _step(w, m, v, x, target):
    depth = w["w_in"].shape[0]
    rows = x.shape[0]
    rc = _ret_consts(rows)
    pos = jnp.arange(SLEN)
    mask = (pos[None, :] // CHUNK) <= (pos[:, None] // CHUNK)
    maskf = mask.astype(F32)

    layers = []
    for l in range(depth):
        gathered = _allgather(f"gather_l{l}", [w[k][l].astype(BF16) for k in BIG])
        p = dict(zip(BIG, gathered))
        for k in ("ret_proj", "sgu_proj", "w_out", "w_ffn_out"):
            p[k] = p[k].reshape(-1, D)
        for k in ("norm_mix_w", "ret_gn_w", "sgu_ln_w", "sgu_ln_b", "norm_ffn_w"):
            p[k] = w[k][l][None, :]
        wm = jnp.where(mask[None], w["sgu_w_s"][l], 0.0)
        p["wm"], p["wmt"] = wm.astype(BF16), jnp.swapaxes(wm, 1, 2).astype(BF16)
        p["bexp"] = jnp.repeat(w["sgu_b_s"][l].T, SCH, axis=1)
        p["maskf"] = maskf
        layers.append(p)

    saved = []
    for l in range(depth):
        x, s = _layer_fwd(l, x, layers[l], rc)
        saved.append(s)

    dx, g_final, loss_row = _loss_bwd("loss", x, w["final_norm_w"][None, :], target)
    loss = lax.psum(loss_row[0, 0], ("x", "y", "c"))

    res = {k: [jnp.zeros(w[k].shape, F32) for _ in range(4)] for k in BIG}
    small = {k: [None] * depth for k in SMALL if k != "final_norm_w"}
    for l in reversed(range(depth)):
        dx, big, sm = _layer_bwd(l, dx, layers[l], saved[l], rc)
        recv = _scatter_to_owners(f"scatter_l{l}", [big[k] for k in BIG])
        for k, rv in zip(BIG, recv):
            res[k] = _sum_and_adamw(f"adamw_{k}_l{l}", rv, w[k], m[k], v[k], l, res[k])
        for k in small:
            small[k][l] = sm[k]

    g_small = {k: jnp.stack(small[k]).reshape(w[k].shape) for k in small}
    g_small["final_norm_w"] = g_final.reshape(w["final_norm_w"].shape)
    recv = _allgather("gather_small", [_pack_small(g_small)])[0]
    sres = _small_sum_and_adamw("adamw_small", recv, _pack_small(w), _pack_small(m), _pack_small(v))
    sres = [_unpack_small(a, w) for a in sres]

    outs = []
    for i in range(4):
        outs.append([sres[i][k] if k in SMALL else res[k][i] for k in WEIGHTS])
    return loss, dx, outs


def kernel(x, norm_mix_w, w_in, ret_gn_w, ret_proj, sgu_ln_w, sgu_ln_b, sgu_w_s, sgu_b_s, sgu_proj, w_out, norm_ffn_w, w_ffn_in, w_ffn_out, final_norm_w, loss_target, m_norm_mix_w, m_w_in, m_ret_gn_w, m_ret_proj, m_sgu_ln_w, m_sgu_ln_b, m_sgu_w_s, m_sgu_b_s, m_sgu_proj, m_w_out, m_norm_ffn_w, m_w_ffn_in, m_w_ffn_out, m_final_norm_w, v_norm_mix_w, v_w_in, v_ret_gn_w, v_ret_proj, v_sgu_ln_w, v_sgu_ln_b, v_sgu_w_s, v_sgu_b_s, v_sgu_proj, v_w_out, v_norm_ffn_w, v_w_ffn_in, v_w_ffn_out, v_final_norm_w):
    w = dict(zip(WEIGHTS, (norm_mix_w, w_in, ret_gn_w, ret_proj, sgu_ln_w, sgu_ln_b, sgu_w_s, sgu_b_s, sgu_proj, w_out,
                           norm_ffn_w, w_ffn_in, w_ffn_out, final_norm_w)))
    m = dict(zip(WEIGHTS, (m_norm_mix_w, m_w_in, m_ret_gn_w, m_ret_proj, m_sgu_ln_w, m_sgu_ln_b, m_sgu_w_s, m_sgu_b_s,
                           m_sgu_proj, m_w_out, m_norm_ffn_w, m_w_ffn_in, m_w_ffn_out, m_final_norm_w)))
    v = dict(zip(WEIGHTS, (v_norm_mix_w, v_w_in, v_ret_gn_w, v_ret_proj, v_sgu_ln_w, v_sgu_ln_b, v_sgu_w_s, v_sgu_b_s,
                           v_sgu_proj, v_w_out, v_norm_ffn_w, v_w_ffn_in, v_w_ffn_out, v_final_norm_w)))
    loss, dx, (grads, deltas, new_m, new_v) = _step(w, m, v, x.reshape(x.shape[1:]), loss_target.reshape(x.shape[1:]))
    return (loss, dx[None], *grads, *deltas, *new_m, *new_v)
```

```python
import functools
import math

import jax
import jax.numpy as jnp
from jax import lax
from jax.experimental import pallas as pl
from jax.experimental.pallas import tpu as pltpu

F32, BF16 = jnp.float32, jnp.bfloat16

D = 2048
HEADS, DK, DV = 8, 128, 256
GROUPS, SLEN = 8, 128
SCH = D // GROUPS
DFF = 5632
CHUNK = 64
IN_COLS = 14336
NDEV = 8
SH_IN = IN_COLS // NDEV
SH_FF = 2 * DFF // NDEV
EPS = 1e-6
ROPE_BASE = 10000.0
RL = 256
Z_G, Z_SU, Z_SV, Z_GA, Z_GB = 2, 3, 4, 5, 6

ADAM_TILE_ELEMS = 256 * 1024
ADAM_LR, ADAM_B1, ADAM_B2, ADAM_EPS, ADAM_WD, ADAM_STEP = 0.001, 0.9, 0.999, 1e-08, 0.01, 10

VMEM_BYTES_V7X = 64 << 20
MESH_T = pl.DeviceIdType.MESH
ANY = pl.BlockSpec(memory_space=pl.ANY)
HBM_SPEC = pl.BlockSpec(memory_space=pltpu.HBM)
SEM_SPEC = pl.BlockSpec(memory_space=pltpu.SEMAPHORE)


def _cparams(sem, vmem_mb):
    assert (vmem_mb << 20) < VMEM_BYTES_V7X
    return pltpu.CompilerParams(dimension_semantics=sem, vmem_limit_bytes=vmem_mb << 20)


def _mm(name, a, b, *, grid, a_spec, b_spec, o_spec, o_block, out_shape, out_dtype, contract, res=None, vmem_mb=48):
    nk = grid[2]
    dn = (contract, ((), ()))
    has_res = res is not None

    def body(*refs):
        if has_res:
            a_ref, b_ref, r_ref, o_ref, acc_ref = refs
        else:
            a_ref, b_ref, o_ref, acc_ref = refs
        part = lax.dot_general(a_ref[...].astype(BF16), b_ref[...].astype(BF16), dn, preferred_element_type=F32)

        def finish(val):
            if has_res:
                val = val + r_ref[...].astype(F32)
            o_ref[...] = val.astype(o_ref.dtype)

        if nk == 1:
            finish(part)
        else:
            k = pl.program_id(2)

            @pl.when(k == 0)
            def _():
                acc_ref[...] = part

            @pl.when(k > 0)
            def _():
                acc_ref[...] += part

            @pl.when(k == nk - 1)
            def _():
                finish(acc_ref[...])

    ins = [a, b] + ([res] if has_res else [])
    in_specs = [a_spec, b_spec] + ([o_spec] if has_res else [])
    return pl.pallas_call(
        body, name=name, grid=grid, in_specs=in_specs, out_specs=o_spec,
        out_shape=jax.ShapeDtypeStruct(out_shape, out_dtype),
        scratch_shapes=[pltpu.VMEM(o_block if nk > 1 else (8, 128), F32)],
        compiler_params=_cparams(("parallel", "parallel", "arbitrary"), vmem_mb),
    )(*ins)


def _tile(n, want):
    t = min(n, want)
    assert n % t == 0, (n, want)
    return t


def _mm_nn(name, a, b, *, out_dtype, tm=1024, tn=1024, tk=None, res=None):
    (m, k), n = a.shape, b.shape[1]
    tm, tn, tk = _tile(m, tm), _tile(n, tn), k if tk is None else tk
    return _mm(name, a, b, grid=(m // tm, n // tn, k // tk),
               a_spec=pl.BlockSpec((tm, tk), lambda i, j, kk: (i, kk)),
               b_spec=pl.BlockSpec((tk, tn), lambda i, j, kk: (kk, j)),
               o_spec=pl.BlockSpec((tm, tn), lambda i, j, kk: (i, j)), o_block=(tm, tn),
               out_shape=(m, n), out_dtype=out_dtype, contract=((1,), (0,)), res=res)


def _mm_nt(name, a, b, *, out_dtype, tm=1024, tn=1024, tk=None):
    (m, k), n = a.shape, b.shape[0]
    tm, tn, tk = _tile(m, tm), _tile(n, tn), k if tk is None else tk
    return _mm(name, a, b, grid=(m // tm, n // tn, k // tk),
               a_spec=pl.BlockSpec((tm, tk), lambda i, j, kk: (i, kk)),
               b_spec=pl.BlockSpec((tn, tk), lambda i, j, kk: (j, kk)),
               o_spec=pl.BlockSpec((tm, tn), lambda i, j, kk: (i, j)), o_block=(tm, tn),
               out_shape=(m, n), out_dtype=out_dtype, contract=((1,), (1,)))


def _mm_tn(name, a, b, *, out_dtype, tm=1024, tn=1024, tk=1024):
    (k, m), n = a.shape, b.shape[1]
    tm, tn, tk = _tile(m, tm), _tile(n, tn), _tile(k, tk)
    return _mm(name, a, b, grid=(m // tm, n // tn, k // tk),
               a_spec=pl.BlockSpec((tk, tm), lambda i, j, kk: (kk, i)),
               b_spec=pl.BlockSpec((tk, tn), lambda i, j, kk: (kk, j)),
               o_spec=pl.BlockSpec((tm, tn), lambda i, j, kk: (i, j)), o_block=(tm, tn),
               out_shape=(m, n), out_dtype=out_dtype, contract=((0,), (0,)))


def _mm_nn_sh(name, a, bsh, *, out_dtype, tm=1024):
    (m, k), ns = a.shape, bsh.shape[2]
    tm = _tile(m, tm)
    return _mm(name, a, bsh, grid=(m // tm, NDEV, 1),
               a_spec=pl.BlockSpec((tm, k), lambda i, j, kk: (i, 0)),
               b_spec=pl.BlockSpec((None, k, ns), lambda i, j, kk: (j, 0, 0)),
               o_spec=pl.BlockSpec((tm, ns), lambda i, j, kk: (i, j)), o_block=(tm, ns),
               out_shape=(m, NDEV * ns), out_dtype=out_dtype, contract=((1,), (0,)), vmem_mb=56)


def _mm_nt_sh(name, a, bsh, *, out_dtype, tm=1024, tn=1024):
    m, (_, n, ns) = a.shape[0], bsh.shape
    tm, tn = _tile(m, tm), _tile(n, tn)
    return _mm(name, a, bsh, grid=(m // tm, n // tn, NDEV),
               a_spec=pl.BlockSpec((tm, ns), lambda i, j, kk: (i, kk)),
               b_spec=pl.BlockSpec((None, tn, ns), lambda i, j, kk: (kk, j, 0)),
               o_spec=pl.BlockSpec((tm, tn), lambda i, j, kk: (i, j)), o_block=(tm, tn),
               out_shape=(m, n), out_dtype=out_dtype, contract=((1,), (1,)))


def _mm_tn_sh(name, a, b, ns, *, out_dtype, tm=1024, tk=1024):
    k, m = a.shape
    tm, tk = _tile(m, tm), _tile(k, tk)
    return _mm(name, a, b, grid=(m // tm, NDEV, k // tk),
               a_spec=pl.BlockSpec((tk, tm), lambda i, j, kk: (kk, i)),
               b_spec=pl.BlockSpec((tk, ns), lambda i, j, kk: (kk, j)),
               o_spec=pl.BlockSpec((None, tm, ns), lambda i, j, kk: (j, i, 0)), o_block=(tm, ns),
               out_shape=(NDEV, m, ns), out_dtype=out_dtype, contract=((0,), (0,)))


def _tilemap(name, fn, tiled_ins, full_ins, tiled_outs, acc_shapes, *, rows, tm, ncol=1, vmem_mb=48):
    n_ti, n_fi, n_to, n_ao = len(tiled_ins), len(full_ins), len(tiled_outs), len(acc_shapes)
    assert ncol == 1 or n_ao == 0

    def body(*refs):
        t_in, f_in = refs[:n_ti], refs[n_ti:n_ti + n_fi]
        t_out, a_out = refs[n_ti + n_fi:n_ti + n_fi + n_to], refs[n_ti + n_fi + n_to:]
        outs = fn(*[r[...] for r in t_in], *[r[...] for r in f_in])
        for r, v in zip(t_out, outs[:n_to]):
            r[...] = v.astype(r.dtype)
        if n_ao:
            i = pl.program_id(1)

            @pl.when(i == 0)
            def _():
                for r, v in zip(a_out, outs[n_to:]):
                    r[...] = v

            @pl.when(i > 0)
            def _():
                for r, v in zip(a_out, outs[n_to:]):
                    r[...] += v

    in_specs = [pl.BlockSpec((tm, bw), lambda j, i, base=base: (i, base + j)) for (_, bw, base) in tiled_ins]
    in_specs += [pl.BlockSpec(a.shape, lambda j, i, nd=a.ndim: (0,) * nd) for a in full_ins]
    out_specs = [pl.BlockSpec((tm, bw), lambda j, i: (i, j)) for (_, bw, _) in tiled_outs]
    out_specs += [pl.BlockSpec(s, lambda j, i: (0, 0)) for s in acc_shapes]
    out_shape = [jax.ShapeDtypeStruct((rows, w), dt) for (w, _, dt) in tiled_outs]
    out_shape += [jax.ShapeDtypeStruct(s, F32) for s in acc_shapes]
    return pl.pallas_call(
        body, name=name, grid=(ncol, rows // tm), in_specs=in_specs, out_specs=out_specs, out_shape=out_shape,
        compiler_params=_cparams(("parallel", "arbitrary"), vmem_mb),
    )(*[t[0] for t in tiled_ins], *full_ins)


def _f_rms(x, w):
    return x * lax.rsqrt(jnp.mean(x * x, axis=-1, keepdims=True) + EPS) * w


def _rms_fwd(name, x, w):
    rows = x.shape[0]
    return _tilemap(name, lambda xv, wv: (_f_rms(xv, wv),), [(x, D, 0)], [w], [(D, D, BF16)], [],
                    rows=rows, tm=_tile(rows, 256))[0]


def _rms_bwd(name, x, w, dh, dres):
    rows = x.shape[0]

    def fn(xv, dhv, drv, wv):
        _, vjp = jax.vjp(_f_rms, xv, wv)
        dx, dw = vjp(dhv.astype(F32))
        return dx + drv, dw

    return _tilemap(name, fn, [(x, D, 0), (dh, D, 0), (dres, D, 0)], [w], [(D, D, F32)], [(1, D)],
                    rows=rows, tm=_tile(rows, 256))


def _f_mixa_head(g, o, gw):
    mu = jnp.mean(o, axis=-1, keepdims=True)
    d = o - mu
    var = jnp.mean(d * d, axis=-1, keepdims=True)
    return g * jax.nn.sigmoid(g) * (d * lax.rsqrt(var + EPS) * gw)


def _heads(width):
    return [slice(h * width, (h + 1) * width) for h in range(HEADS)]


def _mixa_fwd(name, z, o, gw):
    rows = z.shape[0]

    def fn(g, ov, gwv):
        g, ov = g.astype(F32), ov.astype(F32)
        return (jnp.concatenate([_f_mixa_head(g[:, s], ov[:, s], gwv[:, s]) for s in _heads(DV)], axis=1),)

    return _tilemap(name, fn, [(z, D, Z_G), (o, D, 0)], [gw], [(D, D, BF16)], [], rows=rows, tm=_tile(rows, 256))[0]


def _mixa_bwd(name, z, o, gw, da):
    rows = z.shape[0]

    def fn(g, ov, dav, gwv):
        g, ov, dav = g.astype(F32), ov.astype(F32), dav.astype(F32)
        parts = []
        for s in _heads(DV):
            _, vjp = jax.vjp(_f_mixa_head, g[:, s], ov[:, s], gwv[:, s])
            parts.append(vjp(dav[:, s]))
        return tuple(jnp.concatenate([p[i] for p in parts], axis=1) for i in range(3))

    return _tilemap(name, fn, [(z, D, Z_G), (o, D, 0), (da, D, 0)], [gw], [(D, D, BF16), (D, D, BF16)], [(1, D)],
                    rows=rows, tm=_tile(rows, 256))


def _f_merge(ga, gb, a, b):
    return jax.nn.sigmoid(ga) * a + jax.nn.sigmoid(gb) * b


def _merge_fwd(name, z, a, b):
    rows, bw = z.shape[0], 1024

    def fn(ga, gb, av, bv):
        return (_f_merge(ga.astype(F32), gb.astype(F32), av.astype(F32), bv.astype(F32)),)

    return _tilemap(name, fn, [(z, bw, Z_GA * 2), (z, bw, Z_GB * 2), (a, bw, 0), (b, bw, 0)], [], [(D, bw, BF16)], [],
                    rows=rows, tm=_tile(rows, 512), ncol=D // bw)[0]


def _merge_bwd(name, z, a, b, dm):
    rows, bw = z.shape[0], 1024

    def fn(ga, gb, av, bv, dmv):
        _, vjp = jax.vjp(_f_merge, ga.astype(F32), gb.astype(F32), av.astype(F32), bv.astype(F32))
        return vjp(dmv.astype(F32))

    return _tilemap(name, fn, [(z, bw, Z_GA * 2), (z, bw, Z_GB * 2), (a, bw, 0), (b, bw, 0), (dm, bw, 0)], [],
                    [(D, bw, BF16)] * 4, [], rows=rows, tm=_tile(rows, 512), ncol=D // bw)


def _f_swiglu(a, c):
    return a * jax.nn.sigmoid(a) * c


def _swiglu_fwd(name, ac):
    rows = ac.shape[0]

    def fn(a, c):
        return (_f_swiglu(a.astype(F32), c.astype(F32)),)

    return _tilemap(name, fn, [(ac, SH_FF, 0), (ac, SH_FF, 4)], [], [(DFF, SH_FF, BF16)], [],
                    rows=rows, tm=_tile(rows, 512), ncol=4)[0]


def _swiglu_bwd(name, ac, df):
    rows = ac.shape[0]

    def fn(a, c, dfv):
        _, vjp = jax.vjp(_f_swiglu, a.astype(F32), c.astype(F32))
        return vjp(dfv.astype(F32))

    return _tilemap(name, fn, [(ac, SH_FF, 0), (ac, SH_FF, 4), (df, SH_FF, 0)], [], [(DFF, SH_FF, BF16)] * 2, [],
                    rows=rows, tm=_tile(rows, 512), ncol=4)


def _loss_bwd(name, x, w, target):
    rows = x.shape[0]

    def fn(xv, tv, wv):
        y, vjp = jax.vjp(_f_rms, xv, wv)
        e = y - tv
        dx, dw = vjp(e * (1.0 / D))
        per_row = jnp.mean(e * e, axis=-1, keepdims=True)
        return dx, dw, jnp.broadcast_to(0.5 * jnp.sum(per_row, axis=0, keepdims=True), (1, 128))

    return _tilemap(name, fn, [(x, D, 0), (target, D, 0)], [w], [(D, D, F32)], [(1, D), (1, 128)],
                    rows=rows, tm=_tile(rows, 256))


def _gelu(v):
    return 0.5 * v * (1.0 + lax.erf(v * (1.0 / math.sqrt(2.0))))


def _f_sgu_pre(su, sv, lw, lb):
    zv = _gelu(sv)
    mu = jnp.mean(zv, axis=-1, keepdims=True)
    d = zv - mu
    var = jnp.mean(d * d, axis=-1, keepdims=True)
    return _gelu(su), d * lax.rsqrt(var + EPS) * lw + lb


def _sgu_blocks(tm):
    return [slice(r * SLEN, (r + 1) * SLEN) for r in range(tm // SLEN)]


def _sgu_mix(wm, vnb, bexp, tm):
    rows = []
    for r in _sgu_blocks(tm):
        cols = [jnp.dot(wm[g], vnb[r, g * SCH:(g + 1) * SCH], preferred_element_type=F32) for g in range(GROUPS)]
        rows.append(jnp.concatenate(cols, axis=1) + bexp)
    return jnp.concatenate(rows, axis=0)


def _sgu_fwd(name, z, lw, lb, wm, bexp):
    rows = z.shape[0]
    tm = _tile(rows, 256)

    def fn(su, sv, lwv, lbv, wmv, bev):
        zu, vn = _f_sgu_pre(su.astype(F32), sv.astype(F32), lwv, lbv)
        return (zu * _sgu_mix(wmv, vn.astype(BF16), bev, tm),)

    return _tilemap(name, fn, [(z, D, Z_SU), (z, D, Z_SV)], [lw, lb, wm, bexp], [(D, D, BF16)], [], rows=rows, tm=tm)[0]


def _sgu_bwd(name, z, lw, lb, wm, wmt, bexp, maskf, db):
    rows = z.shape[0]
    tm = _tile(rows, 256)

    def fn(su, sv, dbv, lwv, lbv, wmv, wmtv, bev, mkv):
        (zu, vn), vjp = jax.vjp(_f_sgu_pre, su.astype(F32), sv.astype(F32), lwv, lbv)
        vnb = vn.astype(BF16)
        dbv = dbv.astype(F32)
        dmixed = dbv * zu
        dzu = dbv * _sgu_mix(wmv, vnb, bev, tm)
        dmb = dmixed.astype(BF16)
        dvn_rows, dws, dbf = [], [None] * GROUPS, None
        for r in _sgu_blocks(tm):
            cols = []
            for g in range(GROUPS):
                cs = slice(g * SCH, (g + 1) * SCH)
                cols.append(jnp.dot(wmtv[g], dmb[r, cs], preferred_element_type=F32))
                dw = lax.dot_general(dmb[r, cs], vnb[r, cs], (((1,), (1,)), ((), ())), preferred_element_type=F32)
                dws[g] = dw if dws[g] is None else dws[g] + dw
            dvn_rows.append(jnp.concatenate(cols, axis=1))
            dbf = dmixed[r] if dbf is None else dbf + dmixed[r]
        dsu, dsv, dlw, dlb = vjp((dzu, jnp.concatenate(dvn_rows, axis=0)))
        dws_all = jnp.concatenate([dw * mkv for dw in dws], axis=0)
        return dsu, dsv, dws_all, dbf, dlw, dlb

    return _tilemap(name, fn, [(z, D, Z_SU), (z, D, Z_SV), (db, D, 0)], [lw, lb, wm, wmt, bexp, maskf],
                    [(D, D, BF16), (D, D, BF16)], [(GROUPS * SLEN, SLEN), (SLEN, D), (1, D), (1, D)], rows=rows, tm=tm)


def _group_sums(name, dbf):
    def body(e_ref, x_ref, o_ref):
        o_ref[...] = lax.dot_general(e_ref[...], x_ref[...], (((1,), (1,)), ((), ())),
                                     precision=lax.Precision.HIGHEST, preferred_element_type=F32)

    ind = (jnp.arange(D)[None, :] // SCH == jnp.arange(GROUPS)[:, None]).astype(F32)
    return pl.pallas_call(body, name=name, out_shape=jax.ShapeDtypeStruct((GROUPS, SLEN), F32))(ind, dbf)


def _ret_consts(rows):
    hh = jnp.arange(HEADS, dtype=F32)
    log_g = jnp.log1p(-(2.0 ** (-5.0 - hh)))
    idx = jnp.arange(RL, dtype=F32)
    ci = jnp.arange(RL) // CHUNK
    mask = ci[None, :] <= ci[:, None]
    dm = jnp.where(mask[None], jnp.exp(log_g[:, None, None] * jnp.abs(idx[:, None] - idx[None, :])), 0.0)
    qd = jnp.broadcast_to(jnp.exp(log_g[:, None] * (idx[None, :] + 1.0))[:, :, None], (HEADS, RL, DK))
    kd = jnp.broadcast_to(jnp.exp(log_g[:, None] * (RL - 1.0 - idx[None, :]))[:, :, None], (HEADS, RL, DK))
    cd = jnp.broadcast_to(jnp.exp(log_g * RL)[:, None, None], (HEADS, DK, DV))
    half = DK // 2
    inv = ROPE_BASE ** (-jnp.arange(half, dtype=F32) / half)
    ang = jnp.arange(rows, dtype=jnp.int32).astype(F32)[:, None] * inv[None, :]
    cos, sin = jnp.cos(ang), jnp.sin(ang)
    return dict(dm=dm.astype(F32), qd=qd, kd=kd, cd=cd, cos=jnp.concatenate([cos, cos], axis=1),
                sin=jnp.concatenate([-sin, sin], axis=1))


def _rope(x, cos, sin_signed):
    return x * cos + pltpu.roll(x, DK // 2, 1) * sin_signed


def _rope_t(d, cos, sin_signed):
    return d * cos + pltpu.roll(d * sin_signed, DK // 2, 1)


def _dot(a, b, ca, cb):
    return lax.dot_general(a, b, (((ca,), (cb,)), ((), ())), preferred_element_type=F32)


def _ret_specs(nb, rev):
    blk = (lambda n: nb - 1 - n) if rev else (lambda n: n)
    return dict(
        q=pl.BlockSpec((RL, DK), lambda h, n: (blk(n), h)),
        k=pl.BlockSpec((RL, DK), lambda h, n: (blk(n), HEADS + h)),
        v=pl.BlockSpec((RL, DV), lambda h, n: (blk(n), HEADS + h)),
        o=pl.BlockSpec((RL, DV), lambda h, n: (blk(n), h)),
        rope=pl.BlockSpec((RL, DK), lambda h, n: (blk(n), 0)),
        dm=pl.BlockSpec((None, RL, RL), lambda h, n: (h, 0, 0)),
        qd=pl.BlockSpec((None, RL, DK), lambda h, n: (h, 0, 0)),
        cd=pl.BlockSpec((None, DK, DV), lambda h, n: (h, 0, 0)),
        st=pl.BlockSpec((None, None, DK, DV), lambda h, n: (blk(n), h, 0, 0)),
    )


def _ret_fwd(name, z, rc):
    rows = z.shape[0]
    nb = rows // RL
    sp = _ret_specs(nb, False)

    def body(q_ref, k_ref, v_ref, cos_ref, sin_ref, dm_ref, qd_ref, kd_ref, cd_ref, o_ref, st_ref, state):
        @pl.when(pl.program_id(1) == 0)
        def _():
            state[...] = jnp.zeros_like(state)

        cos, sin = cos_ref[...], sin_ref[...]
        qf = _rope(q_ref[...].astype(F32), cos, sin) * (DK ** -0.5)
        kf = _rope(k_ref[...].astype(F32), cos, sin)
        vb = v_ref[...]
        s0 = state[...]
        s0b = s0.astype(BF16)
        st_ref[...] = s0b
        sc = _dot(qf.astype(BF16), kf.astype(BF16), 1, 1) * dm_ref[...]
        o = _dot(sc.astype(BF16), vb, 1, 0) + _dot((qf * qd_ref[...]).astype(BF16), s0b, 1, 0)
        o_ref[...] = o.astype(o_ref.dtype)
        state[...] = cd_ref[...] * s0 + _dot((kf * kd_ref[...]).astype(BF16), vb, 0, 0)

    return pl.pallas_call(
        body, name=name, grid=(HEADS, nb),
        in_specs=[sp["q"], sp["k"], sp["v"], sp["rope"], sp["rope"], sp["dm"], sp["qd"], sp["qd"], sp["cd"]],
        out_specs=[sp["o"], sp["st"]],
        out_shape=[jax.ShapeDtypeStruct((rows, HEADS * DV), BF16), jax.ShapeDtypeStruct((nb, HEADS, DK, DV), BF16)],
        scratch_shapes=[pltpu.VMEM((DK, DV), F32)],
        compiler_params=_cparams(("parallel", "arbitrary"), 32),
    )(z, z, z, rc["cos"], rc["sin"], rc["dm"], rc["qd"], rc["kd"], rc["cd"])


def _ret_bwd(name, z, rc, states, do):
    rows = z.shape[0]
    nb = rows // RL
    sp = _ret_specs(nb, True)

    def body(q_ref, k_ref, v_ref, cos_ref, sin_ref, dm_ref, qd_ref, kd_ref, cd_ref, st_ref, do_ref,
             dq_ref, dk_ref, dv_ref, dstate):
        @pl.when(pl.program_id(1) == 0)
        def _():
            dstate[...] = jnp.zeros_like(dstate)

        cos, sin, dm, qd, kd = cos_ref[...], sin_ref[...], dm_ref[...], qd_ref[...], kd_ref[...]
        qf = _rope(q_ref[...].astype(F32), cos, sin) * (DK ** -0.5)
        kf = _rope(k_ref[...].astype(F32), cos, sin)
        qb, kb, vb, dob, s0b = qf.astype(BF16), kf.astype(BF16), v_ref[...], do_ref[...], st_ref[...]
        qsb, ksb = (qf * qd).astype(BF16), (kf * kd).astype(BF16)
        ds1 = dstate[...]
        ds1b = ds1.astype(BF16)
        scb = (_dot(qb, kb, 1, 1) * dm).astype(BF16)
        dscb = (_dot(dob, vb, 1, 1) * dm).astype(BF16)
        dq = _dot(dscb, kb, 1, 0) + _dot(dob, s0b, 1, 1) * qd
        dk = _dot(dscb, qb, 0, 0) + _dot(vb, ds1b, 1, 1) * kd
        dv = _dot(scb, dob, 0, 0) + _dot(ksb, ds1b, 1, 0)
        dq_ref[...] = _rope_t(dq * (DK ** -0.5), cos, sin).astype(dq_ref.dtype)
        dk_ref[...] = _rope_t(dk, cos, sin).astype(dk_ref.dtype)
        dv_ref[...] = dv.astype(dv_ref.dtype)
        dstate[...] = cd_ref[...] * ds1 + _dot(qsb, dob, 0, 0)

    dqk = pl.BlockSpec((RL, DK), lambda h, n: (nb - 1 - n, h))
    return pl.pallas_call(
        body, name=name, grid=(HEADS, nb),
        in_specs=[sp["q"], sp["k"], sp["v"], sp["rope"], sp["rope"], sp["dm"], sp["qd"], sp["qd"], sp["cd"], sp["st"], sp["o"]],
        out_specs=[dqk, dqk, sp["o"]],
        out_shape=[jax.ShapeDtypeStruct((rows, HEADS * DK), BF16)] * 2 + [jax.ShapeDtypeStruct((rows, HEADS * DV), BF16)],
        scratch_shapes=[pltpu.VMEM((DK, DV), F32)],
        compiler_params=_cparams(("parallel", "arbitrary"), 32),
    )(z, z, z, rc["cos"], rc["sin"], rc["dm"], rc["qd"], rc["kd"], rc["cd"], states, do)


def _layer_fwd(l, x, p, rc):
    n = f"l{l}_"
    h = _rms_fwd(n + "rms1", x, p["norm_mix_w"])
    z = _mm_nn_sh(n + "in_proj", h, p["w_in"], out_dtype=BF16)
    o, states = _ret_fwd(n + "ret", z, rc)
    a_in = _mixa_fwd(n + "mixa", z, o, p["ret_gn_w"])
    b_in = _sgu_fwd(n + "sgu", z, p["sgu_ln_w"], p["sgu_ln_b"], p["wm"], p["bexp"])
    br_a = _mm_nn(n + "ret_proj", a_in, p["ret_proj"], out_dtype=BF16)
    br_b = _mm_nn(n + "sgu_proj", b_in, p["sgu_proj"], out_dtype=BF16)
    merged = _merge_fwd(n + "merge", z, br_a, br_b)
    x1 = _mm_nn(n + "out_proj", merged, p["w_out"], out_dtype=F32, res=x)
    h2 = _rms_fwd(n + "rms2", x1, p["norm_ffn_w"])
    ac = _mm_nn_sh(n + "ffn_in", h2, p["w_ffn_in"], out_dtype=BF16)
    f = _swiglu_fwd(n + "swiglu", ac)
    x2 = _mm_nn(n + "ffn_out", f, p["w_ffn_out"], out_dtype=F32, res=x1, tk=SH_FF)
    saved = dict(x=x, h=h, z=z, o=o, states=states, a_in=a_in, b_in=b_in, br_a=br_a, br_b=br_b, merged=merged,
                 x1=x1, h2=h2, ac=ac, f=f)
    return x2, saved


def _layer_bwd_ffn(l, dx2, p, s):
    n = f"l{l}b_"
    df = _mm_nt(n + "d_f", dx2, p["w_ffn_out"], out_dtype=BF16, tn=SH_FF)
    g_ffn_out = _mm_tn(n + "g_ffn_out", s["f"], dx2, out_dtype=BF16, tm=SH_FF)
    da, dc = _swiglu_bwd(n + "swiglu", s["ac"], df)
    dac = jnp.concatenate([da, dc], axis=1)
    dh2 = _mm_nt_sh(n + "d_h2", dac, p["w_ffn_in"], out_dtype=BF16)
    g_ffn_in = _mm_tn_sh(n + "g_ffn_in", s["h2"], dac, SH_FF, out_dtype=BF16)
    dx1, g_nffn = _rms_bwd(n + "rms2", s["x1"], p["norm_ffn_w"], dh2, dx2)
    return dx1, dict(w_ffn_in=g_ffn_in, w_ffn_out=g_ffn_out.reshape(NDEV, -1, D)), dict(norm_ffn_w=g_nffn)


def _layer_bwd_mix(l, dx1, p, s, rc):
    n = f"l{l}b_"
    dmerged = _mm_nt(n + "d_merged", dx1, p["w_out"], out_dtype=BF16)
    g_out = _mm_tn(n + "g_out", s["merged"], dx1, out_dtype=BF16)
    dga, dgb, dbr_a, dbr_b = _merge_bwd(n + "merge", s["z"], s["br_a"], s["br_b"], dmerged)
    da_in = _mm_nt(n + "d_a_in", dbr_a, p["ret_proj"], out_dtype=BF16)
    g_ret_proj = _mm_tn(n + "g_ret_proj", s["a_in"], dbr_a, out_dtype=BF16)
    db_in = _mm_nt(n + "d_b_in", dbr_b, p["sgu_proj"], out_dtype=BF16)
    g_sgu_proj = _mm_tn(n + "g_sgu_proj", s["b_in"], dbr_b, out_dtype=BF16)
    dg, do, g_gn = _mixa_bwd(n + "mixa", s["z"], s["o"], p["ret_gn_w"], da_in)
    dq, dk, dv = _ret_bwd(n + "ret", s["z"], rc, s["states"], do)
    dsu, dsv, g_ws, dbf, g_lw, g_lb = _sgu_bwd(n + "sgu", s["z"], p["sgu_ln_w"], p["sgu_ln_b"], p["wm"], p["wmt"],
                                                 p["bexp"], p["maskf"], db_in)
    g_bs = _group_sums(n + "g_bs", dbf)
    dz = jnp.concatenate([dq, dk, dv, dg, dsu, dsv, dga, dgb], axis=1)
    dh = _mm_nt_sh(n + "d_h", dz, p["w_in"], out_dtype=BF16)
    g_in = _mm_tn_sh(n + "g_in", s["h"], dz, SH_IN, out_dtype=BF16)
    dx, g_nmix = _rms_bwd(n + "rms1", s["x"], p["norm_mix_w"], dh, dx1)
    big = dict(w_in=g_in, ret_proj=g_ret_proj.reshape(NDEV, -1, D), sgu_proj=g_sgu_proj.reshape(NDEV, -1, D),
               w_out=g_out.reshape(NDEV, -1, D))
    small = dict(norm_mix_w=g_nmix, ret_gn_w=g_gn, sgu_ln_w=g_lw, sgu_ln_b=g_lb, sgu_w_s=g_ws, sgu_b_s=g_bs)
    return dx, big, small


def _slot(ref, p):
    return ref.at[4 * p[0] + 2 * p[1] + p[2]]


def _allgather(name, shards):
    n = len(shards)

    def body(*refs):
        ins, outs = refs[:n], refs[n:2 * n]
        send_sems, recv_sems, local_sems = refs[2 * n:]
        x, y, c = lax.axis_index("x"), lax.axis_index("y"), lax.axis_index("c")
        me, sibling = (x, y, c), (x, y, 1 - c)
        chips = [(1 - x, y), (x, 1 - y), (1 - x, 1 - y)]

        def copy(a, k, block, to, src=None):
            return pltpu.make_async_remote_copy(
                src_ref=_slot(outs[a], block) if src is None else src, dst_ref=_slot(outs[a], block),
                send_sem=send_sems.at[a, k], recv_sem=recv_sems.at[a, k], device_id=to, device_id_type=MESH_T)

        mine = [pltpu.make_async_copy(ins[a], _slot(outs[a], me), local_sems.at[a]) for a in range(n)]
        for cp in mine:
            cp.start()
        first = []
        for a in range(n):
            first.append(copy(a, 0, me, sibling, src=ins[a]))
            first += [copy(a, 1 + j, me, (*chip, c), src=ins[a]) for j, chip in enumerate(chips)]
        for cp in first:
            cp.start()
        passed = []
        for j, chip in enumerate(chips):
            for a in range(n):
                copy(a, 1 + j, (*chip, c), me).wait_recv()
                cp = copy(a, 4 + j, (*chip, c), sibling)
                cp.start()
                passed.append(cp)
        for a in range(n):
            copy(a, 0, sibling, me).wait_recv()
            for j, chip in enumerate(chips):
                copy(a, 4 + j, (*chip, 1 - c), me).wait_recv()
        for cp in first + passed:
            cp.wait_send()
        for cp in mine:
            cp.wait()

    return pl.pallas_call(
        body, name=name, in_specs=[ANY] * n, out_specs=[ANY] * n,
        out_shape=[jax.ShapeDtypeStruct((NDEV,) + s.shape, s.dtype) for s in shards],
        scratch_shapes=[pltpu.SemaphoreType.DMA((n, 7)), pltpu.SemaphoreType.DMA((n, 7)), pltpu.SemaphoreType.DMA((n,))],
    )(*shards)


def _peers(x, y, c):
    return [((1 - x) if fx else x, (1 - y) if fy else y, (1 - c) if fc else c)
            for fx in (0, 1) for fy in (0, 1) for fc in (0, 1) if fx + fy + fc]


def _place_own(name, srcs, whole):
    n = len(srcs)

    def body(*refs):
        ins, outs, sems = refs[:n], refs[n:2 * n], refs[2 * n]
        me = (lax.axis_index("x"), lax.axis_index("y"), lax.axis_index("c"))
        cps = [pltpu.make_async_copy(ins[a] if whole else _slot(ins[a], me), _slot(outs[a], me), sems.at[a]) for a in range(n)]
        for cp in cps:
            cp.start()
        for cp in cps:
            cp.wait()

    shapes = [((NDEV,) + s.shape) if whole else s.shape for s in srcs]
    return pl.pallas_call(
        body, name=name, in_specs=[ANY] * n, out_specs=[ANY] * n,
        out_shape=[jax.ShapeDtypeStruct(sh, s.dtype) for sh, s in zip(shapes, srcs)],
        scratch_shapes=[pltpu.SemaphoreType.DMA((n,))],
    )(*srcs)


def _exchange_copy(whole, src_ref, land_ref, send_sems, recv_sems, a, k, peer, me, landed):
    return pltpu.make_async_remote_copy(
        src_ref=src_ref if whole else _slot(src_ref, peer), dst_ref=_slot(land_ref, peer if landed else me),
        send_sem=send_sems.at[7 * a + k], recv_sem=recv_sems.at[7 * a + k], device_id=peer, device_id_type=MESH_T)


def _exchange_start(name, srcs, lands, whole, after):
    n = len(srcs)

    def body(*refs):
        src_refs, land_refs = refs[:n], refs[n:2 * n]
        send_sems, recv_sems = refs[2 * n + 1], refs[2 * n + 2]
        token = refs[-1]
        x, y, c = lax.axis_index("x"), lax.axis_index("y"), lax.axis_index("c")
        for a in range(n):
            for k, peer in enumerate(_peers(x, y, c)):
                _exchange_copy(whole, src_refs[a], land_refs[a], send_sems, recv_sems, a, k, peer, (x, y, c), False).start()
        token[...] = jnp.zeros_like(token)

    hbm = lambda t: pltpu.HBM(t.shape, t.dtype)
    out = pl.pallas_call(
        body, name=name, in_specs=[HBM_SPEC] * (2 * n) + [ANY],
        out_specs=(SEM_SPEC, SEM_SPEC, *[HBM_SPEC] * (2 * n), pl.BlockSpec(memory_space=pltpu.VMEM)),
        out_shape=(pltpu.SemaphoreType.DMA((7 * n,)), pltpu.SemaphoreType.DMA((7 * n,)), *[hbm(t) for t in srcs],
                   *[hbm(t) for t in lands], jax.ShapeDtypeStruct((8, 128), F32)),
        input_output_aliases={i: 2 + i for i in range(2 * n)},
        compiler_params=pltpu.CompilerParams(has_side_effects=pltpu.SideEffectType.DATAFLOW_SIDE_EFFECTING),
    )(*[pltpu.with_memory_space_constraint(t, pltpu.HBM) for t in (*srcs, *lands)], after)
    return dict(send=out[0], recv=out[1], srcs=out[2:2 + n], lands=out[2 + n:2 + 2 * n], token=out[-1])


def _exchange_wait(name, started, whole, after):
    n = len(started["srcs"])

    def body(*refs):
        src_refs, land_refs = refs[:n], refs[n:2 * n]
        send_sems, recv_sems = refs[2 * n], refs[2 * n + 1]
        x, y, c = lax.axis_index("x"), lax.axis_index("y"), lax.axis_index("c")
        for a in range(n):
            for k, peer in enumerate(_peers(x, y, c)):
                cp = _exchange_copy(whole, src_refs[a], land_refs[a], send_sems, recv_sems, a, k, peer, (x, y, c), True)
                cp.wait_send()
                cp.wait_recv()

    hbm = lambda t: pltpu.HBM(t.shape, t.dtype)
    out = pl.pallas_call(
        body, name=name, in_specs=[HBM_SPEC] * (2 * n) + [SEM_SPEC, SEM_SPEC, ANY], out_specs=[HBM_SPEC] * (2 * n),
        out_shape=[hbm(t) for t in (*started["srcs"], *started["lands"])],
        input_output_aliases={i: i for i in range(2 * n)},
        compiler_params=pltpu.CompilerParams(has_side_effects=pltpu.SideEffectType.DATAFLOW_SIDE_EFFECTING),
    )(*started["srcs"], *started["lands"], started["send"], started["recv"], after)
    return out[n:]


def _adamw(w, g, m, v):
    m = ADAM_B1 * m + (1.0 - ADAM_B1) * g
    v = ADAM_B2 * v + (1.0 - ADAM_B2) * (g * g)
    m_hat = m / (1.0 - ADAM_B1 ** ADAM_STEP)
    v_hat = v / (1.0 - ADAM_B2 ** ADAM_STEP)
    return -ADAM_LR * (m_hat / (jnp.sqrt(v_hat) + ADAM_EPS) + ADAM_WD * w), m, v


def _sum_and_adamw(name, recv, w, m, v, layer, prev):
    _, r, c = recv.shape
    tr = max(t for t in range(16, r + 1, 16) if r % t == 0 and t * c <= ADAM_TILE_ELEMS)

    def body(recv_ref, w_ref, m_ref, v_ref, *rest):
        g_ref, d_ref, m_out, v_out = rest[4:]
        g = recv_ref[0].astype(F32)
        for s in range(1, NDEV):
            g = g + recv_ref[s].astype(F32)
        delta, m_new, v_new = _adamw(w_ref[...], g, m_ref[...], v_ref[...])
        g_ref[...] = g
        d_ref[...] = delta
        m_out[...] = m_new
        v_out[...] = v_new

    lay = pl.BlockSpec((None, tr, c), lambda i: (layer, i, 0))
    return pl.pallas_call(
        body, name=name, grid=(r // tr,),
        in_specs=[pl.BlockSpec((NDEV, tr, c), lambda i: (0, i, 0)), lay, lay, lay] + [ANY] * 4,
        out_specs=[lay] * 4, out_shape=[jax.ShapeDtypeStruct(w.shape, F32)] * 4,
        input_output_aliases={4: 0, 5: 1, 6: 2, 7: 3},
        compiler_params=_cparams(("arbitrary",), 48),
    )(recv, w, m, v, *prev)


def _small_sum_and_adamw(name, recv, w, m, v):
    r = w.shape[0]

    def body(recv_ref, w_ref, m_ref, v_ref, g_ref, d_ref, m_out, v_out):
        g = recv_ref[0]
        for s in range(1, NDEV):
            g = g + recv_ref[s]
        delta, m_new, v_new = _adamw(w_ref[...], g, m_ref[...], v_ref[...])
        g_ref[...] = g
        d_ref[...] = delta
        m_out[...] = m_new
        v_out[...] = v_new

    return pl.pallas_call(body, name=name, out_shape=[jax.ShapeDtypeStruct((r, 128), F32)] * 4,
                          compiler_params=_cparams(None, 48))(recv, w, m, v)


BIG = ("w_in", "ret_proj", "sgu_proj", "w_out", "w_ffn_in", "w_ffn_out")
SMALL = ("norm_mix_w", "ret_gn_w", "sgu_ln_w", "sgu_ln_b", "sgu_w_s", "sgu_b_s", "norm_ffn_w", "final_norm_w")
WEIGHTS = ("norm_mix_w", "w_in", "ret_gn_w", "ret_proj", "sgu_ln_w", "sgu_ln_b", "sgu_w_s", "sgu_b_s", "sgu_proj",
           "w_out", "norm_ffn_w", "w_ffn_in", "w_ffn_out", "final_norm_w")


def _pack_small(d):
    return jnp.concatenate([d[k].reshape(-1, 128) for k in SMALL], axis=0)


def _unpack_small(packed, like):
    out, r0 = {}, 0
    for k in SMALL:
        r = like[k].size // 128
        out[k] = packed[r0:r0 + r].reshape(like[k].shape)
        r0 += r
    return out


def _layer_params(l, w, gathered, mask):
    p = dict(zip(BIG, gathered))
    for k in ("ret_proj", "sgu_proj", "w_out", "w_ffn_out"):
        p[k] = p[k].reshape(-1, D)
    for k in ("norm_mix_w", "ret_gn_w", "sgu_ln_w", "sgu_ln_b", "norm_ffn_w"):
        p[k] = w[k][l][None, :]
    wm = jnp.where(mask[None], w["sgu_w_s"][l], 0.0)
    p["wm"], p["wmt"] = wm.astype(BF16), jnp.swapaxes(wm, 1, 2).astype(BF16)
    p["bexp"] = jnp.repeat(w["sgu_b_s"][l].T, SCH, axis=1)
    p["maskf"] = mask.astype(F32)
    return p


def _step(w, m, v, x, target):
    depth = w["w_in"].shape[0]
    rc = _ret_consts(x.shape[0])
    pos = jnp.arange(SLEN)
    mask = (pos[None, :] // CHUNK) <= (pos[:, None] // CHUNK)
    shards = [[w[k][l].astype(BF16) for k in BIG] for l in range(depth)]

    gathered = _allgather("gather_l0", shards[0])
    saved, layers = [], []
    for l in range(depth):
        p = _layer_params(l, w, gathered, mask)
        if l + 1 < depth:
            lands = _place_own(f"own_w_l{l + 1}", shards[l + 1], True)
            started = _exchange_start(f"gather_start_l{l + 1}", shards[l + 1], lands, True, gathered[0])
            p["norm_mix_w"] = p["norm_mix_w"] + started["token"][0, 0]
        x, s = _layer_fwd(l, x, p, rc)
        if l + 1 < depth:
            gathered = _exchange_wait(f"gather_wait_l{l + 1}", started, True, x)
        saved.append(s)
        layers.append(p)

    dx, g_final, loss_row = _loss_bwd("loss", x, w["final_norm_w"][None, :], target)
    loss = lax.psum(loss_row[0, 0], ("x", "y", "c"))

    res = {k: [jnp.zeros(w[k].shape, F32) for _ in range(4)] for k in BIG}
    small = {k: [None] * depth for k in SMALL if k != "final_norm_w"}

    def scatter_start(tag, big, after):
        keys = [k for k in BIG if k in big]
        parts = [big[k] for k in keys]
        lands = _place_own(f"own_g_{tag}", parts, False)
        return keys, _exchange_start(f"scatter_start_{tag}", parts, lands, False, after)

    def finish(l, pending, after):
        for tag, (keys, started) in pending:
            recv = _exchange_wait(f"scatter_wait_{tag}", started, False, after)
            for k, rv in zip(keys, recv):
                res[k] = _sum_and_adamw(f"adamw_{k}_l{l}", rv, w[k], m[k], v[k], l, res[k])

    pending = None
    for l in reversed(range(depth)):
        dx1, big_ffn, sm_ffn = _layer_bwd_ffn(l, dx, layers[l], saved[l])
        now = [(f"ffn_l{l}", scatter_start(f"ffn_l{l}", big_ffn, dx1))]
        dx, big_mix, sm_mix = _layer_bwd_mix(l, dx1, layers[l], saved[l], rc)
        now.append((f"mix_l{l}", scatter_start(f"mix_l{l}", big_mix, dx)))
        if pending is not None:
            finish(l + 1, pending, dx)
        pending = now
        for k, g in {**sm_ffn, **sm_mix}.items():
            small[k][l] = g
    finish(0, pending, dx)

    g_small = {k: jnp.stack(small[k]).reshape(w[k].shape) for k in small}
    g_small["final_norm_w"] = g_final.reshape(w["final_norm_w"].shape)
    recv = _allgather("gather_small", [_pack_small(g_small)])[0]
    sres = _small_sum_and_adamw("adamw_small", recv, _pack_small(w), _pack_small(m), _pack_small(v))
    sres = [_unpack_small(a, w) for a in sres]

    outs = []
    for i in range(4):
        outs.append([sres[i][k] if k in SMALL else res[k][i] for k in WEIGHTS])
    return loss, dx, outs


def kernel(x, norm_mix_w, w_in, ret_gn_w, ret_proj, sgu_ln_w, sgu_ln_b, sgu_w_s, sgu_b_s, sgu_proj, w_out, norm_ffn_w, w_ffn_in, w_ffn_out, final_norm_w, loss_target, m_norm_mix_w, m_w_in, m_ret_gn_w, m_ret_proj, m_sgu_ln_w, m_sgu_ln_b, m_sgu_w_s, m_sgu_b_s, m_sgu_proj, m_w_out, m_norm_ffn_w, m_w_ffn_in, m_w_ffn_out, m_final_norm_w, v_norm_mix_w, v_w_in, v_ret_gn_w, v_ret_proj, v_sgu_ln_w, v_sgu_ln_b, v_sgu_w_s, v_sgu_b_s, v_sgu_proj, v_w_out, v_norm_ffn_w, v_w_ffn_in, v_w_ffn_out, v_final_norm_w):
    w = dict(zip(WEIGHTS, (norm_mix_w, w_in, ret_gn_w, ret_proj, sgu_ln_w, sgu_ln_b, sgu_w_s, sgu_b_s, sgu_proj, w_out,
                           norm_ffn_w, w_ffn_in, w_ffn_out, final_norm_w)))
    m = dict(zip(WEIGHTS, (m_norm_mix_w, m_w_in, m_ret_gn_w, m_ret_proj, m_sgu_ln_w, m_sgu_ln_b, m_sgu_w_s, m_sgu_b_s,
                           m_sgu_proj, m_w_out, m_norm_ffn_w, m_w_ffn_in, m_w_ffn_out, m_final_norm_w)))
    v = dict(zip(WEIGHTS, (v_norm_mix_w, v_w_in, v_ret_gn_w, v_ret_proj, v_sgu_ln_w, v_sgu_ln_b, v_sgu_w_s, v_sgu_b_s,
                           v_sgu_proj, v_w_out, v_norm_ffn_w, v_w_ffn_in, v_w_ffn_out, v_final_norm_w)))
    loss, dx, (grads, deltas, new_m, new_v) = _step(w, m, v, x.reshape(x.shape[1:]), loss_target.reshape(x.shape[1:]))
    return (loss, dx[None], *grads, *deltas, *new_m, *new_v)
```

```python
import functools
import math

import jax
import jax.numpy as jnp
from jax import lax
from jax.experimental import pallas as pl
from jax.experimental.pallas import tpu as pltpu

F32, BF16 = jnp.float32, jnp.bfloat16

D = 2048
HEADS, DK, DV = 8, 128, 256
GROUPS, SLEN = 8, 128
SCH = D // GROUPS
DFF = 5632
CHUNK = 64
IN_COLS = 14336
NDEV = 8
SH_IN = IN_COLS // NDEV
SH_FF = 2 * DFF // NDEV
EPS = 1e-6
ROPE_BASE = 10000.0
RL = 256
Z_G, Z_SU, Z_SV, Z_GA, Z_GB = 2, 3, 4, 5, 6

ADAM_TILE_ELEMS = 256 * 1024
ADAM_LR, ADAM_B1, ADAM_B2, ADAM_EPS, ADAM_WD, ADAM_STEP = 0.001, 0.9, 0.999, 1e-08, 0.01, 10

VMEM_BYTES_V7X = 64 << 20
MESH_T = pl.DeviceIdType.MESH
ANY = pl.BlockSpec(memory_space=pl.ANY)
HBM_SPEC = pl.BlockSpec(memory_space=pltpu.HBM)
SEM_SPEC = pl.BlockSpec(memory_space=pltpu.SEMAPHORE)


def _cparams(sem, vmem_mb):
    assert (vmem_mb << 20) < VMEM_BYTES_V7X
    return pltpu.CompilerParams(dimension_semantics=sem, vmem_limit_bytes=vmem_mb << 20)


def _mm(name, a, b, *, grid, a_spec, b_spec, o_spec, o_block, out_shape, out_dtype, contract, res=None, after=None,
        vmem_mb=48):
    nk = grid[2]
    dn = (contract, ((), ()))
    has_res = res is not None

    def body(*refs):
        a_ref, b_ref = refs[:2]
        r_ref = refs[2] if has_res else None
        o_ref, acc_ref = refs[-2:]
        part = lax.dot_general(a_ref[...].astype(BF16), b_ref[...].astype(BF16), dn, preferred_element_type=F32)

        def finish(val):
            if has_res:
                val = val + r_ref[...].astype(F32)
            o_ref[...] = val.astype(o_ref.dtype)

        if nk == 1:
            finish(part)
        else:
            k = pl.program_id(2)

            @pl.when(k == 0)
            def _():
                acc_ref[...] = part

            @pl.when(k > 0)
            def _():
                acc_ref[...] += part

            @pl.when(k == nk - 1)
            def _():
                finish(acc_ref[...])

    ins = [a, b] + ([res] if has_res else []) + ([after] if after is not None else [])
    in_specs = [a_spec, b_spec] + ([o_spec] if has_res else []) + ([ANY] if after is not None else [])
    return pl.pallas_call(
        body, name=name, grid=grid, in_specs=in_specs, out_specs=o_spec,
        out_shape=jax.ShapeDtypeStruct(out_shape, out_dtype),
        scratch_shapes=[pltpu.VMEM(o_block if nk > 1 else (8, 128), F32)],
        compiler_params=_cparams(("parallel", "parallel", "arbitrary"), vmem_mb),
    )(*ins)


def _tile(n, want):
    t = min(n, want)
    assert n % t == 0, (n, want)
    return t


def _mm_nn(name, a, b, *, out_dtype, tm=1024, tn=1024, tk=None, res=None):
    (m, k), n = a.shape, b.shape[1]
    tm, tn, tk = _tile(m, tm), _tile(n, tn), k if tk is None else tk
    return _mm(name, a, b, grid=(m // tm, n // tn, k // tk),
               a_spec=pl.BlockSpec((tm, tk), lambda i, j, kk: (i, kk)),
               b_spec=pl.BlockSpec((tk, tn), lambda i, j, kk: (kk, j)),
               o_spec=pl.BlockSpec((tm, tn), lambda i, j, kk: (i, j)), o_block=(tm, tn),
               out_shape=(m, n), out_dtype=out_dtype, contract=((1,), (0,)), res=res)


def _mm_nt(name, a, b, *, out_dtype, tm=1024, tn=1024, tk=None, after=None):
    (m, k), n = a.shape, b.shape[0]
    tm, tn, tk = _tile(m, tm), _tile(n, tn), k if tk is None else tk
    return _mm(name, a, b, grid=(m // tm, n // tn, k // tk),
               a_spec=pl.BlockSpec((tm, tk), lambda i, j, kk: (i, kk)),
               b_spec=pl.BlockSpec((tn, tk), lambda i, j, kk: (j, kk)),
               o_spec=pl.BlockSpec((tm, tn), lambda i, j, kk: (i, j)), o_block=(tm, tn),
               out_shape=(m, n), out_dtype=out_dtype, contract=((1,), (1,)), after=after)


def _mm_tn(name, a, b, *, out_dtype, tm=1024, tn=1024, tk=1024):
    (k, m), n = a.shape, b.shape[1]
    tm, tn, tk = _tile(m, tm), _tile(n, tn), _tile(k, tk)
    return _mm(name, a, b, grid=(m // tm, n // tn, k // tk),
               a_spec=pl.BlockSpec((tk, tm), lambda i, j, kk: (kk, i)),
               b_spec=pl.BlockSpec((tk, tn), lambda i, j, kk: (kk, j)),
               o_spec=pl.BlockSpec((tm, tn), lambda i, j, kk: (i, j)), o_block=(tm, tn),
               out_shape=(m, n), out_dtype=out_dtype, contract=((0,), (0,)))


def _mm_nn_sh(name, a, bsh, *, out_dtype, tm=1024):
    (m, k), ns = a.shape, bsh.shape[2]
    tm = _tile(m, tm)
    return _mm(name, a, bsh, grid=(m // tm, NDEV, 1),
               a_spec=pl.BlockSpec((tm, k), lambda i, j, kk: (i, 0)),
               b_spec=pl.BlockSpec((None, k, ns), lambda i, j, kk: (j, 0, 0)),
               o_spec=pl.BlockSpec((tm, ns), lambda i, j, kk: (i, j)), o_block=(tm, ns),
               out_shape=(m, NDEV * ns), out_dtype=out_dtype, contract=((1,), (0,)), vmem_mb=56)


def _mm_nt_sh(name, a, bsh, *, out_dtype, tm=1024, tn=1024):
    m, (_, n, ns) = a.shape[0], bsh.shape
    tm, tn = _tile(m, tm), _tile(n, tn)
    return _mm(name, a, bsh, grid=(m // tm, n // tn, NDEV),
               a_spec=pl.BlockSpec((tm, ns), lambda i, j, kk: (i, kk)),
               b_spec=pl.BlockSpec((None, tn, ns), lambda i, j, kk: (kk, j, 0)),
               o_spec=pl.BlockSpec((tm, tn), lambda i, j, kk: (i, j)), o_block=(tm, tn),
               out_shape=(m, n), out_dtype=out_dtype, contract=((1,), (1,)))


def _mm_tn_sh(name, a, b, ns, *, out_dtype, tm=1024, tk=1024):
    k, m = a.shape
    tm, tk = _tile(m, tm), _tile(k, tk)
    return _mm(name, a, b, grid=(m // tm, NDEV, k // tk),
               a_spec=pl.BlockSpec((tk, tm), lambda i, j, kk: (kk, i)),
               b_spec=pl.BlockSpec((tk, ns), lambda i, j, kk: (kk, j)),
               o_spec=pl.BlockSpec((None, tm, ns), lambda i, j, kk: (j, i, 0)), o_block=(tm, ns),
               out_shape=(NDEV, m, ns), out_dtype=out_dtype, contract=((0,), (0,)))


def _tilemap(name, fn, tiled_ins, full_ins, tiled_outs, acc_shapes, *, rows, tm, ncol=1, vmem_mb=48):
    n_ti, n_fi, n_to, n_ao = len(tiled_ins), len(full_ins), len(tiled_outs), len(acc_shapes)
    assert ncol == 1 or n_ao == 0

    def body(*refs):
        t_in, f_in = refs[:n_ti], refs[n_ti:n_ti + n_fi]
        t_out, a_out = refs[n_ti + n_fi:n_ti + n_fi + n_to], refs[n_ti + n_fi + n_to:]
        outs = fn(*[r[...] for r in t_in], *[r[...] for r in f_in])
        for r, v in zip(t_out, outs[:n_to]):
            r[...] = v.astype(r.dtype)
        if n_ao:
            i = pl.program_id(1)

            @pl.when(i == 0)
            def _():
                for r, v in zip(a_out, outs[n_to:]):
                    r[...] = v

            @pl.when(i > 0)
            def _():
                for r, v in zip(a_out, outs[n_to:]):
                    r[...] += v

    in_specs = [pl.BlockSpec((tm, bw), lambda j, i, base=base: (i, base + j)) for (_, bw, base) in tiled_ins]
    in_specs += [pl.BlockSpec(a.shape, lambda j, i, nd=a.ndim: (0,) * nd) for a in full_ins]
    out_specs = [pl.BlockSpec((tm, bw), lambda j, i: (i, j)) for (_, bw, _) in tiled_outs]
    out_specs += [pl.BlockSpec(s, lambda j, i: (0, 0)) for s in acc_shapes]
    out_shape = [jax.ShapeDtypeStruct((rows, w), dt) for (w, _, dt) in tiled_outs]
    out_shape += [jax.ShapeDtypeStruct(s, F32) for s in acc_shapes]
    return pl.pallas_call(
        body, name=name, grid=(ncol, rows // tm), in_specs=in_specs, out_specs=out_specs, out_shape=out_shape,
        compiler_params=_cparams(("parallel", "arbitrary"), vmem_mb),
    )(*[t[0] for t in tiled_ins], *full_ins)


def _f_rms(x, w):
    return x * lax.rsqrt(jnp.mean(x * x, axis=-1, keepdims=True) + EPS) * w


def _rms_fwd(name, x, w):
    rows = x.shape[0]
    return _tilemap(name, lambda xv, wv: (_f_rms(xv, wv),), [(x, D, 0)], [w], [(D, D, BF16)], [],
                    rows=rows, tm=_tile(rows, 256))[0]


def _rms_bwd(name, x, w, dh, dres):
    rows = x.shape[0]

    def fn(xv, dhv, drv, wv):
        _, vjp = jax.vjp(_f_rms, xv, wv)
        dx, dw = vjp(dhv.astype(F32))
        return dx + drv, dw

    return _tilemap(name, fn, [(x, D, 0), (dh, D, 0), (dres, D, 0)], [w], [(D, D, F32)], [(1, D)],
                    rows=rows, tm=_tile(rows, 256))


def _f_mixa_head(g, o, gw):
    mu = jnp.mean(o, axis=-1, keepdims=True)
    d = o - mu
    var = jnp.mean(d * d, axis=-1, keepdims=True)
    return g * jax.nn.sigmoid(g) * (d * lax.rsqrt(var + EPS) * gw)


def _heads(width):
    return [slice(h * width, (h + 1) * width) for h in range(HEADS)]


def _mixa_fwd(name, z, o, gw):
    rows = z.shape[0]

    def fn(g, ov, gwv):
        g, ov = g.astype(F32), ov.astype(F32)
        return (jnp.concatenate([_f_mixa_head(g[:, s], ov[:, s], gwv[:, s]) for s in _heads(DV)], axis=1),)

    return _tilemap(name, fn, [(z, D, Z_G), (o, D, 0)], [gw], [(D, D, BF16)], [], rows=rows, tm=_tile(rows, 256))[0]


def _mixa_bwd(name, z, o, gw, da):
    rows = z.shape[0]

    def fn(g, ov, dav, gwv):
        g, ov, dav = g.astype(F32), ov.astype(F32), dav.astype(F32)
        parts = []
        for s in _heads(DV):
            _, vjp = jax.vjp(_f_mixa_head, g[:, s], ov[:, s], gwv[:, s])
            parts.append(vjp(dav[:, s]))
        return tuple(jnp.concatenate([p[i] for p in parts], axis=1) for i in range(3))

    return _tilemap(name, fn, [(z, D, Z_G), (o, D, 0), (da, D, 0)], [gw], [(D, D, BF16), (D, D, BF16)], [(1, D)],
                    rows=rows, tm=_tile(rows, 256))


def _f_merge(ga, gb, a, b):
    return jax.nn.sigmoid(ga) * a + jax.nn.sigmoid(gb) * b


def _merge_fwd(name, z, a, b):
    rows, bw = z.shape[0], 1024

    def fn(ga, gb, av, bv):
        return (_f_merge(ga.astype(F32), gb.astype(F32), av.astype(F32), bv.astype(F32)),)

    return _tilemap(name, fn, [(z, bw, Z_GA * 2), (z, bw, Z_GB * 2), (a, bw, 0), (b, bw, 0)], [], [(D, bw, BF16)], [],
                    rows=rows, tm=_tile(rows, 512), ncol=D // bw)[0]


def _merge_bwd(name, z, a, b, dm):
    rows, bw = z.shape[0], 1024

    def fn(ga, gb, av, bv, dmv):
        _, vjp = jax.vjp(_f_merge, ga.astype(F32), gb.astype(F32), av.astype(F32), bv.astype(F32))
        return vjp(dmv.astype(F32))

    return _tilemap(name, fn, [(z, bw, Z_GA * 2), (z, bw, Z_GB * 2), (a, bw, 0), (b, bw, 0), (dm, bw, 0)], [],
                    [(D, bw, BF16)] * 4, [], rows=rows, tm=_tile(rows, 512), ncol=D // bw)


def _f_swiglu(a, c):
    return a * jax.nn.sigmoid(a) * c


def _swiglu_fwd(name, ac):
    rows = ac.shape[0]

    def fn(a, c):
        return (_f_swiglu(a.astype(F32), c.astype(F32)),)

    return _tilemap(name, fn, [(ac, SH_FF, 0), (ac, SH_FF, 4)], [], [(DFF, SH_FF, BF16)], [],
                    rows=rows, tm=_tile(rows, 512), ncol=4)[0]


def _swiglu_bwd(name, ac, df):
    rows = ac.shape[0]

    def fn(a, c, dfv):
        _, vjp = jax.vjp(_f_swiglu, a.astype(F32), c.astype(F32))
        return (jnp.concatenate(vjp(dfv.astype(F32)), axis=1),)

    return _tilemap(name, fn, [(ac, DFF, 0), (ac, DFF, 1), (df, DFF, 0)], [], [(2 * DFF, 2 * DFF, BF16)], [],
                    rows=rows, tm=_tile(rows, 128))[0]


def _loss_bwd(name, x, w, target):
    rows = x.shape[0]

    def fn(xv, tv, wv):
        y, vjp = jax.vjp(_f_rms, xv, wv)
        e = y - tv
        dx, dw = vjp(e * (1.0 / D))
        per_row = jnp.mean(e * e, axis=-1, keepdims=True)
        return dx, dw, jnp.broadcast_to(0.5 * jnp.sum(per_row, axis=0, keepdims=True), (1, 128))

    return _tilemap(name, fn, [(x, D, 0), (target, D, 0)], [w], [(D, D, F32)], [(1, D), (1, 128)],
                    rows=rows, tm=_tile(rows, 256))


def _gelu(v):
    return 0.5 * v * (1.0 + lax.erf(v * (1.0 / math.sqrt(2.0))))


def _f_sgu_pre(su, sv, lw, lb):
    zv = _gelu(sv)
    mu = jnp.mean(zv, axis=-1, keepdims=True)
    d = zv - mu
    var = jnp.mean(d * d, axis=-1, keepdims=True)
    return _gelu(su), d * lax.rsqrt(var + EPS) * lw + lb


def _sgu_blocks(tm):
    return [slice(r * SLEN, (r + 1) * SLEN) for r in range(tm // SLEN)]


def _sgu_mix(wm, vnb, bexp, tm):
    rows = []
    for r in _sgu_blocks(tm):
        cols = [jnp.dot(wm[g], vnb[r, g * SCH:(g + 1) * SCH], preferred_element_type=F32) for g in range(GROUPS)]
        rows.append(jnp.concatenate(cols, axis=1) + bexp)
    return jnp.concatenate(rows, axis=0)


def _sgu_fwd(name, z, lw, lb, wm, bexp):
    rows = z.shape[0]
    tm = _tile(rows, 256)

    def fn(su, sv, lwv, lbv, wmv, bev):
        zu, vn = _f_sgu_pre(su.astype(F32), sv.astype(F32), lwv, lbv)
        return (zu * _sgu_mix(wmv, vn.astype(BF16), bev, tm),)

    return _tilemap(name, fn, [(z, D, Z_SU), (z, D, Z_SV)], [lw, lb, wm, bexp], [(D, D, BF16)], [], rows=rows, tm=tm)[0]


def _sgu_bwd(name, z, lw, lb, wm, wmt, bexp, maskf, db):
    rows = z.shape[0]
    tm = _tile(rows, 256)

    def fn(su, sv, dbv, lwv, lbv, wmv, wmtv, bev, mkv):
        (zu, vn), vjp = jax.vjp(_f_sgu_pre, su.astype(F32), sv.astype(F32), lwv, lbv)
        vnb = vn.astype(BF16)
        dbv = dbv.astype(F32)
        dmixed = dbv * zu
        dzu = dbv * _sgu_mix(wmv, vnb, bev, tm)
        dmb = dmixed.astype(BF16)
        dvn_rows, dws, dbf = [], [None] * GROUPS, None
        for r in _sgu_blocks(tm):
            cols = []
            for g in range(GROUPS):
                cs = slice(g * SCH, (g + 1) * SCH)
                cols.append(jnp.dot(wmtv[g], dmb[r, cs], preferred_element_type=F32))
                dw = lax.dot_general(dmb[r, cs], vnb[r, cs], (((1,), (1,)), ((), ())), preferred_element_type=F32)
                dws[g] = dw if dws[g] is None else dws[g] + dw
            dvn_rows.append(jnp.concatenate(cols, axis=1))
            dbf = dmixed[r] if dbf is None else dbf + dmixed[r]
        dsu, dsv, dlw, dlb = vjp((dzu, jnp.concatenate(dvn_rows, axis=0)))
        dws_all = jnp.concatenate([dw * mkv for dw in dws], axis=0)
        return dsu, dsv, dws_all, dbf, dlw, dlb

    return _tilemap(name, fn, [(z, D, Z_SU), (z, D, Z_SV), (db, D, 0)], [lw, lb, wm, wmt, bexp, maskf],
                    [(D, D, BF16), (D, D, BF16)], [(GROUPS * SLEN, SLEN), (SLEN, D), (1, D), (1, D)], rows=rows, tm=tm)


def _group_sums(name, dbf):
    def body(e_ref, x_ref, o_ref):
        o_ref[...] = lax.dot_general(e_ref[...], x_ref[...], (((1,), (1,)), ((), ())),
                                     precision=lax.Precision.HIGHEST, preferred_element_type=F32)

    ind = (jnp.arange(D)[None, :] // SCH == jnp.arange(GROUPS)[:, None]).astype(F32)
    return pl.pallas_call(body, name=name, out_shape=jax.ShapeDtypeStruct((GROUPS, SLEN), F32))(ind, dbf)


def _ret_consts(rows):
    hh = jnp.arange(HEADS, dtype=F32)
    log_g = jnp.log1p(-(2.0 ** (-5.0 - hh)))
    idx = jnp.arange(RL, dtype=F32)
    ci = jnp.arange(RL) // CHUNK
    mask = ci[None, :] <= ci[:, None]
    dm = jnp.where(mask[None], jnp.exp(log_g[:, None, None] * jnp.abs(idx[:, None] - idx[None, :])), 0.0)
    qd = jnp.broadcast_to(jnp.exp(log_g[:, None] * (idx[None, :] + 1.0))[:, :, None], (HEADS, RL, DK))
    kd = jnp.broadcast_to(jnp.exp(log_g[:, None] * (RL - 1.0 - idx[None, :]))[:, :, None], (HEADS, RL, DK))
    cd = jnp.broadcast_to(jnp.exp(log_g * RL)[:, None, None], (HEADS, DK, DV))
    half = DK // 2
    inv = ROPE_BASE ** (-jnp.arange(half, dtype=F32) / half)
    ang = jnp.arange(rows, dtype=jnp.int32).astype(F32)[:, None] * inv[None, :]
    cos, sin = jnp.cos(ang), jnp.sin(ang)
    return dict(dm=dm.astype(F32), qd=qd, kd=kd, cd=cd, cos=jnp.concatenate([cos, cos], axis=1),
                sin=jnp.concatenate([-sin, sin], axis=1))


def _rope(x, cos, sin_signed):
    return x * cos + pltpu.roll(x, DK // 2, 1) * sin_signed


def _rope_t(d, cos, sin_signed):
    return d * cos + pltpu.roll(d * sin_signed, DK // 2, 1)


def _dot(a, b, ca, cb):
    return lax.dot_general(a, b, (((ca,), (cb,)), ((), ())), preferred_element_type=F32)


def _ret_specs(nb, rev):
    blk = (lambda n: nb - 1 - n) if rev else (lambda n: n)
    return dict(
        q=pl.BlockSpec((RL, DK), lambda h, n: (blk(n), h)),
        k=pl.BlockSpec((RL, DK), lambda h, n: (blk(n), HEADS + h)),
        v=pl.BlockSpec((RL, DV), lambda h, n: (blk(n), HEADS + h)),
        o=pl.BlockSpec((RL, DV), lambda h, n: (blk(n), h)),
        rope=pl.BlockSpec((RL, DK), lambda h, n: (blk(n), 0)),
        dm=pl.BlockSpec((None, RL, RL), lambda h, n: (h, 0, 0)),
        qd=pl.BlockSpec((None, RL, DK), lambda h, n: (h, 0, 0)),
        cd=pl.BlockSpec((None, DK, DV), lambda h, n: (h, 0, 0)),
        st=pl.BlockSpec((None, None, DK, DV), lambda h, n: (blk(n), h, 0, 0)),
    )


def _ret_fwd(name, z, rc):
    rows = z.shape[0]
    nb = rows // RL
    sp = _ret_specs(nb, False)

    def body(q_ref, k_ref, v_ref, cos_ref, sin_ref, dm_ref, qd_ref, kd_ref, cd_ref, o_ref, st_ref, state):
        @pl.when(pl.program_id(1) == 0)
        def _():
            state[...] = jnp.zeros_like(state)

        cos, sin = cos_ref[...], sin_ref[...]
        qf = _rope(q_ref[...].astype(F32), cos, sin) * (DK ** -0.5)
        kf = _rope(k_ref[...].astype(F32), cos, sin)
        vb = v_ref[...]
        s0 = state[...]
        s0b = s0.astype(BF16)
        st_ref[...] = s0b
        sc = _dot(qf.astype(BF16), kf.astype(BF16), 1, 1) * dm_ref[...]
        o = _dot(sc.astype(BF16), vb, 1, 0) + _dot((qf * qd_ref[...]).astype(BF16), s0b, 1, 0)
        o_ref[...] = o.astype(o_ref.dtype)
        state[...] = cd_ref[...] * s0 + _dot((kf * kd_ref[...]).astype(BF16), vb, 0, 0)

    return pl.pallas_call(
        body, name=name, grid=(HEADS, nb),
        in_specs=[sp["q"], sp["k"], sp["v"], sp["rope"], sp["rope"], sp["dm"], sp["qd"], sp["qd"], sp["cd"]],
        out_specs=[sp["o"], sp["st"]],
        out_shape=[jax.ShapeDtypeStruct((rows, HEADS * DV), BF16), jax.ShapeDtypeStruct((nb, HEADS, DK, DV), BF16)],
        scratch_shapes=[pltpu.VMEM((DK, DV), F32)],
        compiler_params=_cparams(("parallel", "arbitrary"), 32),
    )(z, z, z, rc["cos"], rc["sin"], rc["dm"], rc["qd"], rc["kd"], rc["cd"])


def _ret_bwd(name, z, rc, states, do):
    rows = z.shape[0]
    nb = rows // RL
    sp = _ret_specs(nb, True)

    def body(q_ref, k_ref, v_ref, cos_ref, sin_ref, dm_ref, qd_ref, kd_ref, cd_ref, st_ref, do_ref,
             dq_ref, dk_ref, dv_ref, dstate):
        @pl.when(pl.program_id(1) == 0)
        def _():
            dstate[...] = jnp.zeros_like(dstate)

        cos, sin, dm, qd, kd = cos_ref[...], sin_ref[...], dm_ref[...], qd_ref[...], kd_ref[...]
        qf = _rope(q_ref[...].astype(F32), cos, sin) * (DK ** -0.5)
        kf = _rope(k_ref[...].astype(F32), cos, sin)
        qb, kb, vb, dob, s0b = qf.astype(BF16), kf.astype(BF16), v_ref[...], do_ref[...], st_ref[...]
        qsb, ksb = (qf * qd).astype(BF16), (kf * kd).astype(BF16)
        ds1 = dstate[...]
        ds1b = ds1.astype(BF16)
        scb = (_dot(qb, kb, 1, 1) * dm).astype(BF16)
        dscb = (_dot(dob, vb, 1, 1) * dm).astype(BF16)
        dq = _dot(dscb, kb, 1, 0) + _dot(dob, s0b, 1, 1) * qd
        dk = _dot(dscb, qb, 0, 0) + _dot(vb, ds1b, 1, 1) * kd
        dv = _dot(scb, dob, 0, 0) + _dot(ksb, ds1b, 1, 0)
        dq_ref[...] = _rope_t(dq * (DK ** -0.5), cos, sin).astype(dq_ref.dtype)
        dk_ref[...] = _rope_t(dk, cos, sin).astype(dk_ref.dtype)
        dv_ref[...] = dv.astype(dv_ref.dtype)
        dstate[...] = cd_ref[...] * ds1 + _dot(qsb, dob, 0, 0)

    dqk = pl.BlockSpec((RL, DK), lambda h, n: (nb - 1 - n, h))
    return pl.pallas_call(
        body, name=name, grid=(HEADS, nb),
        in_specs=[sp["q"], sp["k"], sp["v"], sp["rope"], sp["rope"], sp["dm"], sp["qd"], sp["qd"], sp["cd"], sp["st"], sp["o"]],
        out_specs=[dqk, dqk, sp["o"]],
        out_shape=[jax.ShapeDtypeStruct((rows, HEADS * DK), BF16)] * 2 + [jax.ShapeDtypeStruct((rows, HEADS * DV), BF16)],
        scratch_shapes=[pltpu.VMEM((DK, DV), F32)],
        compiler_params=_cparams(("parallel", "arbitrary"), 32),
    )(z, z, z, rc["cos"], rc["sin"], rc["dm"], rc["qd"], rc["kd"], rc["cd"], states, do)


def _layer_fwd(l, x, p, rc):
    n = f"l{l}_"
    h = _rms_fwd(n + "rms1", x, p["norm_mix_w"])
    z = _mm_nn_sh(n + "in_proj", h, p["w_in"], out_dtype=BF16)
    o, states = _ret_fwd(n + "ret", z, rc)
    a_in = _mixa_fwd(n + "mixa", z, o, p["ret_gn_w"])
    b_in = _sgu_fwd(n + "sgu", z, p["sgu_ln_w"], p["sgu_ln_b"], p["wm"], p["bexp"])
    br_a = _mm_nn(n + "ret_proj", a_in, p["ret_proj"], out_dtype=BF16)
    br_b = _mm_nn(n + "sgu_proj", b_in, p["sgu_proj"], out_dtype=BF16)
    merged = _merge_fwd(n + "merge", z, br_a, br_b)
    x1 = _mm_nn(n + "out_proj", merged, p["w_out"], out_dtype=F32, res=x)
    h2 = _rms_fwd(n + "rms2", x1, p["norm_ffn_w"])
    ac = _mm_nn_sh(n + "ffn_in", h2, p["w_ffn_in"], out_dtype=BF16)
    f = _swiglu_fwd(n + "swiglu", ac)
    x2 = _mm_nn(n + "ffn_out", f, p["w_ffn_out"], out_dtype=F32, res=x1, tk=SH_FF)
    saved = dict(x=x, h=h, z=z, o=o, states=states, a_in=a_in, b_in=b_in, br_a=br_a, br_b=br_b, merged=merged,
                 x1=x1, h2=h2, ac=ac, f=f)
    return x2, saved


def _layer_bwd_ffn(l, dx2, p, s, after):
    n = f"l{l}b_"
    df = _mm_nt(n + "d_f", dx2, p["w_ffn_out"], out_dtype=BF16, tn=SH_FF, after=after)
    g_ffn_out = _mm_tn(n + "g_ffn_out", s["f"], dx2, out_dtype=BF16, tm=SH_FF)
    dac = _swiglu_bwd(n + "swiglu", s["ac"], df)
    dh2 = _mm_nt_sh(n + "d_h2", dac, p["w_ffn_in"], out_dtype=BF16)
    g_ffn_in = _mm_tn_sh(n + "g_ffn_in", s["h2"], dac, SH_FF, out_dtype=BF16)
    dx1, g_nffn = _rms_bwd(n + "rms2", s["x1"], p["norm_ffn_w"], dh2, dx2)
    return dx1, dict(w_ffn_in=g_ffn_in, w_ffn_out=g_ffn_out.reshape(NDEV, -1, D)), dict(norm_ffn_w=g_nffn)


def _layer_bwd_mix(l, dx1, p, s, rc, after):
    n = f"l{l}b_"
    dmerged = _mm_nt(n + "d_merged", dx1, p["w_out"], out_dtype=BF16, after=after)
    g_out = _mm_tn(n + "g_out", s["merged"], dx1, out_dtype=BF16)
    dga, dgb, dbr_a, dbr_b = _merge_bwd(n + "merge", s["z"], s["br_a"], s["br_b"], dmerged)
    da_in = _mm_nt(n + "d_a_in", dbr_a, p["ret_proj"], out_dtype=BF16)
    g_ret_proj = _mm_tn(n + "g_ret_proj", s["a_in"], dbr_a, out_dtype=BF16)
    db_in = _mm_nt(n + "d_b_in", dbr_b, p["sgu_proj"], out_dtype=BF16)
    g_sgu_proj = _mm_tn(n + "g_sgu_proj", s["b_in"], dbr_b, out_dtype=BF16)
    dg, do, g_gn = _mixa_bwd(n + "mixa", s["z"], s["o"], p["ret_gn_w"], da_in)
    dq, dk, dv = _ret_bwd(n + "ret", s["z"], rc, s["states"], do)
    dsu, dsv, g_ws, dbf, g_lw, g_lb = _sgu_bwd(n + "sgu", s["z"], p["sgu_ln_w"], p["sgu_ln_b"], p["wm"], p["wmt"],
                                                 p["bexp"], p["maskf"], db_in)
    g_bs = _group_sums(n + "g_bs", dbf)
    dz = jnp.concatenate([dq, dk, dv, dg, dsu, dsv, dga, dgb], axis=1)
    dh = _mm_nt_sh(n + "d_h", dz, p["w_in"], out_dtype=BF16)
    g_in = _mm_tn_sh(n + "g_in", s["h"], dz, SH_IN, out_dtype=BF16)
    dx, g_nmix = _rms_bwd(n + "rms1", s["x"], p["norm_mix_w"], dh, dx1)
    big = dict(w_in=g_in, ret_proj=g_ret_proj.reshape(NDEV, -1, D), sgu_proj=g_sgu_proj.reshape(NDEV, -1, D),
               w_out=g_out.reshape(NDEV, -1, D))
    small = dict(norm_mix_w=g_nmix, ret_gn_w=g_gn, sgu_ln_w=g_lw, sgu_ln_b=g_lb, sgu_w_s=g_ws, sgu_b_s=g_bs)
    return dx, big, small


def _slot(ref, p):
    return ref.at[4 * p[0] + 2 * p[1] + p[2]]


def _allgather(name, shards):
    n = len(shards)

    def body(*refs):
        ins, outs = refs[:n], refs[n:2 * n]
        send_sems, recv_sems, local_sems = refs[2 * n:]
        x, y, c = lax.axis_index("x"), lax.axis_index("y"), lax.axis_index("c")
        me, sibling = (x, y, c), (x, y, 1 - c)
        chips = [(1 - x, y), (x, 1 - y), (1 - x, 1 - y)]

        def copy(a, k, block, to, src=None):
            return pltpu.make_async_remote_copy(
                src_ref=_slot(outs[a], block) if src is None else src, dst_ref=_slot(outs[a], block),
                send_sem=send_sems.at[a, k], recv_sem=recv_sems.at[a, k], device_id=to, device_id_type=MESH_T)

        mine = [pltpu.make_async_copy(ins[a], _slot(outs[a], me), local_sems.at[a]) for a in range(n)]
        for cp in mine:
            cp.start()
        first = []
        for a in range(n):
            first.append(copy(a, 0, me, sibling, src=ins[a]))
            first += [copy(a, 1 + j, me, (*chip, c), src=ins[a]) for j, chip in enumerate(chips)]
        for cp in first:
            cp.start()
        passed = []
        for j, chip in enumerate(chips):
            for a in range(n):
                copy(a, 1 + j, (*chip, c), me).wait_recv()
                cp = copy(a, 4 + j, (*chip, c), sibling)
                cp.start()
                passed.append(cp)
        for a in range(n):
            copy(a, 0, sibling, me).wait_recv()
            for j, chip in enumerate(chips):
                copy(a, 4 + j, (*chip, 1 - c), me).wait_recv()
        for cp in first + passed:
            cp.wait_send()
        for cp in mine:
            cp.wait()

    return pl.pallas_call(
        body, name=name, in_specs=[ANY] * n, out_specs=[ANY] * n,
        out_shape=[jax.ShapeDtypeStruct((NDEV,) + s.shape, s.dtype) for s in shards],
        scratch_shapes=[pltpu.SemaphoreType.DMA((n, 7)), pltpu.SemaphoreType.DMA((n, 7)), pltpu.SemaphoreType.DMA((n,))],
    )(*shards)


def _peers(x, y, c):
    return [((1 - x) if fx else x, (1 - y) if fy else y, (1 - c) if fc else c)
            for fx in (0, 1) for fy in (0, 1) for fc in (0, 1) if fx + fy + fc]


def _exchange_copy(whole, src_ref, land_ref, send_sems, recv_sems, a, k, peer, me, landed):
    return pltpu.make_async_remote_copy(
        src_ref=src_ref if whole else _slot(src_ref, peer), dst_ref=_slot(land_ref, peer if landed else me),
        send_sem=send_sems.at[7 * a + k], recv_sem=recv_sems.at[7 * a + k], device_id=peer, device_id_type=MESH_T)


def _exchange_start(name, srcs, lands, whole, after):
    n = len(srcs)

    def body(*refs):
        src_refs, land_refs = refs[:n], refs[n:2 * n]
        send_sems, recv_sems = refs[2 * n + 1], refs[2 * n + 2]
        token = refs[-1]
        x, y, c = lax.axis_index("x"), lax.axis_index("y"), lax.axis_index("c")
        for a in range(n):
            for k, peer in enumerate(_peers(x, y, c)):
                _exchange_copy(whole, src_refs[a], land_refs[a], send_sems, recv_sems, a, k, peer, (x, y, c), False).start()
        token[...] = jnp.zeros_like(token)

    hbm = lambda t: pltpu.HBM(t.shape, t.dtype)
    out = pl.pallas_call(
        body, name=name, in_specs=[HBM_SPEC] * (2 * n) + [ANY],
        out_specs=(SEM_SPEC, SEM_SPEC, *[HBM_SPEC] * (2 * n), pl.BlockSpec(memory_space=pltpu.VMEM)),
        out_shape=(pltpu.SemaphoreType.DMA((7 * n,)), pltpu.SemaphoreType.DMA((7 * n,)), *[hbm(t) for t in srcs],
                   *[hbm(t) for t in lands], jax.ShapeDtypeStruct((8, 128), F32)),
        input_output_aliases={i: 2 + i for i in range(2 * n)},
        compiler_params=pltpu.CompilerParams(has_side_effects=pltpu.SideEffectType.DATAFLOW_SIDE_EFFECTING),
    )(*[pltpu.with_memory_space_constraint(t, pltpu.HBM) for t in (*srcs, *lands)], after)
    return dict(send=out[0], recv=out[1], srcs=out[2:2 + n], lands=out[2 + n:2 + 2 * n], token=out[-1])


def _exchange_wait(name, started, whole, after):
    n = len(started["srcs"])

    def body(*refs):
        src_refs, land_refs = refs[:n], refs[n:2 * n]
        send_sems, recv_sems = refs[2 * n], refs[2 * n + 1]
        x, y, c = lax.axis_index("x"), lax.axis_index("y"), lax.axis_index("c")
        for a in range(n):
            for k, peer in enumerate(_peers(x, y, c)):
                cp = _exchange_copy(whole, src_refs[a], land_refs[a], send_sems, recv_sems, a, k, peer, (x, y, c), True)
                cp.wait_send()
                cp.wait_recv()

    hbm = lambda t: pltpu.HBM(t.shape, t.dtype)
    out = pl.pallas_call(
        body, name=name, in_specs=[HBM_SPEC] * (2 * n) + [SEM_SPEC, SEM_SPEC, ANY], out_specs=[HBM_SPEC] * (2 * n),
        out_shape=[hbm(t) for t in (*started["srcs"], *started["lands"])],
        input_output_aliases={i: i for i in range(2 * n)},
        compiler_params=pltpu.CompilerParams(has_side_effects=pltpu.SideEffectType.DATAFLOW_SIDE_EFFECTING),
    )(*started["srcs"], *started["lands"], started["send"], started["recv"], after)
    return out[n:]


def _adamw(w, g, m, v):
    m = ADAM_B1 * m + (1.0 - ADAM_B1) * g
    v = ADAM_B2 * v + (1.0 - ADAM_B2) * (g * g)
    m_hat = m / (1.0 - ADAM_B1 ** ADAM_STEP)
    v_hat = v / (1.0 - ADAM_B2 ** ADAM_STEP)
    return -ADAM_LR * (m_hat / (jnp.sqrt(v_hat) + ADAM_EPS) + ADAM_WD * w), m, v


def _sum_and_adamw(name, recv, w, m, v, layer, prev):
    _, r, c = recv.shape
    tr = max(t for t in range(16, r + 1, 16) if r % t == 0 and t * c <= ADAM_TILE_ELEMS)

    def body(recv_ref, w_ref, m_ref, v_ref, *rest):
        g_ref, d_ref, m_out, v_out = rest[4:]
        g = recv_ref[0].astype(F32)
        for s in range(1, NDEV):
            g = g + recv_ref[s].astype(F32)
        delta, m_new, v_new = _adamw(w_ref[...], g, m_ref[...], v_ref[...])
        g_ref[...] = g
        d_ref[...] = delta
        m_out[...] = m_new
        v_out[...] = v_new

    lay = pl.BlockSpec((None, tr, c), lambda i: (layer, i, 0))
    return pl.pallas_call(
        body, name=name, grid=(r // tr,),
        in_specs=[pl.BlockSpec((NDEV, tr, c), lambda i: (0, i, 0)), lay, lay, lay] + [ANY] * 4,
        out_specs=[lay] * 4, out_shape=[jax.ShapeDtypeStruct(w.shape, F32)] * 4,
        input_output_aliases={4: 0, 5: 1, 6: 2, 7: 3},
        compiler_params=_cparams(("arbitrary",), 48),
    )(recv, w, m, v, *prev)


def _small_sum_and_adamw(name, recv, w, m, v):
    r = w.shape[0]

    def body(recv_ref, w_ref, m_ref, v_ref, g_ref, d_ref, m_out, v_out):
        g = recv_ref[0]
        for s in range(1, NDEV):
            g = g + recv_ref[s]
        delta, m_new, v_new = _adamw(w_ref[...], g, m_ref[...], v_ref[...])
        g_ref[...] = g
        d_ref[...] = delta
        m_out[...] = m_new
        v_out[...] = v_new

    return pl.pallas_call(body, name=name, out_shape=[jax.ShapeDtypeStruct((r, 128), F32)] * 4,
                          compiler_params=_cparams(None, 48))(recv, w, m, v)


BIG = ("w_in", "ret_proj", "sgu_proj", "w_out", "w_ffn_in", "w_ffn_out")
SMALL = ("norm_mix_w", "ret_gn_w", "sgu_ln_w", "sgu_ln_b", "sgu_w_s", "sgu_b_s", "norm_ffn_w", "final_norm_w")
WEIGHTS = ("norm_mix_w", "w_in", "ret_gn_w", "ret_proj", "sgu_ln_w", "sgu_ln_b", "sgu_w_s", "sgu_b_s", "sgu_proj",
           "w_out", "norm_ffn_w", "w_ffn_in", "w_ffn_out", "final_norm_w")


def _pack_small(d):
    return jnp.concatenate([d[k].reshape(-1, 128) for k in SMALL], axis=0)


def _unpack_small(packed, like):
    out, r0 = {}, 0
    for k in SMALL:
        r = like[k].size // 128
        out[k] = packed[r0:r0 + r].reshape(like[k].shape)
        r0 += r
    return out


def _layer_params(l, w, gathered, mask):
    p = dict(zip(BIG, gathered))
    for k in ("ret_proj", "sgu_proj", "w_out", "w_ffn_out"):
        p[k] = p[k].reshape(-1, D)
    for k in ("norm_mix_w", "ret_gn_w", "sgu_ln_w", "sgu_ln_b", "norm_ffn_w"):
        p[k] = w[k][l][None, :]
    wm = jnp.where(mask[None], w["sgu_w_s"][l], 0.0)
    p["wm"], p["wmt"] = wm.astype(BF16), jnp.swapaxes(wm, 1, 2).astype(BF16)
    p["bexp"] = jnp.repeat(w["sgu_b_s"][l].T, SCH, axis=1)
    p["maskf"] = mask.astype(F32)
    return p


def _step(w, m, v, x, target):
    depth = w["w_in"].shape[0]
    rc = _ret_consts(x.shape[0])
    pos = jnp.arange(SLEN)
    mask = (pos[None, :] // CHUNK) <= (pos[:, None] // CHUNK)
    shards = [[w[k][l].astype(BF16) for k in BIG] for l in range(depth)]
    me = 4 * lax.axis_index("x") + 2 * lax.axis_index("y") + lax.axis_index("c")

    gathered = _allgather("gather_l0", shards[0])
    saved, layers = [], []
    for l in range(depth):
        p = _layer_params(l, w, gathered, mask)
        if l + 1 < depth:
            lands = [lax.dynamic_update_index_in_dim(jnp.zeros((NDEV,) + t.shape, t.dtype), t, me, 0) for t in shards[l + 1]]
            started = _exchange_start(f"gather_start_l{l + 1}", shards[l + 1], lands, True, gathered[0])
            p["norm_mix_w"] = p["norm_mix_w"] + started["token"][0, 0]
        x, s = _layer_fwd(l, x, p, rc)
        if l + 1 < depth:
            gathered = _exchange_wait(f"gather_wait_l{l + 1}", started, True, x)
        saved.append(s)
        layers.append(p)

    dx, g_final, loss_row = _loss_bwd("loss", x, w["final_norm_w"][None, :], target)
    loss = lax.psum(loss_row[0, 0], ("x", "y", "c"))

    res = {k: [jnp.zeros(w[k].shape, F32) for _ in range(4)] for k in BIG}
    small = {k: [None] * depth for k in SMALL if k != "final_norm_w"}

    def scatter_start(tag, big, after):
        keys = [k for k in BIG if k in big]
        parts = [big[k] for k in keys]
        lands = [lax.dynamic_update_index_in_dim(jnp.zeros(t.shape, t.dtype), lax.dynamic_index_in_dim(t, me, 0), me, 0)
                 for t in parts]
        return keys, _exchange_start(f"scatter_start_{tag}", parts, lands, False, after)

    def finish(l, pending, after):
        for tag, (keys, started) in pending:
            recv = _exchange_wait(f"scatter_wait_{tag}", started, False, after)
            for k, rv in zip(keys, recv):
                res[k] = _sum_and_adamw(f"adamw_{k}_l{l}", rv, w[k], m[k], v[k], l, res[k])

    pending, token = None, loss_row
    for l in reversed(range(depth)):
        dx1, big_ffn, sm_ffn = _layer_bwd_ffn(l, dx, layers[l], saved[l], token)
        now = [(f"ffn_l{l}", scatter_start(f"ffn_l{l}", big_ffn, dx1))]
        dx, big_mix, sm_mix = _layer_bwd_mix(l, dx1, layers[l], saved[l], rc, now[0][1][1]["token"])
        now.append((f"mix_l{l}", scatter_start(f"mix_l{l}", big_mix, dx)))
        token = now[1][1][1]["token"]
        if pending is not None:
            finish(l + 1, pending, dx)
        pending = now
        for k, g in {**sm_ffn, **sm_mix}.items():
            small[k][l] = g
    finish(0, pending, dx)

    g_small = {k: jnp.stack(small[k]).reshape(w[k].shape) for k in small}
    g_small["final_norm_w"] = g_final.reshape(w["final_norm_w"].shape)
    recv = _allgather("gather_small", [_pack_small(g_small)])[0]
    sres = _small_sum_and_adamw("adamw_small", recv, _pack_small(w), _pack_small(m), _pack_small(v))
    sres = [_unpack_small(a, w) for a in sres]

    outs = []
    for i in range(4):
        outs.append([sres[i][k] if k in SMALL else res[k][i] for k in WEIGHTS])
    return loss, dx, outs


def kernel(x, norm_mix_w, w_in, ret_gn_w, ret_proj, sgu_ln_w, sgu_ln_b, sgu_w_s, sgu_b_s, sgu_proj, w_out, norm_ffn_w, w_ffn_in, w_ffn_out, final_norm_w, loss_target, m_norm_mix_w, m_w_in, m_ret_gn_w, m_ret_proj, m_sgu_ln_w, m_sgu_ln_b, m_sgu_w_s, m_sgu_b_s, m_sgu_proj, m_w_out, m_norm_ffn_w, m_w_ffn_in, m_w_ffn_out, m_final_norm_w, v_norm_mix_w, v_w_in, v_ret_gn_w, v_ret_proj, v_sgu_ln_w, v_sgu_ln_b, v_sgu_w_s, v_sgu_b_s, v_sgu_proj, v_w_out, v_norm_ffn_w, v_w_ffn_in, v_w_ffn_out, v_final_norm_w):
    w = dict(zip(WEIGHTS, (norm_mix_w, w_in, ret_gn_w, ret_proj, sgu_ln_w, sgu_ln_b, sgu_w_s, sgu_b_s, sgu_proj, w_out,
                           norm_ffn_w, w_ffn_in, w_ffn_out, final_norm_w)))
    m = dict(zip(WEIGHTS, (m_norm_mix_w, m_w_in, m_ret_gn_w, m_ret_proj, m_sgu_ln_w, m_sgu_ln_b, m_sgu_w_s, m_sgu_b_s,
                           m_sgu_proj, m_w_out, m_norm_ffn_w, m_w_ffn_in, m_w_ffn_out, m_final_norm_w)))
    v = dict(zip(WEIGHTS, (v_norm_mix_w, v_w_in, v_ret_gn_w, v_ret_proj, v_sgu_ln_w, v_sgu_ln_b, v_sgu_w_s, v_sgu_b_s,
                           v_sgu_proj, v_w_out, v_norm_ffn_w, v_w_ffn_in, v_w_ffn_out, v_final_norm_w)))
    loss, dx, (grads, deltas, new_m, new_v) = _step(w, m, v, x.reshape(x.shape[1:]), loss_target.reshape(x.shape[1:]))
    return (loss, dx[None], *grads, *deltas, *new_m, *new_v)
```

```python
import functools
import math

import jax
import jax.numpy as jnp
from jax import lax
from jax.experimental import pallas as pl
from jax.experimental.pallas import tpu as pltpu

F32, BF16 = jnp.float32, jnp.bfloat16

D = 2048
HEADS, DK, DV = 8, 128, 256
GROUPS, SLEN = 8, 128
SCH = D // GROUPS
DFF = 5632
CHUNK = 64
IN_COLS = 14336
NDEV = 8
SH_IN = IN_COLS // NDEV
SH_FF = 2 * DFF // NDEV
EPS = 1e-6
ROPE_BASE = 10000.0
RL = 256
Z_G, Z_SU, Z_SV, Z_GA, Z_GB = 2, 3, 4, 5, 6

ADAM_TILE_ELEMS = 256 * 1024
ADAM_LR, ADAM_B1, ADAM_B2, ADAM_EPS, ADAM_WD, ADAM_STEP = 0.001, 0.9, 0.999, 1e-08, 0.01, 10

VMEM_BYTES_V7X = 64 << 20
MESH_T = pl.DeviceIdType.MESH
ANY = pl.BlockSpec(memory_space=pl.ANY)
HBM_SPEC = pl.BlockSpec(memory_space=pltpu.HBM)
SEM_SPEC = pl.BlockSpec(memory_space=pltpu.SEMAPHORE)


def _cparams(sem, vmem_mb):
    assert (vmem_mb << 20) < VMEM_BYTES_V7X
    return pltpu.CompilerParams(dimension_semantics=sem, vmem_limit_bytes=vmem_mb << 20)


def _mm(name, a, b, *, grid, a_spec, b_spec, o_spec, o_block, out_shape, out_dtype, contract, res=None, after=None,
        split=1, vmem_mb=48):
    nk = grid[2]
    dn = (contract, ((), ()))
    has_res = res is not None

    def body(*refs):
        a_ref, b_ref = refs[:2]
        r_ref = refs[2] if has_res else None
        o_ref, acc_ref = refs[-2:]
        if split == 1:
            part = lax.dot_general(a_ref[...].astype(BF16), b_ref[...].astype(BF16), dn, preferred_element_type=F32)
        else:
            kp = a_ref.shape[1] // split
            part = sum(lax.dot_general(a_ref[:, s * kp:(s + 1) * kp].astype(BF16), b_ref[s].astype(BF16), dn,
                                       preferred_element_type=F32) for s in range(split))

        def finish(val):
            if has_res:
                val = val + r_ref[...].astype(F32)
            o_ref[...] = val.astype(o_ref.dtype)

        if nk == 1:
            finish(part)
        else:
            k = pl.program_id(2)

            @pl.when(k == 0)
            def _():
                acc_ref[...] = part

            @pl.when(k > 0)
            def _():
                acc_ref[...] += part

            @pl.when(k == nk - 1)
            def _():
                finish(acc_ref[...])

    ins = [a, b] + ([res] if has_res else []) + ([after] if after is not None else [])
    in_specs = [a_spec, b_spec] + ([o_spec] if has_res else []) + ([ANY] if after is not None else [])
    return pl.pallas_call(
        body, name=name, grid=grid, in_specs=in_specs, out_specs=o_spec,
        out_shape=jax.ShapeDtypeStruct(out_shape, out_dtype),
        scratch_shapes=[pltpu.VMEM(o_block if nk > 1 else (8, 128), F32)],
        compiler_params=_cparams(("parallel", "parallel", "arbitrary"), vmem_mb),
    )(*ins)


def _tile(n, want):
    t = min(n, want)
    assert n % t == 0, (n, want)
    return t


def _mm_nn(name, a, b, *, out_dtype, tm=1024, tn=1024, tk=None, res=None):
    (m, k), n = a.shape, b.shape[1]
    tm, tn, tk = _tile(m, tm), _tile(n, tn), k if tk is None else tk
    return _mm(name, a, b, grid=(m // tm, n // tn, k // tk),
               a_spec=pl.BlockSpec((tm, tk), lambda i, j, kk: (i, kk)),
               b_spec=pl.BlockSpec((tk, tn), lambda i, j, kk: (kk, j)),
               o_spec=pl.BlockSpec((tm, tn), lambda i, j, kk: (i, j)), o_block=(tm, tn),
               out_shape=(m, n), out_dtype=out_dtype, contract=((1,), (0,)), res=res)


def _mm_nt(name, a, b, *, out_dtype, tm=1024, tn=1024, tk=None, after=None):
    (m, k), n = a.shape, b.shape[0]
    tm, tn, tk = _tile(m, tm), _tile(n, tn), k if tk is None else tk
    return _mm(name, a, b, grid=(m // tm, n // tn, k // tk),
               a_spec=pl.BlockSpec((tm, tk), lambda i, j, kk: (i, kk)),
               b_spec=pl.BlockSpec((tn, tk), lambda i, j, kk: (j, kk)),
               o_spec=pl.BlockSpec((tm, tn), lambda i, j, kk: (i, j)), o_block=(tm, tn),
               out_shape=(m, n), out_dtype=out_dtype, contract=((1,), (1,)), after=after)


def _mm_tn(name, a, b, *, out_dtype, tm=1024, tn=1024, tk=1024):
    (k, m), n = a.shape, b.shape[1]
    tm, tn, tk = _tile(m, tm), _tile(n, tn), _tile(k, tk)
    return _mm(name, a, b, grid=(m // tm, n // tn, k // tk),
               a_spec=pl.BlockSpec((tk, tm), lambda i, j, kk: (kk, i)),
               b_spec=pl.BlockSpec((tk, tn), lambda i, j, kk: (kk, j)),
               o_spec=pl.BlockSpec((tm, tn), lambda i, j, kk: (i, j)), o_block=(tm, tn),
               out_shape=(m, n), out_dtype=out_dtype, contract=((0,), (0,)))


def _mm_nn_sh(name, a, bsh, *, out_dtype, tm=1024):
    (m, k), ns = a.shape, bsh.shape[2]
    tm = _tile(m, tm)
    return _mm(name, a, bsh, grid=(m // tm, NDEV, 1),
               a_spec=pl.BlockSpec((tm, k), lambda i, j, kk: (i, 0)),
               b_spec=pl.BlockSpec((None, k, ns), lambda i, j, kk: (j, 0, 0)),
               o_spec=pl.BlockSpec((tm, ns), lambda i, j, kk: (i, j)), o_block=(tm, ns),
               out_shape=(m, NDEV * ns), out_dtype=out_dtype, contract=((1,), (0,)), vmem_mb=56)


def _mm_nt_sh(name, a, bsh, *, out_dtype, tm=1024, tn=1024, after=None):
    m, (_, n, ns) = a.shape[0], bsh.shape
    tm, tn = _tile(m, tm), _tile(n, tn)
    return _mm(name, a, bsh, grid=(m // tm, n // tn, NDEV // 2),
               a_spec=pl.BlockSpec((tm, 2 * ns), lambda i, j, kk: (i, kk)),
               b_spec=pl.BlockSpec((2, tn, ns), lambda i, j, kk: (kk, j, 0)),
               o_spec=pl.BlockSpec((tm, tn), lambda i, j, kk: (i, j)), o_block=(tm, tn),
               out_shape=(m, n), out_dtype=out_dtype, contract=((1,), (1,)), after=after, split=2)


def _mm_tn_sh(name, a, b, ns, *, out_dtype, tm=1024, tk=2048):
    k, m = a.shape
    tm, tk = _tile(m, tm), _tile(k, tk)
    return _mm(name, a, b, grid=(m // tm, NDEV, k // tk),
               a_spec=pl.BlockSpec((tk, tm), lambda i, j, kk: (kk, i)),
               b_spec=pl.BlockSpec((tk, ns), lambda i, j, kk: (kk, j)),
               o_spec=pl.BlockSpec((None, tm, ns), lambda i, j, kk: (j, i, 0)), o_block=(tm, ns),
               out_shape=(NDEV, m, ns), out_dtype=out_dtype, contract=((0,), (0,)), vmem_mb=56)


def _tilemap(name, fn, tiled_ins, full_ins, tiled_outs, acc_shapes, *, rows, tm, ncol=1, vmem_mb=48):
    n_ti, n_fi, n_to, n_ao = len(tiled_ins), len(full_ins), len(tiled_outs), len(acc_shapes)
    assert ncol == 1 or n_ao == 0

    def body(*refs):
        t_in, f_in = refs[:n_ti], refs[n_ti:n_ti + n_fi]
        t_out, a_out = refs[n_ti + n_fi:n_ti + n_fi + n_to], refs[n_ti + n_fi + n_to:]
        outs = fn(*[r[...] for r in t_in], *[r[...] for r in f_in])
        for r, v in zip(t_out, outs[:n_to]):
            r[...] = v.astype(r.dtype)
        if n_ao:
            i = pl.program_id(1)

            @pl.when(i == 0)
            def _():
                for r, v in zip(a_out, outs[n_to:]):
                    r[...] = v

            @pl.when(i > 0)
            def _():
                for r, v in zip(a_out, outs[n_to:]):
                    r[...] += v

    in_specs = [pl.BlockSpec((tm, bw), lambda j, i, base=base: (i, base + j)) for (_, bw, base) in tiled_ins]
    in_specs += [pl.BlockSpec(a.shape, lambda j, i, nd=a.ndim: (0,) * nd) for a in full_ins]
    out_specs = [pl.BlockSpec((tm, bw), lambda j, i: (i, j)) for (_, bw, _) in tiled_outs]
    out_specs += [pl.BlockSpec(s, lambda j, i: (0, 0)) for s in acc_shapes]
    out_shape = [jax.ShapeDtypeStruct((rows, w), dt) for (w, _, dt) in tiled_outs]
    out_shape += [jax.ShapeDtypeStruct(s, F32) for s in acc_shapes]
    return pl.pallas_call(
        body, name=name, grid=(ncol, rows // tm), in_specs=in_specs, out_specs=out_specs, out_shape=out_shape,
        compiler_params=_cparams(("parallel", "arbitrary"), vmem_mb),
    )(*[t[0] for t in tiled_ins], *full_ins)


def _f_rms(x, w):
    return x * lax.rsqrt(jnp.mean(x * x, axis=-1, keepdims=True) + EPS) * w


def _rms_fwd(name, x, w):
    rows = x.shape[0]
    return _tilemap(name, lambda xv, wv: (_f_rms(xv, wv),), [(x, D, 0)], [w], [(D, D, BF16)], [],
                    rows=rows, tm=_tile(rows, 256))[0]


def _rms_bwd(name, x, w, dh, dres):
    rows = x.shape[0]

    def fn(xv, dhv, drv, wv):
        _, vjp = jax.vjp(_f_rms, xv, wv)
        dx, dw = vjp(dhv.astype(F32))
        return dx + drv, dw

    return _tilemap(name, fn, [(x, D, 0), (dh, D, 0), (dres, D, 0)], [w], [(D, D, F32)], [(1, D)],
                    rows=rows, tm=_tile(rows, 256))


def _f_mixa_head(g, o, gw):
    mu = jnp.mean(o, axis=-1, keepdims=True)
    d = o - mu
    var = jnp.mean(d * d, axis=-1, keepdims=True)
    return g * jax.nn.sigmoid(g) * (d * lax.rsqrt(var + EPS) * gw)


def _heads(width):
    return [slice(h * width, (h + 1) * width) for h in range(HEADS)]


def _mixa_fwd(name, z, o, gw):
    rows = z.shape[0]

    def fn(g, ov, gwv):
        g, ov = g.astype(F32), ov.astype(F32)
        return (jnp.concatenate([_f_mixa_head(g[:, s], ov[:, s], gwv[:, s]) for s in _heads(DV)], axis=1),)

    return _tilemap(name, fn, [(z, D, Z_G), (o, D, 0)], [gw], [(D, D, BF16)], [], rows=rows, tm=_tile(rows, 256))[0]


def _mixa_bwd(name, z, o, gw, da):
    rows = z.shape[0]

    def fn(g, ov, dav, gwv):
        g, ov, dav = g.astype(F32), ov.astype(F32), dav.astype(F32)
        parts = []
        for s in _heads(DV):
            _, vjp = jax.vjp(_f_mixa_head, g[:, s], ov[:, s], gwv[:, s])
            parts.append(vjp(dav[:, s]))
        return tuple(jnp.concatenate([p[i] for p in parts], axis=1) for i in range(3))

    return _tilemap(name, fn, [(z, D, Z_G), (o, D, 0), (da, D, 0)], [gw], [(D, D, BF16), (D, D, BF16)], [(1, D)],
                    rows=rows, tm=_tile(rows, 256))


def _f_merge(ga, gb, a, b):
    return jax.nn.sigmoid(ga) * a + jax.nn.sigmoid(gb) * b


def _merge_fwd(name, z, a, b):
    rows, bw = z.shape[0], 1024

    def fn(ga, gb, av, bv):
        return (_f_merge(ga.astype(F32), gb.astype(F32), av.astype(F32), bv.astype(F32)),)

    return _tilemap(name, fn, [(z, bw, Z_GA * 2), (z, bw, Z_GB * 2), (a, bw, 0), (b, bw, 0)], [], [(D, bw, BF16)], [],
                    rows=rows, tm=_tile(rows, 512), ncol=D // bw)[0]


def _merge_bwd(name, z, a, b, dm):
    rows, bw = z.shape[0], 1024

    def fn(ga, gb, av, bv, dmv):
        _, vjp = jax.vjp(_f_merge, ga.astype(F32), gb.astype(F32), av.astype(F32), bv.astype(F32))
        return vjp(dmv.astype(F32))

    return _tilemap(name, fn, [(z, bw, Z_GA * 2), (z, bw, Z_GB * 2), (a, bw, 0), (b, bw, 0), (dm, bw, 0)], [],
                    [(D, bw, BF16)] * 4, [], rows=rows, tm=_tile(rows, 512), ncol=D // bw)


def _f_swiglu(a, c):
    return a * jax.nn.sigmoid(a) * c


def _swiglu_fwd(name, ac):
    rows = ac.shape[0]

    def fn(a, c):
        return (_f_swiglu(a.astype(F32), c.astype(F32)),)

    return _tilemap(name, fn, [(ac, SH_FF, 0), (ac, SH_FF, 4)], [], [(DFF, SH_FF, BF16)], [],
                    rows=rows, tm=_tile(rows, 512), ncol=4)[0]


def _swiglu_bwd(name, ac, df):
    rows = ac.shape[0]

    def fn(a, c, dfv):
        _, vjp = jax.vjp(_f_swiglu, a.astype(F32), c.astype(F32))
        return (jnp.concatenate(vjp(dfv.astype(F32)), axis=1),)

    return _tilemap(name, fn, [(ac, DFF, 0), (ac, DFF, 1), (df, DFF, 0)], [], [(2 * DFF, 2 * DFF, BF16)], [],
                    rows=rows, tm=_tile(rows, 128))[0]


def _loss_bwd(name, x, w, target):
    rows = x.shape[0]

    def fn(xv, tv, wv):
        y, vjp = jax.vjp(_f_rms, xv, wv)
        e = y - tv
        dx, dw = vjp(e * (1.0 / D))
        per_row = jnp.mean(e * e, axis=-1, keepdims=True)
        return dx, dw, jnp.broadcast_to(0.5 * jnp.sum(per_row, axis=0, keepdims=True), (1, 128))

    return _tilemap(name, fn, [(x, D, 0), (target, D, 0)], [w], [(D, D, F32)], [(1, D), (1, 128)],
                    rows=rows, tm=_tile(rows, 256))


def _gelu(v):
    return 0.5 * v * (1.0 + lax.erf(v * (1.0 / math.sqrt(2.0))))


def _f_sgu_pre(su, sv, lw, lb):
    zv = _gelu(sv)
    mu = jnp.mean(zv, axis=-1, keepdims=True)
    d = zv - mu
    var = jnp.mean(d * d, axis=-1, keepdims=True)
    return _gelu(su), d * lax.rsqrt(var + EPS) * lw + lb


def _sgu_blocks(tm):
    return [slice(r * SLEN, (r + 1) * SLEN) for r in range(tm // SLEN)]


def _sgu_mix(wm, vnb, bexp, tm):
    rows = []
    for r in _sgu_blocks(tm):
        cols = [jnp.dot(wm[g], vnb[r, g * SCH:(g + 1) * SCH], preferred_element_type=F32) for g in range(GROUPS)]
        rows.append(jnp.concatenate(cols, axis=1) + bexp)
    return jnp.concatenate(rows, axis=0)


def _sgu_fwd(name, z, lw, lb, wm, bexp):
    rows = z.shape[0]
    tm = _tile(rows, 256)

    def fn(su, sv, lwv, lbv, wmv, bev):
        zu, vn = _f_sgu_pre(su.astype(F32), sv.astype(F32), lwv, lbv)
        return (zu * _sgu_mix(wmv, vn.astype(BF16), bev, tm),)

    return _tilemap(name, fn, [(z, D, Z_SU), (z, D, Z_SV)], [lw, lb, wm, bexp], [(D, D, BF16)], [], rows=rows, tm=tm)[0]


def _sgu_bwd(name, z, lw, lb, wm, wmt, bexp, maskf, db):
    rows = z.shape[0]
    tm = _tile(rows, 256)

    def fn(su, sv, dbv, lwv, lbv, wmv, wmtv, bev, mkv):
        (zu, vn), vjp = jax.vjp(_f_sgu_pre, su.astype(F32), sv.astype(F32), lwv, lbv)
        vnb = vn.astype(BF16)
        dbv = dbv.astype(F32)
        dmixed = dbv * zu
        dzu = dbv * _sgu_mix(wmv, vnb, bev, tm)
        dmb = dmixed.astype(BF16)
        dvn_rows, dws, dbf = [], [None] * GROUPS, None
        for r in _sgu_blocks(tm):
            cols = []
            for g in range(GROUPS):
                cs = slice(g * SCH, (g + 1) * SCH)
                cols.append(jnp.dot(wmtv[g], dmb[r, cs], preferred_element_type=F32))
                dw = lax.dot_general(dmb[r, cs], vnb[r, cs], (((1,), (1,)), ((), ())), preferred_element_type=F32)
                dws[g] = dw if dws[g] is None else dws[g] + dw
            dvn_rows.append(jnp.concatenate(cols, axis=1))
            dbf = dmixed[r] if dbf is None else dbf + dmixed[r]
        dsu, dsv, dlw, dlb = vjp((dzu, jnp.concatenate(dvn_rows, axis=0)))
        dws_all = jnp.concatenate([dw * mkv for dw in dws], axis=0)
        return dsu, dsv, dws_all, dbf, dlw, dlb

    return _tilemap(name, fn, [(z, D, Z_SU), (z, D, Z_SV), (db, D, 0)], [lw, lb, wm, wmt, bexp, maskf],
                    [(D, D, BF16), (D, D, BF16)], [(GROUPS * SLEN, SLEN), (SLEN, D), (1, D), (1, D)], rows=rows, tm=tm)


def _group_sums(name, dbf):
    def body(e_ref, x_ref, o_ref):
        o_ref[...] = lax.dot_general(e_ref[...], x_ref[...], (((1,), (1,)), ((), ())),
                                     precision=lax.Precision.HIGHEST, preferred_element_type=F32)

    ind = (jnp.arange(D)[None, :] // SCH == jnp.arange(GROUPS)[:, None]).astype(F32)
    return pl.pallas_call(body, name=name, out_shape=jax.ShapeDtypeStruct((GROUPS, SLEN), F32))(ind, dbf)


def _ret_consts(rows):
    hh = jnp.arange(HEADS, dtype=F32)
    log_g = jnp.log1p(-(2.0 ** (-5.0 - hh)))
    idx = jnp.arange(RL, dtype=F32)
    ci = jnp.arange(RL) // CHUNK
    mask = ci[None, :] <= ci[:, None]
    dm = jnp.where(mask[None], jnp.exp(log_g[:, None, None] * jnp.abs(idx[:, None] - idx[None, :])), 0.0)
    qd = jnp.broadcast_to(jnp.exp(log_g[:, None] * (idx[None, :] + 1.0))[:, :, None], (HEADS, RL, DK))
    kd = jnp.broadcast_to(jnp.exp(log_g[:, None] * (RL - 1.0 - idx[None, :]))[:, :, None], (HEADS, RL, DK))
    cd = jnp.broadcast_to(jnp.exp(log_g * RL)[:, None, None], (HEADS, DK, DV))
    half = DK // 2
    inv = ROPE_BASE ** (-jnp.arange(half, dtype=F32) / half)
    ang = jnp.arange(rows, dtype=jnp.int32).astype(F32)[:, None] * inv[None, :]
    cos, sin = jnp.cos(ang), jnp.sin(ang)
    return dict(dm=dm.astype(F32), qd=qd, kd=kd, cd=cd, cos=jnp.concatenate([cos, cos], axis=1),
                sin=jnp.concatenate([-sin, sin], axis=1))


def _rope(x, cos, sin_signed):
    return x * cos + pltpu.roll(x, DK // 2, 1) * sin_signed


def _rope_t(d, cos, sin_signed):
    return d * cos + pltpu.roll(d * sin_signed, DK // 2, 1)


def _dot(a, b, ca, cb):
    return lax.dot_general(a, b, (((ca,), (cb,)), ((), ())), preferred_element_type=F32)


def _ret_specs(nb, rev):
    blk = (lambda n: nb - 1 - n) if rev else (lambda n: n)
    return dict(
        q=pl.BlockSpec((RL, DK), lambda h, n: (blk(n), h)),
        k=pl.BlockSpec((RL, DK), lambda h, n: (blk(n), HEADS + h)),
        v=pl.BlockSpec((RL, DV), lambda h, n: (blk(n), HEADS + h)),
        o=pl.BlockSpec((RL, DV), lambda h, n: (blk(n), h)),
        rope=pl.BlockSpec((RL, DK), lambda h, n: (blk(n), 0)),
        dm=pl.BlockSpec((None, RL, RL), lambda h, n: (h, 0, 0)),
        qd=pl.BlockSpec((None, RL, DK), lambda h, n: (h, 0, 0)),
        cd=pl.BlockSpec((None, DK, DV), lambda h, n: (h, 0, 0)),
        st=pl.BlockSpec((None, None, DK, DV), lambda h, n: (blk(n), h, 0, 0)),
    )


def _ret_fwd(name, z, rc):
    rows = z.shape[0]
    nb = rows // RL
    sp = _ret_specs(nb, False)

    def body(q_ref, k_ref, v_ref, cos_ref, sin_ref, dm_ref, qd_ref, kd_ref, cd_ref, o_ref, st_ref, state):
        @pl.when(pl.program_id(1) == 0)
        def _():
            state[...] = jnp.zeros_like(state)

        cos, sin = cos_ref[...], sin_ref[...]
        qf = _rope(q_ref[...].astype(F32), cos, sin) * (DK ** -0.5)
        kf = _rope(k_ref[...].astype(F32), cos, sin)
        vb = v_ref[...]
        s0 = state[...]
        s0b = s0.astype(BF16)
        st_ref[...] = s0b
        sc = _dot(qf.astype(BF16), kf.astype(BF16), 1, 1) * dm_ref[...]
        o = _dot(sc.astype(BF16), vb, 1, 0) + _dot((qf * qd_ref[...]).astype(BF16), s0b, 1, 0)
        o_ref[...] = o.astype(o_ref.dtype)
        state[...] = cd_ref[...] * s0 + _dot((kf * kd_ref[...]).astype(BF16), vb, 0, 0)

    return pl.pallas_call(
        body, name=name, grid=(HEADS, nb),
        in_specs=[sp["q"], sp["k"], sp["v"], sp["rope"], sp["rope"], sp["dm"], sp["qd"], sp["qd"], sp["cd"]],
        out_specs=[sp["o"], sp["st"]],
        out_shape=[jax.ShapeDtypeStruct((rows, HEADS * DV), BF16), jax.ShapeDtypeStruct((nb, HEADS, DK, DV), BF16)],
        scratch_shapes=[pltpu.VMEM((DK, DV), F32)],
        compiler_params=_cparams(("parallel", "arbitrary"), 32),
    )(z, z, z, rc["cos"], rc["sin"], rc["dm"], rc["qd"], rc["kd"], rc["cd"])


def _ret_bwd(name, z, rc, states, do):
    rows = z.shape[0]
    nb = rows // RL
    sp = _ret_specs(nb, True)

    def body(q_ref, k_ref, v_ref, cos_ref, sin_ref, dm_ref, qd_ref, kd_ref, cd_ref, st_ref, do_ref,
             dq_ref, dk_ref, dv_ref, dstate):
        @pl.when(pl.program_id(1) == 0)
        def _():
            dstate[...] = jnp.zeros_like(dstate)

        cos, sin, dm, qd, kd = cos_ref[...], sin_ref[...], dm_ref[...], qd_ref[...], kd_ref[...]
        qf = _rope(q_ref[...].astype(F32), cos, sin) * (DK ** -0.5)
        kf = _rope(k_ref[...].astype(F32), cos, sin)
        qb, kb, vb, dob, s0b = qf.astype(BF16), kf.astype(BF16), v_ref[...], do_ref[...], st_ref[...]
        qsb, ksb = (qf * qd).astype(BF16), (kf * kd).astype(BF16)
        ds1 = dstate[...]
        ds1b = ds1.astype(BF16)
        scb = (_dot(qb, kb, 1, 1) * dm).astype(BF16)
        dscb = (_dot(dob, vb, 1, 1) * dm).astype(BF16)
        dq = _dot(dscb, kb, 1, 0) + _dot(dob, s0b, 1, 1) * qd
        dk = _dot(dscb, qb, 0, 0) + _dot(vb, ds1b, 1, 1) * kd
        dv = _dot(scb, dob, 0, 0) + _dot(ksb, ds1b, 1, 0)
        dq_ref[...] = _rope_t(dq * (DK ** -0.5), cos, sin).astype(dq_ref.dtype)
        dk_ref[...] = _rope_t(dk, cos, sin).astype(dk_ref.dtype)
        dv_ref[...] = dv.astype(dv_ref.dtype)
        dstate[...] = cd_ref[...] * ds1 + _dot(qsb, dob, 0, 0)

    dqk = pl.BlockSpec((RL, DK), lambda h, n: (nb - 1 - n, h))
    return pl.pallas_call(
        body, name=name, grid=(HEADS, nb),
        in_specs=[sp["q"], sp["k"], sp["v"], sp["rope"], sp["rope"], sp["dm"], sp["qd"], sp["qd"], sp["cd"], sp["st"], sp["o"]],
        out_specs=[dqk, dqk, sp["o"]],
        out_shape=[jax.ShapeDtypeStruct((rows, HEADS * DK), BF16)] * 2 + [jax.ShapeDtypeStruct((rows, HEADS * DV), BF16)],
        scratch_shapes=[pltpu.VMEM((DK, DV), F32)],
        compiler_params=_cparams(("parallel", "arbitrary"), 32),
    )(z, z, z, rc["cos"], rc["sin"], rc["dm"], rc["qd"], rc["kd"], rc["cd"], states, do)


def _layer_fwd(l, x, p, rc):
    n = f"l{l}_"
    h = _rms_fwd(n + "rms1", x, p["norm_mix_w"])
    z = _mm_nn_sh(n + "in_proj", h, p["w_in"], out_dtype=BF16)
    o, states = _ret_fwd(n + "ret", z, rc)
    a_in = _mixa_fwd(n + "mixa", z, o, p["ret_gn_w"])
    b_in = _sgu_fwd(n + "sgu", z, p["sgu_ln_w"], p["sgu_ln_b"], p["wm"], p["bexp"])
    br_a = _mm_nn(n + "ret_proj", a_in, p["ret_proj"], out_dtype=BF16)
    br_b = _mm_nn(n + "sgu_proj", b_in, p["sgu_proj"], out_dtype=BF16)
    merged = _merge_fwd(n + "merge", z, br_a, br_b)
    x1 = _mm_nn(n + "out_proj", merged, p["w_out"], out_dtype=F32, res=x)
    h2 = _rms_fwd(n + "rms2", x1, p["norm_ffn_w"])
    ac = _mm_nn_sh(n + "ffn_in", h2, p["w_ffn_in"], out_dtype=BF16)
    f = _swiglu_fwd(n + "swiglu", ac)
    x2 = _mm_nn(n + "ffn_out", f, p["w_ffn_out"], out_dtype=F32, res=x1, tk=2 * SH_FF)
    saved = dict(x=x, h=h, z=z, o=o, states=states, a_in=a_in, b_in=b_in, br_a=br_a, br_b=br_b, merged=merged,
                 x1=x1, h2=h2, ac=ac, f=f)
    return x2, saved


def _layer_bwd_ffn(l, dx2, p, s, after):
    n = f"l{l}b_"
    df = _mm_nt(n + "d_f", dx2, p["w_ffn_out"], out_dtype=BF16, tn=SH_FF, after=after)
    g_ffn_out = _mm_tn(n + "g_ffn_out", s["f"], dx2, out_dtype=BF16, tm=SH_FF)
    dac = _swiglu_bwd(n + "swiglu", s["ac"], df)
    dh2 = _mm_nt_sh(n + "d_h2", dac, p["w_ffn_in"], out_dtype=BF16)
    g_ffn_in = _mm_tn_sh(n + "g_ffn_in", s["h2"], dac, SH_FF, out_dtype=BF16)
    dx1, g_nffn = _rms_bwd(n + "rms2", s["x1"], p["norm_ffn_w"], dh2, dx2)
    return dx1, dict(w_ffn_in=g_ffn_in, w_ffn_out=g_ffn_out.reshape(NDEV, -1, D)), dict(norm_ffn_w=g_nffn)


def _layer_bwd_mix(l, dx1, p, s, rc, after, send):
    n = f"l{l}b_"
    dmerged = _mm_nt(n + "d_merged", dx1, p["w_out"], out_dtype=BF16, after=after)
    g_out = _mm_tn(n + "g_out", s["merged"], dx1, out_dtype=BF16)
    dga, dgb, dbr_a, dbr_b = _merge_bwd(n + "merge", s["z"], s["br_a"], s["br_b"], dmerged)
    da_in = _mm_nt(n + "d_a_in", dbr_a, p["ret_proj"], out_dtype=BF16)
    g_ret_proj = _mm_tn(n + "g_ret_proj", s["a_in"], dbr_a, out_dtype=BF16, tk=2048)
    db_in = _mm_nt(n + "d_b_in", dbr_b, p["sgu_proj"], out_dtype=BF16)
    g_sgu_proj = _mm_tn(n + "g_sgu_proj", s["b_in"], dbr_b, out_dtype=BF16, tk=2048)
    dg, do, g_gn = _mixa_bwd(n + "mixa", s["z"], s["o"], p["ret_gn_w"], da_in)
    dq, dk, dv = _ret_bwd(n + "ret", s["z"], rc, s["states"], do)
    dsu, dsv, g_ws, dbf, g_lw, g_lb = _sgu_bwd(n + "sgu", s["z"], p["sgu_ln_w"], p["sgu_ln_b"], p["wm"], p["wmt"],
                                                 p["bexp"], p["maskf"], db_in)
    g_bs = _group_sums(n + "g_bs", dbf)
    dz = jnp.concatenate([dq, dk, dv, dg, dsu, dsv, dga, dgb], axis=1)
    g_in = _mm_tn_sh(n + "g_in", s["h"], dz, SH_IN, out_dtype=BF16)
    token = send(dict(w_in=g_in, ret_proj=g_ret_proj.reshape(NDEV, -1, D), sgu_proj=g_sgu_proj.reshape(NDEV, -1, D),
                      w_out=g_out.reshape(NDEV, -1, D)))
    dh = _mm_nt_sh(n + "d_h", dz, p["w_in"], out_dtype=BF16, after=token)
    dx, g_nmix = _rms_bwd(n + "rms1", s["x"], p["norm_mix_w"], dh, dx1)
    small = dict(norm_mix_w=g_nmix, ret_gn_w=g_gn, sgu_ln_w=g_lw, sgu_ln_b=g_lb, sgu_w_s=g_ws, sgu_b_s=g_bs)
    return dx, small


def _slot(ref, p):
    return ref.at[4 * p[0] + 2 * p[1] + p[2]]


def _allgather(name, shards):
    n = len(shards)

    def body(*refs):
        ins, outs = refs[:n], refs[n:2 * n]
        send_sems, recv_sems, local_sems = refs[2 * n:]
        x, y, c = lax.axis_index("x"), lax.axis_index("y"), lax.axis_index("c")
        me, sibling = (x, y, c), (x, y, 1 - c)
        chips = [(1 - x, y), (x, 1 - y), (1 - x, 1 - y)]

        def copy(a, k, block, to, src=None):
            return pltpu.make_async_remote_copy(
                src_ref=_slot(outs[a], block) if src is None else src, dst_ref=_slot(outs[a], block),
                send_sem=send_sems.at[a, k], recv_sem=recv_sems.at[a, k], device_id=to, device_id_type=MESH_T)

        mine = [pltpu.make_async_copy(ins[a], _slot(outs[a], me), local_sems.at[a]) for a in range(n)]
        for cp in mine:
            cp.start()
        first = []
        for a in range(n):
            first.append(copy(a, 0, me, sibling, src=ins[a]))
            first += [copy(a, 1 + j, me, (*chip, c), src=ins[a]) for j, chip in enumerate(chips)]
        for cp in first:
            cp.start()
        passed = []
        for j, chip in enumerate(chips):
            for a in range(n):
                copy(a, 1 + j, (*chip, c), me).wait_recv()
                cp = copy(a, 4 + j, (*chip, c), sibling)
                cp.start()
                passed.append(cp)
        for a in range(n):
            copy(a, 0, sibling, me).wait_recv()
            for j, chip in enumerate(chips):
                copy(a, 4 + j, (*chip, 1 - c), me).wait_recv()
        for cp in first + passed:
            cp.wait_send()
        for cp in mine:
            cp.wait()

    return pl.pallas_call(
        body, name=name, in_specs=[ANY] * n, out_specs=[ANY] * n,
        out_shape=[jax.ShapeDtypeStruct((NDEV,) + s.shape, s.dtype) for s in shards],
        scratch_shapes=[pltpu.SemaphoreType.DMA((n, 7)), pltpu.SemaphoreType.DMA((n, 7)), pltpu.SemaphoreType.DMA((n,))],
    )(*shards)


def _peers(x, y, c):
    return [((1 - x) if fx else x, (1 - y) if fy else y, (1 - c) if fc else c)
            for fx in (0, 1) for fy in (0, 1) for fc in (0, 1) if fx + fy + fc]


def _exchange_copy(whole, src_ref, land_ref, send_sems, recv_sems, a, k, peer, me, landed):
    return pltpu.make_async_remote_copy(
        src_ref=src_ref if whole else _slot(src_ref, peer), dst_ref=_slot(land_ref, peer if landed else me),
        send_sem=send_sems.at[7 * a + k], recv_sem=recv_sems.at[7 * a + k], device_id=peer, device_id_type=MESH_T)


def _exchange_start(name, srcs, lands, whole, after):
    n = len(srcs)

    def body(*refs):
        src_refs, land_refs = refs[:n], refs[n:2 * n]
        send_sems, recv_sems = refs[2 * n + 1], refs[2 * n + 2]
        token = refs[-1]
        x, y, c = lax.axis_index("x"), lax.axis_index("y"), lax.axis_index("c")
        for a in range(n):
            for k, peer in enumerate(_peers(x, y, c)):
                _exchange_copy(whole, src_refs[a], land_refs[a], send_sems, recv_sems, a, k, peer, (x, y, c), False).start()
        token[...] = jnp.zeros_like(token)

    hbm = lambda t: pltpu.HBM(t.shape, t.dtype)
    out = pl.pallas_call(
        body, name=name, in_specs=[HBM_SPEC] * (2 * n) + [ANY],
        out_specs=(SEM_SPEC, SEM_SPEC, *[HBM_SPEC] * (2 * n), pl.BlockSpec(memory_space=pltpu.VMEM)),
        out_shape=(pltpu.SemaphoreType.DMA((7 * n,)), pltpu.SemaphoreType.DMA((7 * n,)), *[hbm(t) for t in srcs],
                   *[hbm(t) for t in lands], jax.ShapeDtypeStruct((8, 128), F32)),
        input_output_aliases={i: 2 + i for i in range(2 * n)},
        compiler_params=pltpu.CompilerParams(has_side_effects=pltpu.SideEffectType.DATAFLOW_SIDE_EFFECTING),
    )(*[pltpu.with_memory_space_constraint(t, pltpu.HBM) for t in (*srcs, *lands)], after)
    return dict(send=out[0], recv=out[1], srcs=out[2:2 + n], lands=out[2 + n:2 + 2 * n], token=out[-1])


def _exchange_wait(name, started, whole, after):
    n = len(started["srcs"])

    def body(*refs):
        src_refs, land_refs = refs[:n], refs[n:2 * n]
        send_sems, recv_sems = refs[2 * n], refs[2 * n + 1]
        x, y, c = lax.axis_index("x"), lax.axis_index("y"), lax.axis_index("c")
        for a in range(n):
            for k, peer in enumerate(_peers(x, y, c)):
                cp = _exchange_copy(whole, src_refs[a], land_refs[a], send_sems, recv_sems, a, k, peer, (x, y, c), True)
                cp.wait_send()
                cp.wait_recv()

    hbm = lambda t: pltpu.HBM(t.shape, t.dtype)
    out = pl.pallas_call(
        body, name=name, in_specs=[HBM_SPEC] * (2 * n) + [SEM_SPEC, SEM_SPEC, ANY], out_specs=[HBM_SPEC] * (2 * n),
        out_shape=[hbm(t) for t in (*started["srcs"], *started["lands"])],
        input_output_aliases={i: i for i in range(2 * n)},
        compiler_params=pltpu.CompilerParams(has_side_effects=pltpu.SideEffectType.DATAFLOW_SIDE_EFFECTING),
    )(*started["srcs"], *started["lands"], started["send"], started["recv"], after)
    return out[n:]


def _adamw(w, g, m, v):
    m = ADAM_B1 * m + (1.0 - ADAM_B1) * g
    v = ADAM_B2 * v + (1.0 - ADAM_B2) * (g * g)
    m_hat = m / (1.0 - ADAM_B1 ** ADAM_STEP)
    v_hat = v / (1.0 - ADAM_B2 ** ADAM_STEP)
    return -ADAM_LR * (m_hat / (jnp.sqrt(v_hat) + ADAM_EPS) + ADAM_WD * w), m, v


def _sum_and_adamw(name, recv, w, m, v, layer, prev):
    _, r, c = recv.shape
    tr = max(t for t in range(16, r + 1, 16) if r % t == 0 and t * c <= ADAM_TILE_ELEMS)

    def body(recv_ref, w_ref, m_ref, v_ref, *rest):
        g_ref, d_ref, m_out, v_out = rest[4:]
        g = recv_ref[0].astype(F32)
        for s in range(1, NDEV):
            g = g + recv_ref[s].astype(F32)
        delta, m_new, v_new = _adamw(w_ref[...], g, m_ref[...], v_ref[...])
        g_ref[...] = g
        d_ref[...] = delta
        m_out[...] = m_new
        v_out[...] = v_new

    lay = pl.BlockSpec((None, tr, c), lambda i: (layer, i, 0))
    return pl.pallas_call(
        body, name=name, grid=(r // tr,),
        in_specs=[pl.BlockSpec((NDEV, tr, c), lambda i: (0, i, 0)), lay, lay, lay] + [ANY] * 4,
        out_specs=[lay] * 4, out_shape=[jax.ShapeDtypeStruct(w.shape, F32)] * 4,
        input_output_aliases={4: 0, 5: 1, 6: 2, 7: 3},
        compiler_params=_cparams(("arbitrary",), 48),
    )(recv, w, m, v, *prev)


def _small_sum_and_adamw(name, recv, w, m, v):
    r = w.shape[0]

    def body(recv_ref, w_ref, m_ref, v_ref, g_ref, d_ref, m_out, v_out):
        g = recv_ref[0]
        for s in range(1, NDEV):
            g = g + recv_ref[s]
        delta, m_new, v_new = _adamw(w_ref[...], g, m_ref[...], v_ref[...])
        g_ref[...] = g
        d_ref[...] = delta
        m_out[...] = m_new
        v_out[...] = v_new

    return pl.pallas_call(body, name=name, out_shape=[jax.ShapeDtypeStruct((r, 128), F32)] * 4,
                          compiler_params=_cparams(None, 48))(recv, w, m, v)


BIG = ("w_in", "ret_proj", "sgu_proj", "w_out", "w_ffn_in", "w_ffn_out")
SMALL = ("norm_mix_w", "ret_gn_w", "sgu_ln_w", "sgu_ln_b", "sgu_w_s", "sgu_b_s", "norm_ffn_w", "final_norm_w")
WEIGHTS = ("norm_mix_w", "w_in", "ret_gn_w", "ret_proj", "sgu_ln_w", "sgu_ln_b", "sgu_w_s", "sgu_b_s", "sgu_proj",
           "w_out", "norm_ffn_w", "w_ffn_in", "w_ffn_out", "final_norm_w")


def _pack_small(d):
    return jnp.concatenate([d[k].reshape(-1, 128) for k in SMALL], axis=0)


def _unpack_small(packed, like):
    out, r0 = {}, 0
    for k in SMALL:
        r = like[k].size // 128
        out[k] = packed[r0:r0 + r].reshape(like[k].shape)
        r0 += r
    return out


def _layer_params(l, w, gathered, mask):
    p = dict(zip(BIG, gathered))
    for k in ("ret_proj", "sgu_proj", "w_out", "w_ffn_out"):
        p[k] = p[k].reshape(-1, D)
    for k in ("norm_mix_w", "ret_gn_w", "sgu_ln_w", "sgu_ln_b", "norm_ffn_w"):
        p[k] = w[k][l][None, :]
    wm = jnp.where(mask[None], w["sgu_w_s"][l], 0.0)
    p["wm"], p["wmt"] = wm.astype(BF16), jnp.swapaxes(wm, 1, 2).astype(BF16)
    p["bexp"] = jnp.repeat(w["sgu_b_s"][l].T, SCH, axis=1)
    p["maskf"] = mask.astype(F32)
    return p


def _step(w, m, v, x, target):
    depth = w["w_in"].shape[0]
    rc = _ret_consts(x.shape[0])
    pos = jnp.arange(SLEN)
    mask = (pos[None, :] // CHUNK) <= (pos[:, None] // CHUNK)
    shards = [[w[k][l].astype(BF16) for k in BIG] for l in range(depth)]
    me = 4 * lax.axis_index("x") + 2 * lax.axis_index("y") + lax.axis_index("c")

    gathered = _allgather("gather_l0", shards[0])
    saved, layers = [], []
    for l in range(depth):
        p = _layer_params(l, w, gathered, mask)
        if l + 1 < depth:
            lands = [lax.dynamic_update_index_in_dim(lax.empty((NDEV,) + t.shape, t.dtype), t, me, 0) for t in shards[l + 1]]
            started = _exchange_start(f"gather_start_l{l + 1}", shards[l + 1], lands, True, gathered[0])
            p["norm_mix_w"] = p["norm_mix_w"] + started["token"][0, 0]
        x, s = _layer_fwd(l, x, p, rc)
        if l + 1 < depth:
            gathered = _exchange_wait(f"gather_wait_l{l + 1}", started, True, x)
        saved.append(s)
        layers.append(p)

    dx, g_final, loss_row = _loss_bwd("loss", x, w["final_norm_w"][None, :], target)
    loss = lax.psum(loss_row[0, 0], ("x", "y", "c"))

    res = {k: [lax.empty(w[k].shape, F32) for _ in range(4)] for k in BIG}
    small = {k: [None] * depth for k in SMALL if k != "final_norm_w"}

    def scatter_start(tag, big, after):
        keys = [k for k in BIG if k in big]
        parts = [big[k] for k in keys]
        lands = [lax.dynamic_update_index_in_dim(lax.empty(t.shape, t.dtype), lax.dynamic_index_in_dim(t, me, 0), me, 0)
                 for t in parts]
        return keys, _exchange_start(f"scatter_start_{tag}", parts, lands, False, after)

    def finish(l, pending, after):
        for tag, (keys, started) in pending:
            recv = _exchange_wait(f"scatter_wait_{tag}", started, False, after)
            for k, rv in zip(keys, recv):
                res[k] = _sum_and_adamw(f"adamw_{k}_l{l}", rv, w[k], m[k], v[k], l, res[k])

    pending, token = None, loss_row
    for l in reversed(range(depth)):
        dx1, big_ffn, sm_ffn = _layer_bwd_ffn(l, dx, layers[l], saved[l], token)
        now = [(f"ffn_l{l}", scatter_start(f"ffn_l{l}", big_ffn, dx1))]

        def send(big_mix, l=l, now=now, dx1=dx1):
            now.append((f"mix_l{l}", scatter_start(f"mix_l{l}", big_mix, dx1)))
            return now[1][1][1]["token"]

        dx, sm_mix = _layer_bwd_mix(l, dx1, layers[l], saved[l], rc, now[0][1][1]["token"], send)
        token = now[1][1][1]["token"]
        if pending is not None:
            finish(l + 1, pending, dx)
        pending = now
        for k, g in {**sm_ffn, **sm_mix}.items():
            small[k][l] = g
    finish(0, pending, dx)

    g_small = {k: jnp.stack(small[k]).reshape(w[k].shape) for k in small}
    g_small["final_norm_w"] = g_final.reshape(w["final_norm_w"].shape)
    recv = _allgather("gather_small", [_pack_small(g_small)])[0]
    sres = _small_sum_and_adamw("adamw_small", recv, _pack_small(w), _pack_small(m), _pack_small(v))
    sres = [_unpack_small(a, w) for a in sres]

    outs = []
    for i in range(4):
        outs.append([sres[i][k] if k in SMALL else res[k][i] for k in WEIGHTS])
    return loss, dx, outs


def kernel(x, norm_mix_w, w_in, ret_gn_w, ret_proj, sgu_ln_w, sgu_ln_b, sgu_w_s, sgu_b_s, sgu_proj, w_out, norm_ffn_w, w_ffn_in, w_ffn_out, final_norm_w, loss_target, m_norm_mix_w, m_w_in, m_ret_gn_w, m_ret_proj, m_sgu_ln_w, m_sgu_ln_b, m_sgu_w_s, m_sgu_b_s, m_sgu_proj, m_w_out, m_norm_ffn_w, m_w_ffn_in, m_w_ffn_out, m_final_norm_w, v_norm_mix_w, v_w_in, v_ret_gn_w, v_ret_proj, v_sgu_ln_w, v_sgu_ln_b, v_sgu_w_s, v_sgu_b_s, v_sgu_proj, v_w_out, v_norm_ffn_w, v_w_ffn_in, v_w_ffn_out, v_final_norm_w):
    w = dict(zip(WEIGHTS, (norm_mix_w, w_in, ret_gn_w, ret_proj, sgu_ln_w, sgu_ln_b, sgu_w_s, sgu_b_s, sgu_proj, w_out,
                           norm_ffn_w, w_ffn_in, w_ffn_out, final_norm_w)))
    m = dict(zip(WEIGHTS, (m_norm_mix_w, m_w_in, m_ret_gn_w, m_ret_proj, m_sgu_ln_w, m_sgu_ln_b, m_sgu_w_s, m_sgu_b_s,
                           m_sgu_proj, m_w_out, m_norm_ffn_w, m_w_ffn_in, m_w_ffn_out, m_final_norm_w)))
    v = dict(zip(WEIGHTS, (v_norm_mix_w, v_w_in, v_ret_gn_w, v_ret_proj, v_sgu_ln_w, v_sgu_ln_b, v_sgu_w_s, v_sgu_b_s,
                           v_sgu_proj, v_w_out, v_norm_ffn_w, v_w_ffn_in, v_w_ffn_out, v_final_norm_w)))
    loss, dx, (grads, deltas, new_m, new_v) = _step(w, m, v, x.reshape(x.shape[1:]), loss_target.reshape(x.shape[1:]))
    return (loss, dx[None], *grads, *deltas, *new_m, *new_v)
```

```python
import functools
import math

import jax
import jax.numpy as jnp
from jax import lax
from jax.experimental import pallas as pl
from jax.experimental.pallas import tpu as pltpu

F32, BF16 = jnp.float32, jnp.bfloat16

D = 2048
HEADS, DK, DV = 8, 128, 256
GROUPS, SLEN = 8, 128
SCH = D // GROUPS
DFF = 5632
CHUNK = 64
IN_COLS = 14336
NDEV = 8
SH_IN = IN_COLS // NDEV
SH_FF = 2 * DFF // NDEV
EPS = 1e-6
ROPE_BASE = 10000.0
RL = 256
Z_G, Z_SU, Z_SV, Z_GA, Z_GB = 2, 3, 4, 5, 6

ADAM_TILE_ELEMS = 256 * 1024
ADAM_LR, ADAM_B1, ADAM_B2, ADAM_EPS, ADAM_WD, ADAM_STEP = 0.001, 0.9, 0.999, 1e-08, 0.01, 10

VMEM_BYTES_V7X = 64 << 20
MESH_T = pl.DeviceIdType.MESH
ANY = pl.BlockSpec(memory_space=pl.ANY)
HBM_SPEC = pl.BlockSpec(memory_space=pltpu.HBM)
SEM_SPEC = pl.BlockSpec(memory_space=pltpu.SEMAPHORE)


def _cparams(sem, vmem_mb):
    assert (vmem_mb << 20) < VMEM_BYTES_V7X
    return pltpu.CompilerParams(dimension_semantics=sem, vmem_limit_bytes=vmem_mb << 20)


def _mm(name, a, b, *, grid, a_spec, b_spec, o_spec, o_block, out_shape, out_dtype, contract, res=None, after=None,
        split=1, vmem_mb=48):
    nk = grid[2]
    dn = (contract, ((), ()))
    has_res = res is not None

    def body(*refs):
        a_ref, b_ref = refs[:2]
        r_ref = refs[2] if has_res else None
        o_ref, acc_ref = refs[-2:]
        if split == 1:
            part = lax.dot_general(a_ref[...].astype(BF16), b_ref[...].astype(BF16), dn, preferred_element_type=F32)
        else:
            kp = a_ref.shape[1] // split
            part = sum(lax.dot_general(a_ref[:, s * kp:(s + 1) * kp].astype(BF16), b_ref[s].astype(BF16), dn,
                                       preferred_element_type=F32) for s in range(split))

        def finish(val):
            if has_res:
                val = val + r_ref[...].astype(F32)
            o_ref[...] = val.astype(o_ref.dtype)

        if nk == 1:
            finish(part)
        else:
            k = pl.program_id(2)

            @pl.when(k == 0)
            def _():
                acc_ref[...] = part

            @pl.when(k > 0)
            def _():
                acc_ref[...] += part

            @pl.when(k == nk - 1)
            def _():
                finish(acc_ref[...])

    ins = [a, b] + ([res] if has_res else []) + ([after] if after is not None else [])
    in_specs = [a_spec, b_spec] + ([o_spec] if has_res else []) + ([ANY] if after is not None else [])
    return pl.pallas_call(
        body, name=name, grid=grid, in_specs=in_specs, out_specs=o_spec,
        out_shape=jax.ShapeDtypeStruct(out_shape, out_dtype),
        scratch_shapes=[pltpu.VMEM(o_block if nk > 1 else (8, 128), F32)],
        compiler_params=_cparams(("parallel", "parallel", "arbitrary"), vmem_mb),
    )(*ins)


def _tile(n, want):
    t = min(n, want)
    assert n % t == 0, (n, want)
    return t


def _mm_nn(name, a, b, *, out_dtype, tm=1024, tn=1024, tk=None, res=None):
    (m, k), n = a.shape, b.shape[1]
    tm, tn, tk = _tile(m, tm), _tile(n, tn), k if tk is None else tk
    return _mm(name, a, b, grid=(m // tm, n // tn, k // tk),
               a_spec=pl.BlockSpec((tm, tk), lambda i, j, kk: (i, kk)),
               b_spec=pl.BlockSpec((tk, tn), lambda i, j, kk: (kk, j)),
               o_spec=pl.BlockSpec((tm, tn), lambda i, j, kk: (i, j)), o_block=(tm, tn),
               out_shape=(m, n), out_dtype=out_dtype, contract=((1,), (0,)), res=res)


def _mm_nt(name, a, b, *, out_dtype, tm=1024, tn=1024, tk=None, after=None):
    (m, k), n = a.shape, b.shape[0]
    tm, tn, tk = _tile(m, tm), _tile(n, tn), k if tk is None else tk
    return _mm(name, a, b, grid=(m // tm, n // tn, k // tk),
               a_spec=pl.BlockSpec((tm, tk), lambda i, j, kk: (i, kk)),
               b_spec=pl.BlockSpec((tn, tk), lambda i, j, kk: (j, kk)),
               o_spec=pl.BlockSpec((tm, tn), lambda i, j, kk: (i, j)), o_block=(tm, tn),
               out_shape=(m, n), out_dtype=out_dtype, contract=((1,), (1,)), after=after)


def _mm_tn(name, a, b, *, out_dtype, tm=1024, tn=1024, tk=1024):
    (k, m), n = a.shape, b.shape[1]
    tm, tn, tk = _tile(m, tm), _tile(n, tn), _tile(k, tk)
    return _mm(name, a, b, grid=(m // tm, n // tn, k // tk),
               a_spec=pl.BlockSpec((tk, tm), lambda i, j, kk: (kk, i)),
               b_spec=pl.BlockSpec((tk, tn), lambda i, j, kk: (kk, j)),
               o_spec=pl.BlockSpec((tm, tn), lambda i, j, kk: (i, j)), o_block=(tm, tn),
               out_shape=(m, n), out_dtype=out_dtype, contract=((0,), (0,)))


def _mm_nn_sh(name, a, bsh, *, out_dtype, tm=1024):
    (m, k), ns = a.shape, bsh.shape[2]
    tm = _tile(m, tm)
    return _mm(name, a, bsh, grid=(m // tm, NDEV, 1),
               a_spec=pl.BlockSpec((tm, k), lambda i, j, kk: (i, 0)),
               b_spec=pl.BlockSpec((None, k, ns), lambda i, j, kk: (j, 0, 0)),
               o_spec=pl.BlockSpec((tm, ns), lambda i, j, kk: (i, j)), o_block=(tm, ns),
               out_shape=(m, NDEV * ns), out_dtype=out_dtype, contract=((1,), (0,)), vmem_mb=56)


def _mm_nt_sh(name, a, bsh, *, out_dtype, tm=1024, tn=1024, after=None):
    m, (_, n, ns) = a.shape[0], bsh.shape
    tm, tn = _tile(m, tm), _tile(n, tn)
    return _mm(name, a, bsh, grid=(m // tm, n // tn, NDEV // 2),
               a_spec=pl.BlockSpec((tm, 2 * ns), lambda i, j, kk: (i, kk)),
               b_spec=pl.BlockSpec((2, tn, ns), lambda i, j, kk: (kk, j, 0)),
               o_spec=pl.BlockSpec((tm, tn), lambda i, j, kk: (i, j)), o_block=(tm, tn),
               out_shape=(m, n), out_dtype=out_dtype, contract=((1,), (1,)), after=after, split=2)


def _mm_tn_sh(name, a, b, ns, *, out_dtype, tm=1024, tk=2048):
    k, m = a.shape
    tm, tk = _tile(m, tm), _tile(k, tk)
    return _mm(name, a, b, grid=(m // tm, NDEV, k // tk),
               a_spec=pl.BlockSpec((tk, tm), lambda i, j, kk: (kk, i)),
               b_spec=pl.BlockSpec((tk, ns), lambda i, j, kk: (kk, j)),
               o_spec=pl.BlockSpec((None, tm, ns), lambda i, j, kk: (j, i, 0)), o_block=(tm, ns),
               out_shape=(NDEV, m, ns), out_dtype=out_dtype, contract=((0,), (0,)), vmem_mb=56)


def _tilemap(name, fn, tiled_ins, full_ins, tiled_outs, acc_shapes, *, rows, tm, ncol=1, vmem_mb=48, slab=None):
    n_ti, n_fi, n_to, n_ao = len(tiled_ins), len(full_ins), len(tiled_outs), len(acc_shapes)
    assert ncol == 1 or (n_ao == 0 and slab is None)
    if slab is not None:
        n_to += 1

    def body(*refs):
        t_in, f_in = refs[:n_ti], refs[n_ti:n_ti + n_fi]
        n_in = n_ti + n_fi + (slab is not None and slab[0] is not None)
        t_out, a_out = refs[n_in:n_in + n_to], refs[n_in + n_to:]
        outs = fn(*[r[...] for r in t_in], *[r[...] for r in f_in])
        for r, v in zip(t_out, outs[:n_to]):
            r[...] = v.astype(r.dtype)
        if n_ao:
            i = pl.program_id(1)

            @pl.when(i == 0)
            def _():
                for r, v in zip(a_out, outs[n_to:]):
                    r[...] = v

            @pl.when(i > 0)
            def _():
                for r, v in zip(a_out, outs[n_to:]):
                    r[...] += v

    in_specs = [pl.BlockSpec((tm, bw), lambda j, i, base=base: (i, base + j)) for (_, bw, base) in tiled_ins]
    in_specs += [pl.BlockSpec(a.shape, lambda j, i, nd=a.ndim: (0,) * nd) for a in full_ins]
    out_specs = [pl.BlockSpec((tm, bw), lambda j, i: (i, j)) for (_, bw, _) in tiled_outs]
    out_shape = [jax.ShapeDtypeStruct((rows, w), dt) for (w, _, dt) in tiled_outs]
    extra, aliases = [], {}
    if slab is not None:
        arr, total, col0, width = slab
        out_specs.insert(0, pl.BlockSpec((pl.Element(tm), pl.Element(width)), lambda j, i: (i * tm, col0)))
        out_shape.insert(0, jax.ShapeDtypeStruct((rows, total), BF16))
        if arr is not None:
            extra, aliases = [arr], {n_ti + n_fi: 0}
            in_specs.append(ANY)
    out_specs += [pl.BlockSpec(s, lambda j, i: (0, 0)) for s in acc_shapes]
    out_shape += [jax.ShapeDtypeStruct(s, F32) for s in acc_shapes]
    return pl.pallas_call(
        body, name=name, grid=(ncol, rows // tm), in_specs=in_specs, out_specs=out_specs, out_shape=out_shape,
        input_output_aliases=aliases, compiler_params=_cparams(("parallel", "arbitrary"), vmem_mb),
    )(*[t[0] for t in tiled_ins], *full_ins, *extra)


def _f_rms(x, w):
    return x * lax.rsqrt(jnp.mean(x * x, axis=-1, keepdims=True) + EPS) * w


def _rms_fwd(name, x, w):
    rows = x.shape[0]
    return _tilemap(name, lambda xv, wv: (_f_rms(xv, wv),), [(x, D, 0)], [w], [(D, D, BF16)], [],
                    rows=rows, tm=_tile(rows, 256))[0]


def _rms_bwd(name, x, w, dh, dres):
    rows = x.shape[0]

    def fn(xv, dhv, drv, wv):
        _, vjp = jax.vjp(_f_rms, xv, wv)
        dx, dw = vjp(dhv.astype(F32))
        return dx + drv, dw

    return _tilemap(name, fn, [(x, D, 0), (dh, D, 0), (dres, D, 0)], [w], [(D, D, F32)], [(1, D)],
                    rows=rows, tm=_tile(rows, 256))


def _f_mixa_head(g, o, gw):
    mu = jnp.mean(o, axis=-1, keepdims=True)
    d = o - mu
    var = jnp.mean(d * d, axis=-1, keepdims=True)
    return g * jax.nn.sigmoid(g) * (d * lax.rsqrt(var + EPS) * gw)


def _heads(width):
    return [slice(h * width, (h + 1) * width) for h in range(HEADS)]


def _mixa_fwd(name, z, o, gw):
    rows = z.shape[0]

    def fn(g, ov, gwv):
        g, ov = g.astype(F32), ov.astype(F32)
        return (jnp.concatenate([_f_mixa_head(g[:, s], ov[:, s], gwv[:, s]) for s in _heads(DV)], axis=1),)

    return _tilemap(name, fn, [(z, D, Z_G), (o, D, 0)], [gw], [(D, D, BF16)], [], rows=rows, tm=_tile(rows, 256))[0]


def _mixa_bwd(name, z, o, gw, da, dz):
    rows = z.shape[0]

    def fn(g, ov, dav, gwv):
        g, ov, dav = g.astype(F32), ov.astype(F32), dav.astype(F32)
        parts = []
        for s in _heads(DV):
            _, vjp = jax.vjp(_f_mixa_head, g[:, s], ov[:, s], gwv[:, s])
            parts.append(vjp(dav[:, s]))
        return tuple(jnp.concatenate([p[i] for p in parts], axis=1) for i in range(3))

    return _tilemap(name, fn, [(z, D, Z_G), (o, D, 0), (da, D, 0)], [gw], [(D, D, BF16)], [(1, D)],
                    rows=rows, tm=_tile(rows, 256), slab=(dz, IN_COLS, Z_G * D, D))


def _f_merge(ga, gb, a, b):
    return jax.nn.sigmoid(ga) * a + jax.nn.sigmoid(gb) * b


def _merge_fwd(name, z, a, b):
    rows, bw = z.shape[0], 1024

    def fn(ga, gb, av, bv):
        return (_f_merge(ga.astype(F32), gb.astype(F32), av.astype(F32), bv.astype(F32)),)

    return _tilemap(name, fn, [(z, bw, Z_GA * 2), (z, bw, Z_GB * 2), (a, bw, 0), (b, bw, 0)], [], [(D, bw, BF16)], [],
                    rows=rows, tm=_tile(rows, 512), ncol=D // bw)[0]


def _merge_bwd(name, z, a, b, dm):
    rows = z.shape[0]

    def fn(ga, gb, av, bv, dmv):
        _, vjp = jax.vjp(_f_merge, ga.astype(F32), gb.astype(F32), av.astype(F32), bv.astype(F32))
        dga, dgb, dav, dbv = vjp(dmv.astype(F32))
        return jnp.concatenate([dga, dgb], axis=1), dav, dbv

    return _tilemap(name, fn, [(z, D, Z_GA), (z, D, Z_GB), (a, D, 0), (b, D, 0), (dm, D, 0)], [],
                    [(D, D, BF16)] * 2, [], rows=rows, tm=_tile(rows, 256), slab=(None, IN_COLS, Z_GA * D, 2 * D))


def _f_swiglu(a, c):
    return a * jax.nn.sigmoid(a) * c


def _swiglu_fwd(name, ac):
    rows = ac.shape[0]

    def fn(a, c):
        return (_f_swiglu(a.astype(F32), c.astype(F32)),)

    return _tilemap(name, fn, [(ac, SH_FF, 0), (ac, SH_FF, 4)], [], [(DFF, SH_FF, BF16)], [],
                    rows=rows, tm=_tile(rows, 512), ncol=4)[0]


def _swiglu_bwd(name, ac, df):
    rows = ac.shape[0]

    def fn(a, c, dfv):
        _, vjp = jax.vjp(_f_swiglu, a.astype(F32), c.astype(F32))
        return (jnp.concatenate(vjp(dfv.astype(F32)), axis=1),)

    return _tilemap(name, fn, [(ac, DFF, 0), (ac, DFF, 1), (df, DFF, 0)], [], [(2 * DFF, 2 * DFF, BF16)], [],
                    rows=rows, tm=_tile(rows, 128))[0]


def _loss_bwd(name, x, w, target):
    rows = x.shape[0]

    def fn(xv, tv, wv):
        y, vjp = jax.vjp(_f_rms, xv, wv)
        e = y - tv
        dx, dw = vjp(e * (1.0 / D))
        per_row = jnp.mean(e * e, axis=-1, keepdims=True)
        return dx, dw, jnp.broadcast_to(0.5 * jnp.sum(per_row, axis=0, keepdims=True), (1, 128))

    return _tilemap(name, fn, [(x, D, 0), (target, D, 0)], [w], [(D, D, F32)], [(1, D), (1, 128)],
                    rows=rows, tm=_tile(rows, 256))


def _gelu(v):
    return 0.5 * v * (1.0 + lax.erf(v * (1.0 / math.sqrt(2.0))))


def _f_sgu_pre(su, sv, lw, lb):
    zv = _gelu(sv)
    mu = jnp.mean(zv, axis=-1, keepdims=True)
    d = zv - mu
    var = jnp.mean(d * d, axis=-1, keepdims=True)
    return _gelu(su), d * lax.rsqrt(var + EPS) * lw + lb


def _sgu_blocks(tm):
    return [slice(r * SLEN, (r + 1) * SLEN) for r in range(tm // SLEN)]


def _sgu_mix(wm, vnb, bexp, tm):
    rows = []
    for r in _sgu_blocks(tm):
        cols = [jnp.dot(wm[g], vnb[r, g * SCH:(g + 1) * SCH], preferred_element_type=F32) for g in range(GROUPS)]
        rows.append(jnp.concatenate(cols, axis=1) + bexp)
    return jnp.concatenate(rows, axis=0)


def _sgu_fwd(name, z, lw, lb, wm, bexp):
    rows = z.shape[0]
    tm = _tile(rows, 256)

    def fn(su, sv, lwv, lbv, wmv, bev):
        zu, vn = _f_sgu_pre(su.astype(F32), sv.astype(F32), lwv, lbv)
        return (zu * _sgu_mix(wmv, vn.astype(BF16), bev, tm),)

    return _tilemap(name, fn, [(z, D, Z_SU), (z, D, Z_SV)], [lw, lb, wm, bexp], [(D, D, BF16)], [], rows=rows, tm=tm)[0]


def _sgu_bwd(name, z, lw, lb, wm, wmt, bexp, maskf, db, dz):
    rows = z.shape[0]
    tm = _tile(rows, 256)

    def fn(su, sv, dbv, lwv, lbv, wmv, wmtv, bev, mkv):
        (zu, vn), vjp = jax.vjp(_f_sgu_pre, su.astype(F32), sv.astype(F32), lwv, lbv)
        vnb = vn.astype(BF16)
        dbv = dbv.astype(F32)
        dmixed = dbv * zu
        dzu = dbv * _sgu_mix(wmv, vnb, bev, tm)
        dmb = dmixed.astype(BF16)
        dvn_rows, dws, dbf = [], [None] * GROUPS, None
        for r in _sgu_blocks(tm):
            cols = []
            for g in range(GROUPS):
                cs = slice(g * SCH, (g + 1) * SCH)
                cols.append(jnp.dot(wmtv[g], dmb[r, cs], preferred_element_type=F32))
                dw = lax.dot_general(dmb[r, cs], vnb[r, cs], (((1,), (1,)), ((), ())), preferred_element_type=F32)
                dws[g] = dw if dws[g] is None else dws[g] + dw
            dvn_rows.append(jnp.concatenate(cols, axis=1))
            dbf = dmixed[r] if dbf is None else dbf + dmixed[r]
        dsu, dsv, dlw, dlb = vjp((dzu, jnp.concatenate(dvn_rows, axis=0)))
        dws_all = jnp.concatenate([dw * mkv for dw in dws], axis=0)
        return jnp.concatenate([dsu, dsv], axis=1), dws_all, dbf, dlw, dlb

    return _tilemap(name, fn, [(z, D, Z_SU), (z, D, Z_SV), (db, D, 0)], [lw, lb, wm, wmt, bexp, maskf],
                    [], [(GROUPS * SLEN, SLEN), (SLEN, D), (1, D), (1, D)], rows=rows, tm=tm,
                    slab=(dz, IN_COLS, Z_SU * D, 2 * D))


def _group_sums(name, dbf):
    def body(e_ref, x_ref, o_ref):
        o_ref[...] = lax.dot_general(e_ref[...], x_ref[...], (((1,), (1,)), ((), ())),
                                     precision=lax.Precision.HIGHEST, preferred_element_type=F32)

    ind = (jnp.arange(D)[None, :] // SCH == jnp.arange(GROUPS)[:, None]).astype(F32)
    return pl.pallas_call(body, name=name, out_shape=jax.ShapeDtypeStruct((GROUPS, SLEN), F32))(ind, dbf)


def _ret_consts(rows):
    hh = jnp.arange(HEADS, dtype=F32)
    log_g = jnp.log1p(-(2.0 ** (-5.0 - hh)))
    idx = jnp.arange(RL, dtype=F32)
    ci = jnp.arange(RL) // CHUNK
    mask = ci[None, :] <= ci[:, None]
    dm = jnp.where(mask[None], jnp.exp(log_g[:, None, None] * jnp.abs(idx[:, None] - idx[None, :])), 0.0)
    qd = jnp.broadcast_to(jnp.exp(log_g[:, None] * (idx[None, :] + 1.0))[:, :, None], (HEADS, RL, DK))
    kd = jnp.broadcast_to(jnp.exp(log_g[:, None] * (RL - 1.0 - idx[None, :]))[:, :, None], (HEADS, RL, DK))
    cd = jnp.broadcast_to(jnp.exp(log_g * RL)[:, None, None], (HEADS, DK, DV))
    half = DK // 2
    inv = ROPE_BASE ** (-jnp.arange(half, dtype=F32) / half)
    ang = jnp.arange(rows, dtype=jnp.int32).astype(F32)[:, None] * inv[None, :]
    cos, sin = jnp.cos(ang), jnp.sin(ang)
    return dict(dm=dm.astype(F32), qd=qd, kd=kd, cd=cd, cos=jnp.concatenate([cos, cos], axis=1),
                sin=jnp.concatenate([-sin, sin], axis=1))


def _rope(x, cos, sin_signed):
    return x * cos + pltpu.roll(x, DK // 2, 1) * sin_signed


def _rope_t(d, cos, sin_signed):
    return d * cos + pltpu.roll(d * sin_signed, DK // 2, 1)


def _dot(a, b, ca, cb):
    return lax.dot_general(a, b, (((ca,), (cb,)), ((), ())), preferred_element_type=F32)


QKV = 2 * HEADS * DK + HEADS * DV


def _ret_in_specs(blk):
    whole = lambda shape: pl.BlockSpec(shape, lambda n: (0,) * len(shape))
    return [pl.BlockSpec((RL, QKV), lambda n: (blk(n), 0)), pl.BlockSpec((RL, DK), lambda n: (blk(n), 0)),
            pl.BlockSpec((RL, DK), lambda n: (blk(n), 0)), whole((HEADS, RL, RL)), whole((HEADS, RL, DK)),
            whole((HEADS, RL, DK)), whole((HEADS, DK, DV))]


def _ret_head(ref, h):
    q0, k0, v0 = h * DK, HEADS * DK + h * DK, 2 * HEADS * DK + h * DV
    return (slice(None), slice(q0, q0 + DK)), (slice(None), slice(k0, k0 + DK)), (slice(None), slice(v0, v0 + DV))


def _ret_fwd(name, z, rc):
    rows = z.shape[0]
    nb = rows // RL

    def body(qkv_ref, cos_ref, sin_ref, dm_ref, qd_ref, kd_ref, cd_ref, o_ref, st_ref, state):
        @pl.when(pl.program_id(0) == 0)
        def _():
            state[...] = jnp.zeros_like(state)

        cos, sin = cos_ref[...], sin_ref[...]
        for h in range(HEADS):
            qs, ks, vs = _ret_head(qkv_ref, h)
            qf = _rope(qkv_ref[qs].astype(F32), cos, sin) * (DK ** -0.5)
            kf = _rope(qkv_ref[ks].astype(F32), cos, sin)
            vb = qkv_ref[vs]
            s0 = state[h]
            s0b = s0.astype(BF16)
            st_ref[h] = s0b
            sc = _dot(qf.astype(BF16), kf.astype(BF16), 1, 1) * dm_ref[h]
            o = _dot(sc.astype(BF16), vb, 1, 0) + _dot((qf * qd_ref[h]).astype(BF16), s0b, 1, 0)
            o_ref[:, h * DV:(h + 1) * DV] = o.astype(o_ref.dtype)
            state[h] = cd_ref[h] * s0 + _dot((kf * kd_ref[h]).astype(BF16), vb, 0, 0)

    return pl.pallas_call(
        body, name=name, grid=(nb,), in_specs=_ret_in_specs(lambda n: n),
        out_specs=[pl.BlockSpec((RL, HEADS * DV), lambda n: (n, 0)),
                   pl.BlockSpec((None, HEADS, DK, DV), lambda n: (n, 0, 0, 0))],
        out_shape=[jax.ShapeDtypeStruct((rows, HEADS * DV), BF16), jax.ShapeDtypeStruct((nb, HEADS, DK, DV), BF16)],
        scratch_shapes=[pltpu.VMEM((HEADS, DK, DV), F32)],
        compiler_params=_cparams(("arbitrary",), 40),
    )(z, rc["cos"], rc["sin"], rc["dm"], rc["qd"], rc["kd"], rc["cd"])


def _ret_bwd(name, z, rc, states, do, dz):
    rows = z.shape[0]
    nb = rows // RL
    rev = lambda n: nb - 1 - n

    def body(qkv_ref, cos_ref, sin_ref, dm_ref, qd_ref, kd_ref, cd_ref, st_ref, do_ref, dz_in, dqkv_ref, dstate):
        @pl.when(pl.program_id(0) == 0)
        def _():
            dstate[...] = jnp.zeros_like(dstate)

        cos, sin = cos_ref[...], sin_ref[...]
        for h in range(HEADS):
            qs, ks, vs = _ret_head(qkv_ref, h)
            dm, qd, kd = dm_ref[h], qd_ref[h], kd_ref[h]
            qf = _rope(qkv_ref[qs].astype(F32), cos, sin) * (DK ** -0.5)
            kf = _rope(qkv_ref[ks].astype(F32), cos, sin)
            qb, kb, vb, s0b = qf.astype(BF16), kf.astype(BF16), qkv_ref[vs], st_ref[h]
            dob = do_ref[:, h * DV:(h + 1) * DV]
            qsb, ksb = (qf * qd).astype(BF16), (kf * kd).astype(BF16)
            ds1 = dstate[h]
            ds1b = ds1.astype(BF16)
            scb = (_dot(qb, kb, 1, 1) * dm).astype(BF16)
            dscb = (_dot(dob, vb, 1, 1) * dm).astype(BF16)
            dq = _dot(dscb, kb, 1, 0) + _dot(dob, s0b, 1, 1) * qd
            dk = _dot(dscb, qb, 0, 0) + _dot(vb, ds1b, 1, 1) * kd
            dv = _dot(scb, dob, 0, 0) + _dot(ksb, ds1b, 1, 0)
            dqkv_ref[qs] = _rope_t(dq * (DK ** -0.5), cos, sin).astype(dqkv_ref.dtype)
            dqkv_ref[ks] = _rope_t(dk, cos, sin).astype(dqkv_ref.dtype)
            dqkv_ref[vs] = dv.astype(dqkv_ref.dtype)
            dstate[h] = cd_ref[h] * ds1 + _dot(qsb, dob, 0, 0)

    return pl.pallas_call(
        body, name=name, grid=(nb,),
        in_specs=_ret_in_specs(rev) + [pl.BlockSpec((None, HEADS, DK, DV), lambda n: (rev(n), 0, 0, 0)),
                                       pl.BlockSpec((RL, HEADS * DV), lambda n: (rev(n), 0)), ANY],
        out_specs=pl.BlockSpec((RL, QKV), lambda n: (rev(n), 0)),
        out_shape=jax.ShapeDtypeStruct(dz.shape, dz.dtype), input_output_aliases={9: 0},
        scratch_shapes=[pltpu.VMEM((HEADS, DK, DV), F32)],
        compiler_params=_cparams(("arbitrary",), 40),
    )(z, rc["cos"], rc["sin"], rc["dm"], rc["qd"], rc["kd"], rc["cd"], states, do, dz)


def _layer_fwd(l, x, p, rc):
    n = f"l{l}_"
    h = _rms_fwd(n + "rms1", x, p["norm_mix_w"])
    z = _mm_nn_sh(n + "in_proj", h, p["w_in"], out_dtype=BF16)
    o, states = _ret_fwd(n + "ret", z, rc)
    a_in = _mixa_fwd(n + "mixa", z, o, p["ret_gn_w"])
    b_in = _sgu_fwd(n + "sgu", z, p["sgu_ln_w"], p["sgu_ln_b"], p["wm"], p["bexp"])
    if "late" in p:
        p.update(p.pop("late")(b_in))
    br_a = _mm_nn(n + "ret_proj", a_in, p["ret_proj"], out_dtype=BF16)
    br_b = _mm_nn(n + "sgu_proj", b_in, p["sgu_proj"], out_dtype=BF16)
    merged = _merge_fwd(n + "merge", z, br_a, br_b)
    x1 = _mm_nn(n + "out_proj", merged, p["w_out"], out_dtype=F32, res=x)
    h2 = _rms_fwd(n + "rms2", x1, p["norm_ffn_w"])
    ac = _mm_nn_sh(n + "ffn_in", h2, p["w_ffn_in"], out_dtype=BF16)
    f = _swiglu_fwd(n + "swiglu", ac)
    x2 = _mm_nn(n + "ffn_out", f, p["w_ffn_out"], out_dtype=F32, res=x1, tk=2 * SH_FF)
    saved = dict(x=x, h=h, z=z, o=o, states=states, a_in=a_in, b_in=b_in, br_a=br_a, br_b=br_b, merged=merged,
                 x1=x1, h2=h2, ac=ac, f=f)
    return x2, saved


def _layer_bwd_ffn(l, dx2, p, s, after):
    n = f"l{l}b_"
    df = _mm_nt(n + "d_f", dx2, p["w_ffn_out"], out_dtype=BF16, tn=SH_FF, after=after)
    g_ffn_out = _mm_tn(n + "g_ffn_out", s["f"], dx2, out_dtype=BF16, tm=SH_FF)
    dac = _swiglu_bwd(n + "swiglu", s["ac"], df)
    dh2 = _mm_nt_sh(n + "d_h2", dac, p["w_ffn_in"], out_dtype=BF16)
    g_ffn_in = _mm_tn_sh(n + "g_ffn_in", s["h2"], dac, SH_FF, out_dtype=BF16)
    dx1, g_nffn = _rms_bwd(n + "rms2", s["x1"], p["norm_ffn_w"], dh2, dx2)
    return dx1, dict(w_ffn_in=g_ffn_in, w_ffn_out=g_ffn_out.reshape(NDEV, -1, D)), dict(norm_ffn_w=g_nffn)


def _layer_bwd_mix(l, dx1, p, s, rc, after, send):
    n = f"l{l}b_"
    dmerged = _mm_nt(n + "d_merged", dx1, p["w_out"], out_dtype=BF16, after=after)
    g_out = _mm_tn(n + "g_out", s["merged"], dx1, out_dtype=BF16)
    dz, dbr_a, dbr_b = _merge_bwd(n + "merge", s["z"], s["br_a"], s["br_b"], dmerged)
    da_in = _mm_nt(n + "d_a_in", dbr_a, p["ret_proj"], out_dtype=BF16)
    g_ret_proj = _mm_tn(n + "g_ret_proj", s["a_in"], dbr_a, out_dtype=BF16, tk=2048)
    db_in = _mm_nt(n + "d_b_in", dbr_b, p["sgu_proj"], out_dtype=BF16)
    g_sgu_proj = _mm_tn(n + "g_sgu_proj", s["b_in"], dbr_b, out_dtype=BF16, tk=2048)
    dz, do, g_gn = _mixa_bwd(n + "mixa", s["z"], s["o"], p["ret_gn_w"], da_in, dz)
    dz = _ret_bwd(n + "ret", s["z"], rc, s["states"], do, dz)
    dz, g_ws, dbf, g_lw, g_lb = _sgu_bwd(n + "sgu", s["z"], p["sgu_ln_w"], p["sgu_ln_b"], p["wm"], p["wmt"],
                                          p["bexp"], p["maskf"], db_in, dz)
    g_bs = _group_sums(n + "g_bs", dbf)
    g_in = _mm_tn_sh(n + "g_in", s["h"], dz, SH_IN, out_dtype=BF16)
    token = send(dict(w_in=g_in, ret_proj=g_ret_proj.reshape(NDEV, -1, D), sgu_proj=g_sgu_proj.reshape(NDEV, -1, D),
                      w_out=g_out.reshape(NDEV, -1, D)))
    dh = _mm_nt_sh(n + "d_h", dz, p["w_in"], out_dtype=BF16, after=token)
    dx, g_nmix = _rms_bwd(n + "rms1", s["x"], p["norm_mix_w"], dh, dx1)
    small = dict(norm_mix_w=g_nmix, ret_gn_w=g_gn, sgu_ln_w=g_lw, sgu_ln_b=g_lb, sgu_w_s=g_ws, sgu_b_s=g_bs)
    return dx, small


def _slot(ref, p):
    return ref.at[4 * p[0] + 2 * p[1] + p[2]]


def _allgather(name, shards):
    n = len(shards)

    def body(*refs):
        ins, outs = refs[:n], refs[n:2 * n]
        send_sems, recv_sems, local_sems = refs[2 * n:]
        x, y, c = lax.axis_index("x"), lax.axis_index("y"), lax.axis_index("c")
        me, sibling = (x, y, c), (x, y, 1 - c)
        chips = [(1 - x, y), (x, 1 - y), (1 - x, 1 - y)]

        def copy(a, k, block, to, src=None):
            return pltpu.make_async_remote_copy(
                src_ref=_slot(outs[a], block) if src is None else src, dst_ref=_slot(outs[a], block),
                send_sem=send_sems.at[a, k], recv_sem=recv_sems.at[a, k], device_id=to, device_id_type=MESH_T)

        mine = [pltpu.make_async_copy(ins[a], _slot(outs[a], me), local_sems.at[a]) for a in range(n)]
        for cp in mine:
            cp.start()
        first = []
        for a in range(n):
            first.append(copy(a, 0, me, sibling, src=ins[a]))
            first += [copy(a, 1 + j, me, (*chip, c), src=ins[a]) for j, chip in enumerate(chips)]
        for cp in first:
            cp.start()
        passed = []
        for j, chip in enumerate(chips):
            for a in range(n):
                copy(a, 1 + j, (*chip, c), me).wait_recv()
                cp = copy(a, 4 + j, (*chip, c), sibling)
                cp.start()
                passed.append(cp)
        for a in range(n):
            copy(a, 0, sibling, me).wait_recv()
            for j, chip in enumerate(chips):
                copy(a, 4 + j, (*chip, 1 - c), me).wait_recv()
        for cp in first + passed:
            cp.wait_send()
        for cp in mine:
            cp.wait()

    return pl.pallas_call(
        body, name=name, in_specs=[ANY] * n, out_specs=[ANY] * n,
        out_shape=[jax.ShapeDtypeStruct((NDEV,) + s.shape, s.dtype) for s in shards],
        scratch_shapes=[pltpu.SemaphoreType.DMA((n, 7)), pltpu.SemaphoreType.DMA((n, 7)), pltpu.SemaphoreType.DMA((n,))],
    )(*shards)


def _peers(x, y, c):
    return [((1 - x) if fx else x, (1 - y) if fy else y, (1 - c) if fc else c)
            for fx in (0, 1) for fy in (0, 1) for fc in (0, 1) if fx + fy + fc]


def _exchange_copy(whole, src_ref, land_ref, send_sems, recv_sems, a, k, peer, me, landed):
    return pltpu.make_async_remote_copy(
        src_ref=src_ref if whole else _slot(src_ref, peer), dst_ref=_slot(land_ref, peer if landed else me),
        send_sem=send_sems.at[7 * a + k], recv_sem=recv_sems.at[7 * a + k], device_id=peer, device_id_type=MESH_T)


def _exchange_start(name, srcs, lands, whole, after):
    n = len(srcs)

    def body(*refs):
        src_refs, land_refs = refs[:n], refs[n:2 * n]
        send_sems, recv_sems = refs[2 * n + 1], refs[2 * n + 2]
        token = refs[-1]
        x, y, c = lax.axis_index("x"), lax.axis_index("y"), lax.axis_index("c")
        for a in range(n):
            for k, peer in enumerate(_peers(x, y, c)):
                _exchange_copy(whole, src_refs[a], land_refs[a], send_sems, recv_sems, a, k, peer, (x, y, c), False).start()
        token[...] = jnp.zeros_like(token)

    hbm = lambda t: pltpu.HBM(t.shape, t.dtype)
    out = pl.pallas_call(
        body, name=name, in_specs=[HBM_SPEC] * (2 * n) + [ANY],
        out_specs=(SEM_SPEC, SEM_SPEC, *[HBM_SPEC] * (2 * n), pl.BlockSpec(memory_space=pltpu.VMEM)),
        out_shape=(pltpu.SemaphoreType.DMA((7 * n,)), pltpu.SemaphoreType.DMA((7 * n,)), *[hbm(t) for t in srcs],
                   *[hbm(t) for t in lands], jax.ShapeDtypeStruct((8, 128), F32)),
        input_output_aliases={i: 2 + i for i in range(2 * n)},
        compiler_params=pltpu.CompilerParams(has_side_effects=pltpu.SideEffectType.DATAFLOW_SIDE_EFFECTING),
    )(*[pltpu.with_memory_space_constraint(t, pltpu.HBM) for t in (*srcs, *lands)], after)
    return dict(send=out[0], recv=out[1], srcs=out[2:2 + n], lands=out[2 + n:2 + 2 * n], token=out[-1])


def _exchange_wait(name, started, whole, after):
    n = len(started["srcs"])

    def body(*refs):
        src_refs, land_refs = refs[:n], refs[n:2 * n]
        send_sems, recv_sems = refs[2 * n], refs[2 * n + 1]
        x, y, c = lax.axis_index("x"), lax.axis_index("y"), lax.axis_index("c")
        for a in range(n):
            for k, peer in enumerate(_peers(x, y, c)):
                cp = _exchange_copy(whole, src_refs[a], land_refs[a], send_sems, recv_sems, a, k, peer, (x, y, c), True)
                cp.wait_send()
                cp.wait_recv()

    hbm = lambda t: pltpu.HBM(t.shape, t.dtype)
    out = pl.pallas_call(
        body, name=name, in_specs=[HBM_SPEC] * (2 * n) + [SEM_SPEC, SEM_SPEC, ANY], out_specs=[HBM_SPEC] * (2 * n),
        out_shape=[hbm(t) for t in (*started["srcs"], *started["lands"])],
        input_output_aliases={i: i for i in range(2 * n)},
        compiler_params=pltpu.CompilerParams(has_side_effects=pltpu.SideEffectType.DATAFLOW_SIDE_EFFECTING),
    )(*started["srcs"], *started["lands"], started["send"], started["recv"], after)
    return out[n:]


def _adamw(w, g, m, v):
    m = ADAM_B1 * m + (1.0 - ADAM_B1) * g
    v = ADAM_B2 * v + (1.0 - ADAM_B2) * (g * g)
    m_hat = m / (1.0 - ADAM_B1 ** ADAM_STEP)
    v_hat = v / (1.0 - ADAM_B2 ** ADAM_STEP)
    return -ADAM_LR * (m_hat / (jnp.sqrt(v_hat) + ADAM_EPS) + ADAM_WD * w), m, v


def _sum_and_adamw(name, recv, w, m, v, layer, prev):
    _, r, c = recv.shape
    tr = max(t for t in range(16, r + 1, 16) if r % t == 0 and t * c <= ADAM_TILE_ELEMS)

    def body(recv_ref, w_ref, m_ref, v_ref, *rest):
        g_ref, d_ref, m_out, v_out = rest[4:]
        g = recv_ref[0].astype(F32)
        for s in range(1, NDEV):
            g = g + recv_ref[s].astype(F32)
        delta, m_new, v_new = _adamw(w_ref[...], g, m_ref[...], v_ref[...])
        g_ref[...] = g
        d_ref[...] = delta
        m_out[...] = m_new
        v_out[...] = v_new

    lay = pl.BlockSpec((None, tr, c), lambda i: (layer, i, 0))
    return pl.pallas_call(
        body, name=name, grid=(r // tr,),
        in_specs=[pl.BlockSpec((NDEV, tr, c), lambda i: (0, i, 0)), lay, lay, lay] + [ANY] * 4,
        out_specs=[lay] * 4, out_shape=[jax.ShapeDtypeStruct(w.shape, F32)] * 4,
        input_output_aliases={4: 0, 5: 1, 6: 2, 7: 3},
        compiler_params=_cparams(("arbitrary",), 48),
    )(recv, w, m, v, *prev)


def _small_sum_and_adamw(name, recv, w, m, v):
    r = w.shape[0]

    def body(recv_ref, w_ref, m_ref, v_ref, g_ref, d_ref, m_out, v_out):
        g = recv_ref[0]
        for s in range(1, NDEV):
            g = g + recv_ref[s]
        delta, m_new, v_new = _adamw(w_ref[...], g, m_ref[...], v_ref[...])
        g_ref[...] = g
        d_ref[...] = delta
        m_out[...] = m_new
        v_out[...] = v_new

    return pl.pallas_call(body, name=name, out_shape=[jax.ShapeDtypeStruct((r, 128), F32)] * 4,
                          compiler_params=_cparams(None, 48))(recv, w, m, v)


BIG = ("w_in", "ret_proj", "sgu_proj", "w_out", "w_ffn_in", "w_ffn_out")
SMALL = ("norm_mix_w", "ret_gn_w", "sgu_ln_w", "sgu_ln_b", "sgu_w_s", "sgu_b_s", "norm_ffn_w", "final_norm_w")
WEIGHTS = ("norm_mix_w", "w_in", "ret_gn_w", "ret_proj", "sgu_ln_w", "sgu_ln_b", "sgu_w_s", "sgu_b_s", "sgu_proj",
           "w_out", "norm_ffn_w", "w_ffn_in", "w_ffn_out", "final_norm_w")


def _pack_small(d):
    return jnp.concatenate([d[k].reshape(-1, 128) for k in SMALL], axis=0)


def _unpack_small(packed, like):
    out, r0 = {}, 0
    for k in SMALL:
        r = like[k].size // 128
        out[k] = packed[r0:r0 + r].reshape(like[k].shape)
        r0 += r
    return out


def _gathered(keys, arrays):
    return {k: a if k in ("w_in", "w_ffn_in") else a.reshape(-1, D) for k, a in zip(keys, arrays)}


def _layer_params(l, w, gathered, mask):
    p = _gathered(BIG, gathered)
    for k in ("norm_mix_w", "ret_gn_w", "sgu_ln_w", "sgu_ln_b", "norm_ffn_w"):
        p[k] = w[k][l][None, :]
    wm = jnp.where(mask[None], w["sgu_w_s"][l], 0.0)
    p["wm"], p["wmt"] = wm.astype(BF16), jnp.swapaxes(wm, 1, 2).astype(BF16)
    p["bexp"] = jnp.repeat(w["sgu_b_s"][l].T, SCH, axis=1)
    p["maskf"] = mask.astype(F32)
    return p


def _step(w, m, v, x, target):
    depth = w["w_in"].shape[0]
    rc = _ret_consts(x.shape[0])
    pos = jnp.arange(SLEN)
    mask = (pos[None, :] // CHUNK) <= (pos[:, None] // CHUNK)
    shards = [[w[k][l].astype(BF16) for k in BIG] for l in range(depth)]
    me = 4 * lax.axis_index("x") + 2 * lax.axis_index("y") + lax.axis_index("c")

    def lands_of(ts):
        return [lax.dynamic_update_index_in_dim(lax.empty((NDEV,) + t.shape, t.dtype), t, me, 0) for t in ts]

    w_in0 = _allgather("gather_l0_in", shards[0][:1])[0]
    rest0 = _exchange_start("gather_start_l0", shards[0][1:], lands_of(shards[0][1:]), True, w_in0)
    gathered = [w_in0] + list(rest0["lands"])
    saved, layers = [], []
    for l in range(depth):
        p = _layer_params(l, w, gathered, mask)
        if l == 0:
            p["norm_mix_w"] = p["norm_mix_w"] + rest0["token"][0, 0]
            p["late"] = lambda after: _gathered(BIG[1:], _exchange_wait("gather_wait_l0", rest0, True, after))
        if l + 1 < depth:
            started = _exchange_start(f"gather_start_l{l + 1}", shards[l + 1], lands_of(shards[l + 1]), True, gathered[0])
            p["norm_mix_w"] = p["norm_mix_w"] + started["token"][0, 0]
        x, s = _layer_fwd(l, x, p, rc)
        if l + 1 < depth:
            gathered = _exchange_wait(f"gather_wait_l{l + 1}", started, True, x)
        saved.append(s)
        layers.append(p)

    dx, g_final, loss_row = _loss_bwd("loss", x, w["final_norm_w"][None, :], target)
    loss = lax.psum(loss_row[0, 0], ("x", "y", "c"))

    res = {k: [lax.empty(w[k].shape, F32) for _ in range(4)] for k in BIG}
    small = {k: [None] * depth for k in SMALL if k != "final_norm_w"}

    def scatter_start(tag, big, after):
        keys = [k for k in BIG if k in big]
        parts = [big[k] for k in keys]
        lands = [lax.dynamic_update_index_in_dim(lax.empty(t.shape, t.dtype), lax.dynamic_index_in_dim(t, me, 0), me, 0)
                 for t in parts]
        return keys, _exchange_start(f"scatter_start_{tag}", parts, lands, False, after)

    def finish(l, pending, after):
        for tag, (keys, started) in pending:
            recv = _exchange_wait(f"scatter_wait_{tag}", started, False, after)
            for k, rv in zip(keys, recv):
                res[k] = _sum_and_adamw(f"adamw_{k}_l{l}", rv, w[k], m[k], v[k], l, res[k])

    pending, token = None, loss_row
    for l in reversed(range(depth)):
        dx1, big_ffn, sm_ffn = _layer_bwd_ffn(l, dx, layers[l], saved[l], token)
        now = [(f"ffn_l{l}", scatter_start(f"ffn_l{l}", big_ffn, dx1))]

        def send(big_mix, l=l, now=now, dx1=dx1):
            now.append((f"mix_l{l}", scatter_start(f"mix_l{l}", big_mix, dx1)))
            return now[1][1][1]["token"]

        dx, sm_mix = _layer_bwd_mix(l, dx1, layers[l], saved[l], rc, now[0][1][1]["token"], send)
        token = now[1][1][1]["token"]
        if pending is not None:
            finish(l + 1, pending, dx)
        pending = now
        for k, g in {**sm_ffn, **sm_mix}.items():
            small[k][l] = g
    finish(0, pending, dx)

    g_small = {k: jnp.stack(small[k]).reshape(w[k].shape) for k in small}
    g_small["final_norm_w"] = g_final.reshape(w["final_norm_w"].shape)
    recv = _allgather("gather_small", [_pack_small(g_small)])[0]
    sres = _small_sum_and_adamw("adamw_small", recv, _pack_small(w), _pack_small(m), _pack_small(v))
    sres = [_unpack_small(a, w) for a in sres]

    outs = []
    for i in range(4):
        outs.append([sres[i][k] if k in SMALL else res[k][i] for k in WEIGHTS])
    return loss, dx, outs


def kernel(x, norm_mix_w, w_in, ret_gn_w, ret_proj, sgu_ln_w, sgu_ln_b, sgu_w_s, sgu_b_s, sgu_proj, w_out, norm_ffn_w, w_ffn_in, w_ffn_out, final_norm_w, loss_target, m_norm_mix_w, m_w_in, m_ret_gn_w, m_ret_proj, m_sgu_ln_w, m_sgu_ln_b, m_sgu_w_s, m_sgu_b_s, m_sgu_proj, m_w_out, m_norm_ffn_w, m_w_ffn_in, m_w_ffn_out, m_final_norm_w, v_norm_mix_w, v_w_in, v_ret_gn_w, v_ret_proj, v_sgu_ln_w, v_sgu_ln_b, v_sgu_w_s, v_sgu_b_s, v_sgu_proj, v_w_out, v_norm_ffn_w, v_w_ffn_in, v_w_ffn_out, v_final_norm_w):
    w = dict(zip(WEIGHTS, (norm_mix_w, w_in, ret_gn_w, ret_proj, sgu_ln_w, sgu_ln_b, sgu_w_s, sgu_b_s, sgu_proj, w_out,
                           norm_ffn_w, w_ffn_in, w_ffn_out, final_norm_w)))
    m = dict(zip(WEIGHTS, (m_norm_mix_w, m_w_in, m_ret_gn_w, m_ret_proj, m_sgu_ln_w, m_sgu_ln_b, m_sgu_w_s, m_sgu_b_s,
                           m_sgu_proj, m_w_out, m_norm_ffn_w, m_w_ffn_in, m_w_ffn_out, m_final_norm_w)))
    v = dict(zip(WEIGHTS, (v_norm_mix_w, v_w_in, v_ret_gn_w, v_ret_proj, v_sgu_ln_w, v_sgu_ln_b, v_sgu_w_s, v_sgu_b_s,
                           v_sgu_proj, v_w_out, v_norm_ffn_w, v_w_ffn_in, v_w_ffn_out, v_final_norm_w)))
    loss, dx, (grads, deltas, new_m, new_v) = _step(w, m, v, x.reshape(x.shape[1:]), loss_target.reshape(x.shape[1:]))
    return (loss, dx[None], *grads, *deltas, *new_m, *new_v)
```

```python
import functools
import math

import jax
import jax.numpy as jnp
from jax import lax
from jax.experimental import pallas as pl
from jax.experimental.pallas import tpu as pltpu

F32, BF16 = jnp.float32, jnp.bfloat16

D = 2048
HEADS, DK, DV = 8, 128, 256
GROUPS, SLEN = 8, 128
SCH = D // GROUPS
DFF = 5632
CHUNK = 64
IN_COLS = 14336
NDEV = 8
SH_IN = IN_COLS // NDEV
SH_FF = 2 * DFF // NDEV
EPS = 1e-6
ROPE_BASE = 10000.0
RL = 256
Z_G, Z_SU, Z_SV, Z_GA, Z_GB = 2, 3, 4, 5, 6

ADAM_TILE_ELEMS = 256 * 1024
ADAM_LR, ADAM_B1, ADAM_B2, ADAM_EPS, ADAM_WD, ADAM_STEP = 0.001, 0.9, 0.999, 1e-08, 0.01, 10

VMEM_BYTES_V7X = 64 << 20
MESH_T = pl.DeviceIdType.MESH
ANY = pl.BlockSpec(memory_space=pl.ANY)
HBM_SPEC = pl.BlockSpec(memory_space=pltpu.HBM)
SEM_SPEC = pl.BlockSpec(memory_space=pltpu.SEMAPHORE)


def _cparams(sem, vmem_mb):
    assert (vmem_mb << 20) < VMEM_BYTES_V7X
    return pltpu.CompilerParams(dimension_semantics=sem, vmem_limit_bytes=vmem_mb << 20)


def _mm(name, a, b, *, grid, a_spec, b_spec, o_spec, o_block, out_shape, out_dtype, contract, res=None, after=None,
        split=1, vmem_mb=48):
    nk = grid[2]
    dn = (contract, ((), ()))
    has_res = res is not None

    def body(*refs):
        a_ref, b_ref = refs[:2]
        r_ref = refs[2] if has_res else None
        o_ref, acc_ref = refs[-2:]
        if split == 1:
            part = lax.dot_general(a_ref[...].astype(BF16), b_ref[...].astype(BF16), dn, preferred_element_type=F32)
        else:
            kp = a_ref.shape[1] // split
            part = sum(lax.dot_general(a_ref[:, s * kp:(s + 1) * kp].astype(BF16), b_ref[s].astype(BF16), dn,
                                       preferred_element_type=F32) for s in range(split))

        def finish(val):
            if has_res:
                val = val + r_ref[...].astype(F32)
            o_ref[...] = val.astype(o_ref.dtype)

        if nk == 1:
            finish(part)
        else:
            k = pl.program_id(2)

            @pl.when(k == 0)
            def _():
                acc_ref[...] = part

            @pl.when(k > 0)
            def _():
                acc_ref[...] += part

            @pl.when(k == nk - 1)
            def _():
                finish(acc_ref[...])

    ins = [a, b] + ([res] if has_res else []) + ([after] if after is not None else [])
    in_specs = [a_spec, b_spec] + ([o_spec] if has_res else []) + ([ANY] if after is not None else [])
    return pl.pallas_call(
        body, name=name, grid=grid, in_specs=in_specs, out_specs=o_spec,
        out_shape=jax.ShapeDtypeStruct(out_shape, out_dtype),
        scratch_shapes=[pltpu.VMEM(o_block if nk > 1 else (8, 128), F32)],
        compiler_params=_cparams(("parallel", "parallel", "arbitrary"), vmem_mb),
    )(*ins)


def _tile(n, want):
    t = min(n, want)
    assert n % t == 0, (n, want)
    return t


def _mm_nn(name, a, b, *, out_dtype, tm=1024, tn=1024, tk=None, res=None, after=None):
    (m, k), n = a.shape, b.shape[1]
    tm, tn, tk = _tile(m, tm), _tile(n, tn), k if tk is None else tk
    return _mm(name, a, b, grid=(m // tm, n // tn, k // tk),
               a_spec=pl.BlockSpec((tm, tk), lambda i, j, kk: (i, kk)),
               b_spec=pl.BlockSpec((tk, tn), lambda i, j, kk: (kk, j)),
               o_spec=pl.BlockSpec((tm, tn), lambda i, j, kk: (i, j)), o_block=(tm, tn),
               out_shape=(m, n), out_dtype=out_dtype, contract=((1,), (0,)), res=res, after=after)


def _mm_nt(name, a, b, *, out_dtype, tm=1024, tn=1024, tk=None, after=None):
    (m, k), n = a.shape, b.shape[0]
    tm, tn, tk = _tile(m, tm), _tile(n, tn), k if tk is None else tk
    return _mm(name, a, b, grid=(m // tm, n // tn, k // tk),
               a_spec=pl.BlockSpec((tm, tk), lambda i, j, kk: (i, kk)),
               b_spec=pl.BlockSpec((tn, tk), lambda i, j, kk: (j, kk)),
               o_spec=pl.BlockSpec((tm, tn), lambda i, j, kk: (i, j)), o_block=(tm, tn),
               out_shape=(m, n), out_dtype=out_dtype, contract=((1,), (1,)), after=after)


def _mm_tn(name, a, b, *, out_dtype, tm=1024, tn=1024, tk=1024):
    (k, m), n = a.shape, b.shape[1]
    tm, tn, tk = _tile(m, tm), _tile(n, tn), _tile(k, tk)
    return _mm(name, a, b, grid=(m // tm, n // tn, k // tk),
               a_spec=pl.BlockSpec((tk, tm), lambda i, j, kk: (kk, i)),
               b_spec=pl.BlockSpec((tk, tn), lambda i, j, kk: (kk, j)),
               o_spec=pl.BlockSpec((tm, tn), lambda i, j, kk: (i, j)), o_block=(tm, tn),
               out_shape=(m, n), out_dtype=out_dtype, contract=((0,), (0,)))


def _mm_nn_sh(name, a, bsh, *, out_dtype, tm=1024):
    (m, k), ns = a.shape, bsh.shape[2]
    tm = _tile(m, tm)
    return _mm(name, a, bsh, grid=(m // tm, NDEV, 1),
               a_spec=pl.BlockSpec((tm, k), lambda i, j, kk: (i, 0)),
               b_spec=pl.BlockSpec((None, k, ns), lambda i, j, kk: (j, 0, 0)),
               o_spec=pl.BlockSpec((tm, ns), lambda i, j, kk: (i, j)), o_block=(tm, ns),
               out_shape=(m, NDEV * ns), out_dtype=out_dtype, contract=((1,), (0,)), vmem_mb=56)


def _mm_nt_sh(name, a, bsh, *, out_dtype, tm=1024, tn=1024, after=None):
    m, (_, n, ns) = a.shape[0], bsh.shape
    tm, tn = _tile(m, tm), _tile(n, tn)
    return _mm(name, a, bsh, grid=(m // tm, n // tn, NDEV // 2),
               a_spec=pl.BlockSpec((tm, 2 * ns), lambda i, j, kk: (i, kk)),
               b_spec=pl.BlockSpec((2, tn, ns), lambda i, j, kk: (kk, j, 0)),
               o_spec=pl.BlockSpec((tm, tn), lambda i, j, kk: (i, j)), o_block=(tm, tn),
               out_shape=(m, n), out_dtype=out_dtype, contract=((1,), (1,)), after=after, split=2)


def _mm_tn_sh(name, a, b, ns, *, out_dtype, tm=1024, tk=2048):
    k, m = a.shape
    tm, tk = _tile(m, tm), _tile(k, tk)
    return _mm(name, a, b, grid=(m // tm, NDEV, k // tk),
               a_spec=pl.BlockSpec((tk, tm), lambda i, j, kk: (kk, i)),
               b_spec=pl.BlockSpec((tk, ns), lambda i, j, kk: (kk, j)),
               o_spec=pl.BlockSpec((None, tm, ns), lambda i, j, kk: (j, i, 0)), o_block=(tm, ns),
               out_shape=(NDEV, m, ns), out_dtype=out_dtype, contract=((0,), (0,)), vmem_mb=56)


def _tilemap(name, fn, tiled_ins, full_ins, tiled_outs, acc_shapes, *, rows, tm, ncol=1, vmem_mb=48, slab=None):
    n_ti, n_fi, n_to, n_ao = len(tiled_ins), len(full_ins), len(tiled_outs), len(acc_shapes)
    assert ncol == 1 or (n_ao == 0 and slab is None)
    if slab is not None:
        n_to += 1

    def body(*refs):
        t_in, f_in = refs[:n_ti], refs[n_ti:n_ti + n_fi]
        n_in = n_ti + n_fi + (slab is not None and slab[0] is not None)
        t_out, a_out = refs[n_in:n_in + n_to], refs[n_in + n_to:]
        outs = fn(*[r[...] for r in t_in], *[r[...] for r in f_in])
        for r, v in zip(t_out, outs[:n_to]):
            r[...] = v.astype(r.dtype)
        if n_ao:
            i = pl.program_id(1)

            @pl.when(i == 0)
            def _():
                for r, v in zip(a_out, outs[n_to:]):
                    r[...] = v

            @pl.when(i > 0)
            def _():
                for r, v in zip(a_out, outs[n_to:]):
                    r[...] += v

    in_specs = [pl.BlockSpec((tm, bw), lambda j, i, base=base: (i, base + j)) for (_, bw, base) in tiled_ins]
    in_specs += [pl.BlockSpec(a.shape, lambda j, i, nd=a.ndim: (0,) * nd) for a in full_ins]
    out_specs = [pl.BlockSpec((tm, bw), lambda j, i: (i, j)) for (_, bw, _) in tiled_outs]
    out_shape = [jax.ShapeDtypeStruct((rows, w), dt) for (w, _, dt) in tiled_outs]
    extra, aliases = [], {}
    if slab is not None:
        arr, total, col0, width = slab
        out_specs.insert(0, pl.BlockSpec((pl.Element(tm), pl.Element(width)), lambda j, i: (i * tm, col0)))
        out_shape.insert(0, jax.ShapeDtypeStruct((rows, total), BF16))
        if arr is not None:
            extra, aliases = [arr], {n_ti + n_fi: 0}
            in_specs.append(ANY)
    out_specs += [pl.BlockSpec(s, lambda j, i: (0, 0)) for s in acc_shapes]
    out_shape += [jax.ShapeDtypeStruct(s, F32) for s in acc_shapes]
    return pl.pallas_call(
        body, name=name, grid=(ncol, rows // tm), in_specs=in_specs, out_specs=out_specs, out_shape=out_shape,
        input_output_aliases=aliases, compiler_params=_cparams(("parallel", "arbitrary"), vmem_mb),
    )(*[t[0] for t in tiled_ins], *full_ins, *extra)


def _f_rms(x, w):
    return x * lax.rsqrt(jnp.mean(x * x, axis=-1, keepdims=True) + EPS) * w


def _rms_fwd(name, x, w):
    rows = x.shape[0]
    return _tilemap(name, lambda xv, wv: (_f_rms(xv, wv),), [(x, D, 0)], [w], [(D, D, BF16)], [],
                    rows=rows, tm=_tile(rows, 256))[0]


def _rms_bwd(name, x, w, dh, dres):
    rows = x.shape[0]

    def fn(xv, dhv, drv, wv):
        _, vjp = jax.vjp(_f_rms, xv, wv)
        dx, dw = vjp(dhv.astype(F32))
        return dx + drv, dw

    return _tilemap(name, fn, [(x, D, 0), (dh, D, 0), (dres, D, 0)], [w], [(D, D, F32)], [(1, D)],
                    rows=rows, tm=_tile(rows, 256))


def _f_mixa_head(g, o, gw):
    mu = jnp.mean(o, axis=-1, keepdims=True)
    d = o - mu
    var = jnp.mean(d * d, axis=-1, keepdims=True)
    return g * jax.nn.sigmoid(g) * (d * lax.rsqrt(var + EPS) * gw)


def _heads(width):
    return [slice(h * width, (h + 1) * width) for h in range(HEADS)]


def _mixa_fwd(name, z, o, gw):
    rows = z.shape[0]

    def fn(g, ov, gwv):
        g, ov = g.astype(F32), ov.astype(F32)
        return (jnp.concatenate([_f_mixa_head(g[:, s], ov[:, s], gwv[:, s]) for s in _heads(DV)], axis=1),)

    return _tilemap(name, fn, [(z, D, Z_G), (o, D, 0)], [gw], [(D, D, BF16)], [], rows=rows, tm=_tile(rows, 256))[0]


def _mixa_bwd(name, z, o, gw, da, dz):
    rows = z.shape[0]

    def fn(g, ov, dav, gwv):
        g, ov, dav = g.astype(F32), ov.astype(F32), dav.astype(F32)
        parts = []
        for s in _heads(DV):
            _, vjp = jax.vjp(_f_mixa_head, g[:, s], ov[:, s], gwv[:, s])
            parts.append(vjp(dav[:, s]))
        return tuple(jnp.concatenate([p[i] for p in parts], axis=1) for i in range(3))

    return _tilemap(name, fn, [(z, D, Z_G), (o, D, 0), (da, D, 0)], [gw], [(D, D, BF16)], [(1, D)],
                    rows=rows, tm=_tile(rows, 256), slab=(dz, IN_COLS, Z_G * D, D))


def _f_merge(ga, gb, a, b):
    return jax.nn.sigmoid(ga) * a + jax.nn.sigmoid(gb) * b


def _merge_fwd(name, z, a, b):
    rows, bw = z.shape[0], 1024

    def fn(ga, gb, av, bv):
        return (_f_merge(ga.astype(F32), gb.astype(F32), av.astype(F32), bv.astype(F32)),)

    return _tilemap(name, fn, [(z, bw, Z_GA * 2), (z, bw, Z_GB * 2), (a, bw, 0), (b, bw, 0)], [], [(D, bw, BF16)], [],
                    rows=rows, tm=_tile(rows, 512), ncol=D // bw)[0]


def _merge_bwd(name, z, a, b, dm):
    rows = z.shape[0]

    def fn(ga, gb, av, bv, dmv):
        _, vjp = jax.vjp(_f_merge, ga.astype(F32), gb.astype(F32), av.astype(F32), bv.astype(F32))
        dga, dgb, dav, dbv = vjp(dmv.astype(F32))
        return jnp.concatenate([dga, dgb], axis=1), dav, dbv

    return _tilemap(name, fn, [(z, D, Z_GA), (z, D, Z_GB), (a, D, 0), (b, D, 0), (dm, D, 0)], [],
                    [(D, D, BF16)] * 2, [], rows=rows, tm=_tile(rows, 256), slab=(None, IN_COLS, Z_GA * D, 2 * D))


def _f_swiglu(a, c):
    return a * jax.nn.sigmoid(a) * c


def _ffn_in_swiglu(name, h, wsh):
    rows = h.shape[0]
    tm = _tile(rows, 512)

    def body(h_ref, wa_ref, wc_ref, a_ref, c_ref, f_ref):
        hv = h_ref[...]
        a = jnp.dot(hv, wa_ref[...], preferred_element_type=F32)
        c = jnp.dot(hv, wc_ref[...], preferred_element_type=F32)
        a_ref[...] = a.astype(a_ref.dtype)
        c_ref[...] = c.astype(c_ref.dtype)
        f_ref[...] = _f_swiglu(a, c).astype(f_ref.dtype)

    out = pl.BlockSpec((tm, SH_FF), lambda j, i: (i, j))
    return pl.pallas_call(
        body, name=name, grid=(NDEV // 2, rows // tm),
        in_specs=[pl.BlockSpec((tm, D), lambda j, i: (i, 0)), pl.BlockSpec((None, D, SH_FF), lambda j, i: (j, 0, 0)),
                  pl.BlockSpec((None, D, SH_FF), lambda j, i: (j + NDEV // 2, 0, 0))],
        out_specs=[out, out, out], out_shape=[jax.ShapeDtypeStruct((rows, DFF), BF16)] * 3,
        compiler_params=_cparams(("parallel", "arbitrary"), 56),
    )(h, wsh, wsh)


def _swiglu_bwd(name, a_part, c_part, df):
    rows = a_part.shape[0]

    def fn(a, c, dfv):
        _, vjp = jax.vjp(_f_swiglu, a.astype(F32), c.astype(F32))
        return (jnp.concatenate(vjp(dfv.astype(F32)), axis=1),)

    return _tilemap(name, fn, [(a_part, DFF, 0), (c_part, DFF, 0), (df, DFF, 0)], [], [(2 * DFF, 2 * DFF, BF16)], [],
                    rows=rows, tm=_tile(rows, 128))[0]


def _loss_bwd(name, x, w, target):
    rows = x.shape[0]

    def fn(xv, tv, wv):
        y, vjp = jax.vjp(_f_rms, xv, wv)
        e = y - tv
        dx, dw = vjp(e * (1.0 / D))
        per_row = jnp.mean(e * e, axis=-1, keepdims=True)
        return dx, dw, jnp.broadcast_to(0.5 * jnp.sum(per_row, axis=0, keepdims=True), (1, 128))

    return _tilemap(name, fn, [(x, D, 0), (target, D, 0)], [w], [(D, D, F32)], [(1, D), (1, 128)],
                    rows=rows, tm=_tile(rows, 256))


def _gelu(v):
    return 0.5 * v * (1.0 + lax.erf(v * (1.0 / math.sqrt(2.0))))


def _f_sgu_pre(su, sv, lw, lb):
    zv = _gelu(sv)
    mu = jnp.mean(zv, axis=-1, keepdims=True)
    d = zv - mu
    var = jnp.mean(d * d, axis=-1, keepdims=True)
    return _gelu(su), d * lax.rsqrt(var + EPS) * lw + lb


def _sgu_blocks(tm):
    return [slice(r * SLEN, (r + 1) * SLEN) for r in range(tm // SLEN)]


def _sgu_mix(wm, vnb, bexp, tm):
    rows = []
    for r in _sgu_blocks(tm):
        cols = [jnp.dot(wm[g], vnb[r, g * SCH:(g + 1) * SCH], preferred_element_type=F32) for g in range(GROUPS)]
        rows.append(jnp.concatenate(cols, axis=1) + bexp)
    return jnp.concatenate(rows, axis=0)


def _sgu_fwd(name, z, lw, lb, wm, bexp):
    rows = z.shape[0]
    tm = _tile(rows, 256)

    def fn(su, sv, lwv, lbv, wmv, bev):
        zu, vn = _f_sgu_pre(su.astype(F32), sv.astype(F32), lwv, lbv)
        return (zu * _sgu_mix(wmv, vn.astype(BF16), bev, tm),)

    return _tilemap(name, fn, [(z, D, Z_SU), (z, D, Z_SV)], [lw, lb, wm, bexp], [(D, D, BF16)], [], rows=rows, tm=tm)[0]


def _sgu_bwd(name, z, lw, lb, wm, wmt, bexp, maskf, db, dz):
    rows = z.shape[0]
    tm = _tile(rows, 256)

    def fn(su, sv, dbv, lwv, lbv, wmv, wmtv, bev, mkv):
        (zu, vn), vjp = jax.vjp(_f_sgu_pre, su.astype(F32), sv.astype(F32), lwv, lbv)
        vnb = vn.astype(BF16)
        dbv = dbv.astype(F32)
        dmixed = dbv * zu
        dzu = dbv * _sgu_mix(wmv, vnb, bev, tm)
        dmb = dmixed.astype(BF16)
        dvn_rows, dws, dbf = [], [None] * GROUPS, None
        for r in _sgu_blocks(tm):
            cols = []
            for g in range(GROUPS):
                cs = slice(g * SCH, (g + 1) * SCH)
                cols.append(jnp.dot(wmtv[g], dmb[r, cs], preferred_element_type=F32))
                dw = lax.dot_general(dmb[r, cs], vnb[r, cs], (((1,), (1,)), ((), ())), preferred_element_type=F32)
                dws[g] = dw if dws[g] is None else dws[g] + dw
            dvn_rows.append(jnp.concatenate(cols, axis=1))
            dbf = dmixed[r] if dbf is None else dbf + dmixed[r]
        dsu, dsv, dlw, dlb = vjp((dzu, jnp.concatenate(dvn_rows, axis=0)))
        dws_all = jnp.concatenate([dw * mkv for dw in dws], axis=0)
        return jnp.concatenate([dsu, dsv], axis=1), dws_all, dbf, dlw, dlb

    return _tilemap(name, fn, [(z, D, Z_SU), (z, D, Z_SV), (db, D, 0)], [lw, lb, wm, wmt, bexp, maskf],
                    [], [(GROUPS * SLEN, SLEN), (SLEN, D), (1, D), (1, D)], rows=rows, tm=tm,
                    slab=(dz, IN_COLS, Z_SU * D, 2 * D))


def _group_sums(name, dbf):
    def body(e_ref, x_ref, o_ref):
        o_ref[...] = lax.dot_general(e_ref[...], x_ref[...], (((1,), (1,)), ((), ())),
                                     precision=lax.Precision.HIGHEST, preferred_element_type=F32)

    ind = (jnp.arange(D)[None, :] // SCH == jnp.arange(GROUPS)[:, None]).astype(F32)
    return pl.pallas_call(body, name=name, out_shape=jax.ShapeDtypeStruct((GROUPS, SLEN), F32))(ind, dbf)


def _ret_consts(rows):
    hh = jnp.arange(HEADS, dtype=F32)
    log_g = jnp.log1p(-(2.0 ** (-5.0 - hh)))
    idx = jnp.arange(RL, dtype=F32)
    ci = jnp.arange(RL) // CHUNK
    mask = ci[None, :] <= ci[:, None]
    dm = jnp.where(mask[None], jnp.exp(log_g[:, None, None] * jnp.abs(idx[:, None] - idx[None, :])), 0.0)
    qd = jnp.broadcast_to(jnp.exp(log_g[:, None] * (idx[None, :] + 1.0))[:, :, None], (HEADS, RL, DK))
    kd = jnp.broadcast_to(jnp.exp(log_g[:, None] * (RL - 1.0 - idx[None, :]))[:, :, None], (HEADS, RL, DK))
    cd = jnp.broadcast_to(jnp.exp(log_g * RL)[:, None, None], (HEADS, DK, DV))
    half = DK // 2
    inv = ROPE_BASE ** (-jnp.arange(half, dtype=F32) / half)
    ang = jnp.arange(rows, dtype=jnp.int32).astype(F32)[:, None] * inv[None, :]
    cos, sin = jnp.cos(ang), jnp.sin(ang)
    return dict(dm=dm.astype(F32), qd=qd, kd=kd, cd=cd, cos=jnp.concatenate([cos, cos], axis=1),
                sin=jnp.concatenate([-sin, sin], axis=1))


def _rope(x, cos, sin_signed):
    return x * cos + pltpu.roll(x, DK // 2, 1) * sin_signed


def _rope_t(d, cos, sin_signed):
    return d * cos + pltpu.roll(d * sin_signed, DK // 2, 1)


def _dot(a, b, ca, cb):
    return lax.dot_general(a, b, (((ca,), (cb,)), ((), ())), preferred_element_type=F32)


QKV = 2 * HEADS * DK + HEADS * DV


def _ret_in_specs(blk):
    whole = lambda shape: pl.BlockSpec(shape, lambda n: (0,) * len(shape))
    return [pl.BlockSpec((RL, QKV), lambda n: (blk(n), 0)), pl.BlockSpec((RL, DK), lambda n: (blk(n), 0)),
            pl.BlockSpec((RL, DK), lambda n: (blk(n), 0)), whole((HEADS, RL, RL)), whole((HEADS, RL, DK)),
            whole((HEADS, RL, DK)), whole((HEADS, DK, DV))]


def _ret_head(ref, h):
    q0, k0, v0 = h * DK, HEADS * DK + h * DK, 2 * HEADS * DK + h * DV
    return (slice(None), slice(q0, q0 + DK)), (slice(None), slice(k0, k0 + DK)), (slice(None), slice(v0, v0 + DV))


def _ret_fwd(name, z, rc):
    rows = z.shape[0]
    nb = rows // RL

    def body(qkv_ref, cos_ref, sin_ref, dm_ref, qd_ref, kd_ref, cd_ref, o_ref, st_ref, state):
        @pl.when(pl.program_id(0) == 0)
        def _():
            state[...] = jnp.zeros_like(state)

        cos, sin = cos_ref[...], sin_ref[...]
        for h in range(HEADS):
            qs, ks, vs = _ret_head(qkv_ref, h)
            qf = _rope(qkv_ref[qs].astype(F32), cos, sin) * (DK ** -0.5)
            kf = _rope(qkv_ref[ks].astype(F32), cos, sin)
            vb = qkv_ref[vs]
            s0 = state[h]
            s0b = s0.astype(BF16)
            st_ref[h] = s0b
            sc = _dot(qf.astype(BF16), kf.astype(BF16), 1, 1) * dm_ref[h]
            o = _dot(sc.astype(BF16), vb, 1, 0) + _dot((qf * qd_ref[h]).astype(BF16), s0b, 1, 0)
            o_ref[:, h * DV:(h + 1) * DV] = o.astype(o_ref.dtype)
            state[h] = cd_ref[h] * s0 + _dot((kf * kd_ref[h]).astype(BF16), vb, 0, 0)

    return pl.pallas_call(
        body, name=name, grid=(nb,), in_specs=_ret_in_specs(lambda n: n),
        out_specs=[pl.BlockSpec((RL, HEADS * DV), lambda n: (n, 0)),
                   pl.BlockSpec((None, HEADS, DK, DV), lambda n: (n, 0, 0, 0))],
        out_shape=[jax.ShapeDtypeStruct((rows, HEADS * DV), BF16), jax.ShapeDtypeStruct((nb, HEADS, DK, DV), BF16)],
        scratch_shapes=[pltpu.VMEM((HEADS, DK, DV), F32)],
        compiler_params=_cparams(("arbitrary",), 40),
    )(z, rc["cos"], rc["sin"], rc["dm"], rc["qd"], rc["kd"], rc["cd"])


def _ret_bwd(name, z, rc, states, do, dz):
    rows = z.shape[0]
    nb = rows // RL
    rev = lambda n: nb - 1 - n

    def body(qkv_ref, cos_ref, sin_ref, dm_ref, qd_ref, kd_ref, cd_ref, st_ref, do_ref, dz_in, dqkv_ref, dstate):
        @pl.when(pl.program_id(0) == 0)
        def _():
            dstate[...] = jnp.zeros_like(dstate)

        cos, sin = cos_ref[...], sin_ref[...]
        for h in range(HEADS):
            qs, ks, vs = _ret_head(qkv_ref, h)
            dm, qd, kd = dm_ref[h], qd_ref[h], kd_ref[h]
            qf = _rope(qkv_ref[qs].astype(F32), cos, sin) * (DK ** -0.5)
            kf = _rope(qkv_ref[ks].astype(F32), cos, sin)
            qb, kb, vb, s0b = qf.astype(BF16), kf.astype(BF16), qkv_ref[vs], st_ref[h]
            dob = do_ref[:, h * DV:(h + 1) * DV]
            qsb, ksb = (qf * qd).astype(BF16), (kf * kd).astype(BF16)
            ds1 = dstate[h]
            ds1b = ds1.astype(BF16)
            scb = (_dot(qb, kb, 1, 1) * dm).astype(BF16)
            dscb = (_dot(dob, vb, 1, 1) * dm).astype(BF16)
            dq = _dot(dscb, kb, 1, 0) + _dot(dob, s0b, 1, 1) * qd
            dk = _dot(dscb, qb, 0, 0) + _dot(vb, ds1b, 1, 1) * kd
            dv = _dot(scb, dob, 0, 0) + _dot(ksb, ds1b, 1, 0)
            dqkv_ref[qs] = _rope_t(dq * (DK ** -0.5), cos, sin).astype(dqkv_ref.dtype)
            dqkv_ref[ks] = _rope_t(dk, cos, sin).astype(dqkv_ref.dtype)
            dqkv_ref[vs] = dv.astype(dqkv_ref.dtype)
            dstate[h] = cd_ref[h] * ds1 + _dot(qsb, dob, 0, 0)

    return pl.pallas_call(
        body, name=name, grid=(nb,),
        in_specs=_ret_in_specs(rev) + [pl.BlockSpec((None, HEADS, DK, DV), lambda n: (rev(n), 0, 0, 0)),
                                       pl.BlockSpec((RL, HEADS * DV), lambda n: (rev(n), 0)), ANY],
        out_specs=pl.BlockSpec((RL, QKV), lambda n: (rev(n), 0)),
        out_shape=jax.ShapeDtypeStruct(dz.shape, dz.dtype), input_output_aliases={9: 0},
        scratch_shapes=[pltpu.VMEM((HEADS, DK, DV), F32)],
        compiler_params=_cparams(("arbitrary",), 40),
    )(z, rc["cos"], rc["sin"], rc["dm"], rc["qd"], rc["kd"], rc["cd"], states, do, dz)


def _layer_fwd(l, x, p, rc):
    n = f"l{l}_"
    h = _rms_fwd(n + "rms1", x, p["norm_mix_w"])
    z = _mm_nn_sh(n + "in_proj", h, p["w_in"], out_dtype=BF16)
    o, states = _ret_fwd(n + "ret", z, rc)
    a_in = _mixa_fwd(n + "mixa", z, o, p["ret_gn_w"])
    b_in = _sgu_fwd(n + "sgu", z, p["sgu_ln_w"], p["sgu_ln_b"], p["wm"], p["bexp"])
    if "late" in p:
        p.update(p.pop("late")(b_in))
    br_a = _mm_nn(n + "ret_proj", a_in, p["ret_proj"], out_dtype=BF16, after=p.get("late_token"))
    br_b = _mm_nn(n + "sgu_proj", b_in, p["sgu_proj"], out_dtype=BF16)
    merged = _merge_fwd(n + "merge", z, br_a, br_b)
    x1 = _mm_nn(n + "out_proj", merged, p["w_out"], out_dtype=F32, res=x)
    h2 = _rms_fwd(n + "rms2", x1, p["norm_ffn_w"])
    ffn_a, ffn_c, f = _ffn_in_swiglu(n + "ffn_in", h2, p["w_ffn_in"])
    x2 = _mm_nn(n + "ffn_out", f, p["w_ffn_out"], out_dtype=F32, res=x1, tk=2 * SH_FF)
    saved = dict(x=x, h=h, z=z, o=o, states=states, a_in=a_in, b_in=b_in, br_a=br_a, br_b=br_b, merged=merged,
                 x1=x1, h2=h2, ffn_a=ffn_a, ffn_c=ffn_c, f=f)
    return x2, saved


def _layer_bwd_ffn(l, dx2, p, s, after):
    n = f"l{l}b_"
    df = _mm_nt(n + "d_f", dx2, p["w_ffn_out"], out_dtype=BF16, tn=SH_FF, after=after)
    g_ffn_out = _mm_tn(n + "g_ffn_out", s["f"], dx2, out_dtype=BF16, tm=SH_FF)
    dac = _swiglu_bwd(n + "swiglu", s["ffn_a"], s["ffn_c"], df)
    dh2 = _mm_nt_sh(n + "d_h2", dac, p["w_ffn_in"], out_dtype=BF16)
    g_ffn_in = _mm_tn_sh(n + "g_ffn_in", s["h2"], dac, SH_FF, out_dtype=BF16)
    dx1, g_nffn = _rms_bwd(n + "rms2", s["x1"], p["norm_ffn_w"], dh2, dx2)
    return dx1, dict(w_ffn_in=g_ffn_in, w_ffn_out=g_ffn_out.reshape(NDEV, -1, D)), dict(norm_ffn_w=g_nffn)


def _layer_bwd_mix(l, dx1, p, s, rc, after, send):
    n = f"l{l}b_"
    dmerged = _mm_nt(n + "d_merged", dx1, p["w_out"], out_dtype=BF16, after=after)
    g_out = _mm_tn(n + "g_out", s["merged"], dx1, out_dtype=BF16)
    dz, dbr_a, dbr_b = _merge_bwd(n + "merge", s["z"], s["br_a"], s["br_b"], dmerged)
    da_in = _mm_nt(n + "d_a_in", dbr_a, p["ret_proj"], out_dtype=BF16)
    g_ret_proj = _mm_tn(n + "g_ret_proj", s["a_in"], dbr_a, out_dtype=BF16, tk=2048)
    db_in = _mm_nt(n + "d_b_in", dbr_b, p["sgu_proj"], out_dtype=BF16)
    g_sgu_proj = _mm_tn(n + "g_sgu_proj", s["b_in"], dbr_b, out_dtype=BF16, tk=2048)
    dz, do, g_gn = _mixa_bwd(n + "mixa", s["z"], s["o"], p["ret_gn_w"], da_in, dz)
    dz = _ret_bwd(n + "ret", s["z"], rc, s["states"], do, dz)
    dz, g_ws, dbf, g_lw, g_lb = _sgu_bwd(n + "sgu", s["z"], p["sgu_ln_w"], p["sgu_ln_b"], p["wm"], p["wmt"],
                                          p["bexp"], p["maskf"], db_in, dz)
    g_bs = _group_sums(n + "g_bs", dbf)
    g_in = _mm_tn_sh(n + "g_in", s["h"], dz, SH_IN, out_dtype=BF16)
    token = send(dict(w_in=g_in, ret_proj=g_ret_proj.reshape(NDEV, -1, D), sgu_proj=g_sgu_proj.reshape(NDEV, -1, D),
                      w_out=g_out.reshape(NDEV, -1, D)))
    dh = _mm_nt_sh(n + "d_h", dz, p["w_in"], out_dtype=BF16, after=token)
    dx, g_nmix = _rms_bwd(n + "rms1", s["x"], p["norm_mix_w"], dh, dx1)
    small = dict(norm_mix_w=g_nmix, ret_gn_w=g_gn, sgu_ln_w=g_lw, sgu_ln_b=g_lb, sgu_w_s=g_ws, sgu_b_s=g_bs)
    return dx, small


def _slot(ref, p):
    return ref.at[4 * p[0] + 2 * p[1] + p[2]]


def _allgather(name, shards):
    n = len(shards)

    def body(*refs):
        ins, outs = refs[:n], refs[n:2 * n]
        send_sems, recv_sems, local_sems = refs[2 * n:]
        x, y, c = lax.axis_index("x"), lax.axis_index("y"), lax.axis_index("c")
        me, sibling = (x, y, c), (x, y, 1 - c)
        chips = [(1 - x, y), (x, 1 - y), (1 - x, 1 - y)]

        def copy(a, k, block, to, src=None):
            return pltpu.make_async_remote_copy(
                src_ref=_slot(outs[a], block) if src is None else src, dst_ref=_slot(outs[a], block),
                send_sem=send_sems.at[a, k], recv_sem=recv_sems.at[a, k], device_id=to, device_id_type=MESH_T)

        mine = [pltpu.make_async_copy(ins[a], _slot(outs[a], me), local_sems.at[a]) for a in range(n)]
        for cp in mine:
            cp.start()
        first = []
        for a in range(n):
            first.append(copy(a, 0, me, sibling, src=ins[a]))
            first += [copy(a, 1 + j, me, (*chip, c), src=ins[a]) for j, chip in enumerate(chips)]
        for cp in first:
            cp.start()
        passed = []
        for j, chip in enumerate(chips):
            for a in range(n):
                copy(a, 1 + j, (*chip, c), me).wait_recv()
                cp = copy(a, 4 + j, (*chip, c), sibling)
                cp.start()
                passed.append(cp)
        for a in range(n):
            copy(a, 0, sibling, me).wait_recv()
            for j, chip in enumerate(chips):
                copy(a, 4 + j, (*chip, 1 - c), me).wait_recv()
        for cp in first + passed:
            cp.wait_send()
        for cp in mine:
            cp.wait()

    return pl.pallas_call(
        body, name=name, in_specs=[ANY] * n, out_specs=[ANY] * n,
        out_shape=[jax.ShapeDtypeStruct((NDEV,) + s.shape, s.dtype) for s in shards],
        scratch_shapes=[pltpu.SemaphoreType.DMA((n, 7)), pltpu.SemaphoreType.DMA((n, 7)), pltpu.SemaphoreType.DMA((n,))],
    )(*shards)


def _peers(x, y, c):
    return [((1 - x) if fx else x, (1 - y) if fy else y, (1 - c) if fc else c)
            for fx in (0, 1) for fy in (0, 1) for fc in (0, 1) if fx + fy + fc]


def _exchange_copy(whole, src_ref, land_ref, send_sems, recv_sems, a, k, peer, me, landed):
    return pltpu.make_async_remote_copy(
        src_ref=src_ref if whole else _slot(src_ref, peer), dst_ref=_slot(land_ref, peer if landed else me),
        send_sem=send_sems.at[7 * a + k], recv_sem=recv_sems.at[7 * a + k], device_id=peer, device_id_type=MESH_T)


def _exchange_start(name, srcs, lands, whole, after):
    n = len(srcs)

    def body(*refs):
        src_refs, land_refs = refs[:n], refs[n:2 * n]
        send_sems, recv_sems = refs[2 * n + 1], refs[2 * n + 2]
        token = refs[-1]
        x, y, c = lax.axis_index("x"), lax.axis_index("y"), lax.axis_index("c")
        for a in range(n):
            for k, peer in enumerate(_peers(x, y, c)):
                _exchange_copy(whole, src_refs[a], land_refs[a], send_sems, recv_sems, a, k, peer, (x, y, c), False).start()
        token[...] = jnp.zeros_like(token)

    hbm = lambda t: pltpu.HBM(t.shape, t.dtype)
    out = pl.pallas_call(
        body, name=name, in_specs=[HBM_SPEC] * (2 * n) + [ANY],
        out_specs=(SEM_SPEC, SEM_SPEC, *[HBM_SPEC] * (2 * n), pl.BlockSpec(memory_space=pltpu.VMEM)),
        out_shape=(pltpu.SemaphoreType.DMA((7 * n,)), pltpu.SemaphoreType.DMA((7 * n,)), *[hbm(t) for t in srcs],
                   *[hbm(t) for t in lands], jax.ShapeDtypeStruct((8, 128), F32)),
        input_output_aliases={i: 2 + i for i in range(2 * n)},
        compiler_params=pltpu.CompilerParams(has_side_effects=pltpu.SideEffectType.DATAFLOW_SIDE_EFFECTING),
    )(*[pltpu.with_memory_space_constraint(t, pltpu.HBM) for t in (*srcs, *lands)], after)
    return dict(send=out[0], recv=out[1], srcs=out[2:2 + n], lands=out[2 + n:2 + 2 * n], token=out[-1])


def _exchange_wait(name, started, whole, after):
    n = len(started["srcs"])

    def body(*refs):
        src_refs, land_refs = refs[:n], refs[n:2 * n]
        send_sems, recv_sems = refs[2 * n], refs[2 * n + 1]
        x, y, c = lax.axis_index("x"), lax.axis_index("y"), lax.axis_index("c")
        for a in range(n):
            for k, peer in enumerate(_peers(x, y, c)):
                cp = _exchange_copy(whole, src_refs[a], land_refs[a], send_sems, recv_sems, a, k, peer, (x, y, c), True)
                cp.wait_send()
                cp.wait_recv()

    hbm = lambda t: pltpu.HBM(t.shape, t.dtype)
    out = pl.pallas_call(
        body, name=name, in_specs=[HBM_SPEC] * (2 * n) + [SEM_SPEC, SEM_SPEC, ANY], out_specs=[HBM_SPEC] * (2 * n),
        out_shape=[hbm(t) for t in (*started["srcs"], *started["lands"])],
        input_output_aliases={i: i for i in range(2 * n)},
        compiler_params=pltpu.CompilerParams(has_side_effects=pltpu.SideEffectType.DATAFLOW_SIDE_EFFECTING),
    )(*started["srcs"], *started["lands"], started["send"], started["recv"], after)
    return out[n:]


def _adamw(w, g, m, v):
    m = ADAM_B1 * m + (1.0 - ADAM_B1) * g
    v = ADAM_B2 * v + (1.0 - ADAM_B2) * (g * g)
    m_hat = m / (1.0 - ADAM_B1 ** ADAM_STEP)
    v_hat = v / (1.0 - ADAM_B2 ** ADAM_STEP)
    return -ADAM_LR * (m_hat / (jnp.sqrt(v_hat) + ADAM_EPS) + ADAM_WD * w), m, v


def _sum_and_adamw(name, recv, w, m, v, layer, prev):
    _, r, c = recv.shape
    tr = max(t for t in range(16, r + 1, 16) if r % t == 0 and t * c <= ADAM_TILE_ELEMS)

    def body(recv_ref, w_ref, m_ref, v_ref, *rest):
        g_ref, d_ref, m_out, v_out = rest[4:]
        g = recv_ref[0].astype(F32)
        for s in range(1, NDEV):
            g = g + recv_ref[s].astype(F32)
        delta, m_new, v_new = _adamw(w_ref[...], g, m_ref[...], v_ref[...])
        g_ref[...] = g
        d_ref[...] = delta
        m_out[...] = m_new
        v_out[...] = v_new

    lay = pl.BlockSpec((None, tr, c), lambda i: (layer, i, 0))
    return pl.pallas_call(
        body, name=name, grid=(r // tr,),
        in_specs=[pl.BlockSpec((NDEV, tr, c), lambda i: (0, i, 0)), lay, lay, lay] + [ANY] * 4,
        out_specs=[lay] * 4, out_shape=[jax.ShapeDtypeStruct(w.shape, F32)] * 4,
        input_output_aliases={4: 0, 5: 1, 6: 2, 7: 3},
        compiler_params=_cparams(("arbitrary",), 48),
    )(recv, w, m, v, *prev)


def _small_sum_and_adamw(name, recv, w, m, v):
    r = w.shape[0]

    def body(recv_ref, w_ref, m_ref, v_ref, g_ref, d_ref, m_out, v_out):
        g = recv_ref[0]
        for s in range(1, NDEV):
            g = g + recv_ref[s]
        delta, m_new, v_new = _adamw(w_ref[...], g, m_ref[...], v_ref[...])
        g_ref[...] = g
        d_ref[...] = delta
        m_out[...] = m_new
        v_out[...] = v_new

    return pl.pallas_call(body, name=name, out_shape=[jax.ShapeDtypeStruct((r, 128), F32)] * 4,
                          compiler_params=_cparams(None, 48))(recv, w, m, v)


BIG = ("w_in", "ret_proj", "sgu_proj", "w_out", "w_ffn_in", "w_ffn_out")
SMALL = ("norm_mix_w", "ret_gn_w", "sgu_ln_w", "sgu_ln_b", "sgu_w_s", "sgu_b_s", "norm_ffn_w", "final_norm_w")
WEIGHTS = ("norm_mix_w", "w_in", "ret_gn_w", "ret_proj", "sgu_ln_w", "sgu_ln_b", "sgu_w_s", "sgu_b_s", "sgu_proj",
           "w_out", "norm_ffn_w", "w_ffn_in", "w_ffn_out", "final_norm_w")


def _pack_small(d):
    return jnp.concatenate([d[k].reshape(-1, 128) for k in SMALL], axis=0)


def _unpack_small(packed, like):
    out, r0 = {}, 0
    for k in SMALL:
        r = like[k].size // 128
        out[k] = packed[r0:r0 + r].reshape(like[k].shape)
        r0 += r
    return out


def _gathered(keys, arrays):
    return {k: a if k in ("w_in", "w_ffn_in") else a.reshape(-1, D) for k, a in zip(keys, arrays)}


def _layer_params(l, w, gathered, mask):
    p = _gathered(BIG, gathered)
    for k in ("norm_mix_w", "ret_gn_w", "sgu_ln_w", "sgu_ln_b", "norm_ffn_w"):
        p[k] = w[k][l][None, :]
    wm = jnp.where(mask[None], w["sgu_w_s"][l], 0.0)
    p["wm"], p["wmt"] = wm.astype(BF16), jnp.swapaxes(wm, 1, 2).astype(BF16)
    p["bexp"] = jnp.repeat(w["sgu_b_s"][l].T, SCH, axis=1)
    p["maskf"] = mask.astype(F32)
    return p


def _step(w, m, v, x, target):
    depth = w["w_in"].shape[0]
    rc = _ret_consts(x.shape[0])
    pos = jnp.arange(SLEN)
    mask = (pos[None, :] // CHUNK) <= (pos[:, None] // CHUNK)
    shards = [[w[k][l].astype(BF16) for k in BIG] for l in range(depth)]
    me = 4 * lax.axis_index("x") + 2 * lax.axis_index("y") + lax.axis_index("c")

    def lands_of(ts):
        return [lax.dynamic_update_index_in_dim(lax.empty((NDEV,) + t.shape, t.dtype), t, me, 0) for t in ts]

    w_in0 = _allgather("gather_l0_in", shards[0][:1])[0]
    rest0 = _exchange_start("gather_start_l0", shards[0][1:], lands_of(shards[0][1:]), True, w_in0)
    gathered = [w_in0] + list(rest0["lands"])
    saved, layers, next_gather = [], [], {}

    def late0(after):
        got = _gathered(BIG[1:], _exchange_wait("gather_wait_l0", rest0, True, after))
        next_gather[1] = _exchange_start("gather_start_l1", shards[1], lands_of(shards[1]), True, got["ret_proj"])
        return {**got, "late_token": next_gather[1]["token"]}

    for l in range(depth):
        p = _layer_params(l, w, gathered, mask)
        if l == 0:
            p["norm_mix_w"] = p["norm_mix_w"] + rest0["token"][0, 0]
            p["late"] = late0
        elif l + 1 < depth:
            next_gather[l + 1] = _exchange_start(f"gather_start_l{l + 1}", shards[l + 1], lands_of(shards[l + 1]), True,
                                                 gathered[0])
            p["norm_mix_w"] = p["norm_mix_w"] + next_gather[l + 1]["token"][0, 0]
        x, s = _layer_fwd(l, x, p, rc)
        started = next_gather.get(l + 1)
        if l + 1 < depth:
            gathered = _exchange_wait(f"gather_wait_l{l + 1}", started, True, x)
        saved.append(s)
        layers.append(p)

    dx, g_final, loss_row = _loss_bwd("loss", x, w["final_norm_w"][None, :], target)
    loss = lax.psum(loss_row[0, 0], ("x", "y", "c"))

    res = {k: [lax.empty(w[k].shape, F32) for _ in range(4)] for k in BIG}
    small = {k: [None] * depth for k in SMALL if k != "final_norm_w"}

    def scatter_start(tag, big, after):
        keys = [k for k in BIG if k in big]
        parts = [big[k] for k in keys]
        lands = [lax.dynamic_update_index_in_dim(lax.empty(t.shape, t.dtype), lax.dynamic_index_in_dim(t, me, 0), me, 0)
                 for t in parts]
        return keys, _exchange_start(f"scatter_start_{tag}", parts, lands, False, after)

    def finish(l, pending, after):
        for tag, (keys, started) in pending:
            recv = _exchange_wait(f"scatter_wait_{tag}", started, False, after)
            for k, rv in zip(keys, recv):
                res[k] = _sum_and_adamw(f"adamw_{k}_l{l}", rv, w[k], m[k], v[k], l, res[k])

    pending, token = None, loss_row
    for l in reversed(range(depth)):
        dx1, big_ffn, sm_ffn = _layer_bwd_ffn(l, dx, layers[l], saved[l], token)
        now = [(f"ffn_l{l}", scatter_start(f"ffn_l{l}", big_ffn, dx1))]

        def send(big_mix, l=l, now=now, dx1=dx1):
            now.append((f"mix_l{l}", scatter_start(f"mix_l{l}", big_mix, dx1)))
            return now[1][1][1]["token"]

        dx, sm_mix = _layer_bwd_mix(l, dx1, layers[l], saved[l], rc, now[0][1][1]["token"], send)
        token = now[1][1][1]["token"]
        if pending is not None:
            finish(l + 1, pending, dx)
        pending = now
        for k, g in {**sm_ffn, **sm_mix}.items():
            small[k][l] = g
    finish(0, pending, dx)

    g_small = {k: jnp.stack(small[k]).reshape(w[k].shape) for k in small}
    g_small["final_norm_w"] = g_final.reshape(w["final_norm_w"].shape)
    recv = _allgather("gather_small", [_pack_small(g_small)])[0]
    sres = _small_sum_and_adamw("adamw_small", recv, _pack_small(w), _pack_small(m), _pack_small(v))
    sres = [_unpack_small(a, w) for a in sres]

    outs = []
    for i in range(4):
        outs.append([sres[i][k] if k in SMALL else res[k][i] for k in WEIGHTS])
    return loss, dx, outs


def kernel(x, norm_mix_w, w_in, ret_gn_w, ret_proj, sgu_ln_w, sgu_ln_b, sgu_w_s, sgu_b_s, sgu_proj, w_out, norm_ffn_w, w_ffn_in, w_ffn_out, final_norm_w, loss_target, m_norm_mix_w, m_w_in, m_ret_gn_w, m_ret_proj, m_sgu_ln_w, m_sgu_ln_b, m_sgu_w_s, m_sgu_b_s, m_sgu_proj, m_w_out, m_norm_ffn_w, m_w_ffn_in, m_w_ffn_out, m_final_norm_w, v_norm_mix_w, v_w_in, v_ret_gn_w, v_ret_proj, v_sgu_ln_w, v_sgu_ln_b, v_sgu_w_s, v_sgu_b_s, v_sgu_proj, v_w_out, v_norm_ffn_w, v_w_ffn_in, v_w_ffn_out, v_final_norm_w):
    w = dict(zip(WEIGHTS, (norm_mix_w, w_in, ret_gn_w, ret_proj, sgu_ln_w, sgu_ln_b, sgu_w_s, sgu_b_s, sgu_proj, w_out,
                           norm_ffn_w, w_ffn_in, w_ffn_out, final_norm_w)))
    m = dict(zip(WEIGHTS, (m_norm_mix_w, m_w_in, m_ret_gn_w, m_ret_proj, m_sgu_ln_w, m_sgu_ln_b, m_sgu_w_s, m_sgu_b_s,
                           m_sgu_proj, m_w_out, m_norm_ffn_w, m_w_ffn_in, m_w_ffn_out, m_final_norm_w)))
    v = dict(zip(WEIGHTS, (v_norm_mix_w, v_w_in, v_ret_gn_w, v_ret_proj, v_sgu_ln_w, v_sgu_ln_b, v_sgu_w_s, v_sgu_b_s,
                           v_sgu_proj, v_w_out, v_norm_ffn_w, v_w_ffn_in, v_w_ffn_out, v_final_norm_w)))
    loss, dx, (grads, deltas, new_m, new_v) = _step(w, m, v, x.reshape(x.shape[1:]), loss_target.reshape(x.shape[1:]))
    return (loss, dx[None], *grads, *deltas, *new_m, *new_v)
```

```python
import functools
import math

import jax
import jax.numpy as jnp
from jax import lax
from jax.experimental import pallas as pl
from jax.experimental.pallas import tpu as pltpu

F32, BF16 = jnp.float32, jnp.bfloat16

D = 2048
HEADS, DK, DV = 8, 128, 256
GROUPS, SLEN = 8, 128
SCH = D // GROUPS
DFF = 5632
CHUNK = 64
IN_COLS = 14336
NDEV = 8
SH_IN = IN_COLS // NDEV
SH_FF = 2 * DFF // NDEV
EPS = 1e-6
ROPE_BASE = 10000.0
RL = 256
Z_G, Z_SU, Z_SV, Z_GA, Z_GB = 2, 3, 4, 5, 6

ADAM_TILE_ELEMS = 256 * 1024
ADAM_LR, ADAM_B1, ADAM_B2, ADAM_EPS, ADAM_WD, ADAM_STEP = 0.001, 0.9, 0.999, 1e-08, 0.01, 10

VMEM_BYTES_V7X = 64 << 20
MESH_T = pl.DeviceIdType.MESH
ANY = pl.BlockSpec(memory_space=pl.ANY)
HBM_SPEC = pl.BlockSpec(memory_space=pltpu.HBM)
SEM_SPEC = pl.BlockSpec(memory_space=pltpu.SEMAPHORE)


def _cparams(sem, vmem_mb):
    assert (vmem_mb << 20) < VMEM_BYTES_V7X
    return pltpu.CompilerParams(dimension_semantics=sem, vmem_limit_bytes=vmem_mb << 20)


def _mm(name, a, b, *, grid, a_spec, b_spec, o_spec, o_block, out_shape, out_dtype, contract, res=None, after=None,
        split=1, vmem_mb=48):
    nk = grid[2]
    dn = (contract, ((), ()))
    has_res = res is not None

    def body(*refs):
        a_ref, b_ref = refs[:2]
        r_ref = refs[2] if has_res else None
        o_ref, acc_ref = refs[-2:]
        if split == 1:
            part = lax.dot_general(a_ref[...].astype(BF16), b_ref[...].astype(BF16), dn, preferred_element_type=F32)
        else:
            kp = a_ref.shape[1] // split
            part = sum(lax.dot_general(a_ref[:, s * kp:(s + 1) * kp].astype(BF16), b_ref[s].astype(BF16), dn,
                                       preferred_element_type=F32) for s in range(split))

        def finish(val):
            if has_res:
                val = val + r_ref[...].astype(F32)
            o_ref[...] = val.astype(o_ref.dtype)

        if nk == 1:
            finish(part)
        else:
            k = pl.program_id(2)

            @pl.when(k == 0)
            def _():
                acc_ref[...] = part

            @pl.when(k > 0)
            def _():
                acc_ref[...] += part

            @pl.when(k == nk - 1)
            def _():
                finish(acc_ref[...])

    ins = [a, b] + ([res] if has_res else []) + ([after] if after is not None else [])
    in_specs = [a_spec, b_spec] + ([o_spec] if has_res else []) + ([ANY] if after is not None else [])
    return pl.pallas_call(
        body, name=name, grid=grid, in_specs=in_specs, out_specs=o_spec,
        out_shape=jax.ShapeDtypeStruct(out_shape, out_dtype),
        scratch_shapes=[pltpu.VMEM(o_block if nk > 1 else (8, 128), F32)],
        compiler_params=_cparams(("parallel", "parallel", "arbitrary"), vmem_mb),
    )(*ins)


def _tile(n, want):
    t = min(n, want)
    assert n % t == 0, (n, want)
    return t


def _mm_nn(name, a, b, *, out_dtype, tm=1024, tn=1024, tk=None, res=None, after=None):
    (m, k), n = a.shape, b.shape[1]
    tm, tn, tk = _tile(m, tm), _tile(n, tn), k if tk is None else tk
    return _mm(name, a, b, grid=(m // tm, n // tn, k // tk),
               a_spec=pl.BlockSpec((tm, tk), lambda i, j, kk: (i, kk)),
               b_spec=pl.BlockSpec((tk, tn), lambda i, j, kk: (kk, j)),
               o_spec=pl.BlockSpec((tm, tn), lambda i, j, kk: (i, j)), o_block=(tm, tn),
               out_shape=(m, n), out_dtype=out_dtype, contract=((1,), (0,)), res=res, after=after)


def _mm_nt(name, a, b, *, out_dtype, tm=1024, tn=1024, tk=None, after=None):
    (m, k), n = a.shape, b.shape[0]
    tm, tn, tk = _tile(m, tm), _tile(n, tn), k if tk is None else tk
    return _mm(name, a, b, grid=(m // tm, n // tn, k // tk),
               a_spec=pl.BlockSpec((tm, tk), lambda i, j, kk: (i, kk)),
               b_spec=pl.BlockSpec((tn, tk), lambda i, j, kk: (j, kk)),
               o_spec=pl.BlockSpec((tm, tn), lambda i, j, kk: (i, j)), o_block=(tm, tn),
               out_shape=(m, n), out_dtype=out_dtype, contract=((1,), (1,)), after=after)


def _mm_tn(name, a, b, *, out_dtype, tm=1024, tn=1024, tk=1024):
    (k, m), n = a.shape, b.shape[1]
    tm, tn, tk = _tile(m, tm), _tile(n, tn), _tile(k, tk)
    return _mm(name, a, b, grid=(m // tm, n // tn, k // tk),
               a_spec=pl.BlockSpec((tk, tm), lambda i, j, kk: (kk, i)),
               b_spec=pl.BlockSpec((tk, tn), lambda i, j, kk: (kk, j)),
               o_spec=pl.BlockSpec((tm, tn), lambda i, j, kk: (i, j)), o_block=(tm, tn),
               out_shape=(m, n), out_dtype=out_dtype, contract=((0,), (0,)))


def _mm_nn_sh(name, a, bsh, *, out_dtype, tm=1024):
    (m, k), ns = a.shape, bsh.shape[2]
    tm = _tile(m, tm)
    return _mm(name, a, bsh, grid=(m // tm, NDEV, 1),
               a_spec=pl.BlockSpec((tm, k), lambda i, j, kk: (i, 0)),
               b_spec=pl.BlockSpec((None, k, ns), lambda i, j, kk: (j, 0, 0)),
               o_spec=pl.BlockSpec((tm, ns), lambda i, j, kk: (i, j)), o_block=(tm, ns),
               out_shape=(m, NDEV * ns), out_dtype=out_dtype, contract=((1,), (0,)), vmem_mb=56)


def _mm_nt_sh(name, a, bsh, *, out_dtype, tm=1024, tn=1024, after=None):
    m, (_, n, ns) = a.shape[0], bsh.shape
    tm, tn = _tile(m, tm), _tile(n, tn)
    return _mm(name, a, bsh, grid=(m // tm, n // tn, NDEV // 2),
               a_spec=pl.BlockSpec((tm, 2 * ns), lambda i, j, kk: (i, kk)),
               b_spec=pl.BlockSpec((2, tn, ns), lambda i, j, kk: (kk, j, 0)),
               o_spec=pl.BlockSpec((tm, tn), lambda i, j, kk: (i, j)), o_block=(tm, tn),
               out_shape=(m, n), out_dtype=out_dtype, contract=((1,), (1,)), after=after, split=2)


def _mm_tn_sh(name, a, b, ns, *, out_dtype, tm=1024, tk=2048):
    k, m = a.shape
    tm, tk = _tile(m, tm), _tile(k, tk)
    return _mm(name, a, b, grid=(m // tm, NDEV, k // tk),
               a_spec=pl.BlockSpec((tk, tm), lambda i, j, kk: (kk, i)),
               b_spec=pl.BlockSpec((tk, ns), lambda i, j, kk: (kk, j)),
               o_spec=pl.BlockSpec((None, tm, ns), lambda i, j, kk: (j, i, 0)), o_block=(tm, ns),
               out_shape=(NDEV, m, ns), out_dtype=out_dtype, contract=((0,), (0,)), vmem_mb=56)


def _tilemap(name, fn, tiled_ins, full_ins, tiled_outs, acc_shapes, *, rows, tm, ncol=1, vmem_mb=48, slab=None):
    n_ti, n_fi, n_to, n_ao = len(tiled_ins), len(full_ins), len(tiled_outs), len(acc_shapes)
    assert ncol == 1 or (n_ao == 0 and slab is None)
    if slab is not None:
        n_to += 1

    def body(*refs):
        t_in, f_in = refs[:n_ti], refs[n_ti:n_ti + n_fi]
        n_in = n_ti + n_fi + (slab is not None and slab[0] is not None)
        t_out, a_out = refs[n_in:n_in + n_to], refs[n_in + n_to:]
        outs = fn(*[r[...] for r in t_in], *[r[...] for r in f_in])
        for r, v in zip(t_out, outs[:n_to]):
            r[...] = v.astype(r.dtype)
        if n_ao:
            i = pl.program_id(1)

            @pl.when(i == 0)
            def _():
                for r, v in zip(a_out, outs[n_to:]):
                    r[...] = v

            @pl.when(i > 0)
            def _():
                for r, v in zip(a_out, outs[n_to:]):
                    r[...] += v

    in_specs = [pl.BlockSpec((tm, bw), lambda j, i, base=base: (i, base + j)) for (_, bw, base) in tiled_ins]
    in_specs += [pl.BlockSpec(a.shape, lambda j, i, nd=a.ndim: (0,) * nd) for a in full_ins]
    out_specs = [pl.BlockSpec((tm, bw), lambda j, i: (i, j)) for (_, bw, _) in tiled_outs]
    out_shape = [jax.ShapeDtypeStruct((rows, w), dt) for (w, _, dt) in tiled_outs]
    extra, aliases = [], {}
    if slab is not None:
        arr, total, col0, width = slab
        out_specs.insert(0, pl.BlockSpec((pl.Element(tm), pl.Element(width)), lambda j, i: (i * tm, col0)))
        out_shape.insert(0, jax.ShapeDtypeStruct((rows, total), BF16))
        if arr is not None:
            extra, aliases = [arr], {n_ti + n_fi: 0}
            in_specs.append(ANY)
    out_specs += [pl.BlockSpec(s, lambda j, i: (0, 0)) for s in acc_shapes]
    out_shape += [jax.ShapeDtypeStruct(s, F32) for s in acc_shapes]
    return pl.pallas_call(
        body, name=name, grid=(ncol, rows // tm), in_specs=in_specs, out_specs=out_specs, out_shape=out_shape,
        input_output_aliases=aliases, compiler_params=_cparams(("parallel", "arbitrary"), vmem_mb),
    )(*[t[0] for t in tiled_ins], *full_ins, *extra)


def _f_rms(x, w):
    return x * lax.rsqrt(jnp.mean(x * x, axis=-1, keepdims=True) + EPS) * w


def _rms_fwd(name, x, w):
    rows = x.shape[0]
    return _tilemap(name, lambda xv, wv: (_f_rms(xv, wv),), [(x, D, 0)], [w], [(D, D, BF16)], [],
                    rows=rows, tm=_tile(rows, 256))[0]


def _rms_bwd(name, x, w, dh, dres):
    rows = x.shape[0]

    def fn(xv, dhv, drv, wv):
        _, vjp = jax.vjp(_f_rms, xv, wv)
        dx, dw = vjp(dhv.astype(F32))
        return dx + drv, dw

    return _tilemap(name, fn, [(x, D, 0), (dh, D, 0), (dres, D, 0)], [w], [(D, D, F32)], [(1, D)],
                    rows=rows, tm=_tile(rows, 256))


def _f_mixa_head(g, o, gw):
    mu = jnp.mean(o, axis=-1, keepdims=True)
    d = o - mu
    var = jnp.mean(d * d, axis=-1, keepdims=True)
    return g * jax.nn.sigmoid(g) * (d * lax.rsqrt(var + EPS) * gw)


def _heads(width):
    return [slice(h * width, (h + 1) * width) for h in range(HEADS)]


def _mixa_fwd(name, z, o, gw):
    rows = z.shape[0]

    def fn(g, ov, gwv):
        g, ov = g.astype(F32), ov.astype(F32)
        return (jnp.concatenate([_f_mixa_head(g[:, s], ov[:, s], gwv[:, s]) for s in _heads(DV)], axis=1),)

    return _tilemap(name, fn, [(z, D, Z_G), (o, D, 0)], [gw], [(D, D, BF16)], [], rows=rows, tm=_tile(rows, 256))[0]


def _mixa_bwd(name, z, o, gw, da, dz):
    rows = z.shape[0]

    def fn(g, ov, dav, gwv):
        g, ov, dav = g.astype(F32), ov.astype(F32), dav.astype(F32)
        parts = []
        for s in _heads(DV):
            _, vjp = jax.vjp(_f_mixa_head, g[:, s], ov[:, s], gwv[:, s])
            parts.append(vjp(dav[:, s]))
        return tuple(jnp.concatenate([p[i] for p in parts], axis=1) for i in range(3))

    return _tilemap(name, fn, [(z, D, Z_G), (o, D, 0), (da, D, 0)], [gw], [(D, D, BF16)], [(1, D)],
                    rows=rows, tm=_tile(rows, 256), slab=(dz, IN_COLS, Z_G * D, D))


def _f_merge(ga, gb, a, b):
    return jax.nn.sigmoid(ga) * a + jax.nn.sigmoid(gb) * b


def _merge_fwd(name, z, a, b):
    rows, bw = z.shape[0], 1024

    def fn(ga, gb, av, bv):
        return (_f_merge(ga.astype(F32), gb.astype(F32), av.astype(F32), bv.astype(F32)),)

    return _tilemap(name, fn, [(z, bw, Z_GA * 2), (z, bw, Z_GB * 2), (a, bw, 0), (b, bw, 0)], [], [(D, bw, BF16)], [],
                    rows=rows, tm=_tile(rows, 512), ncol=D // bw)[0]


def _merge_bwd(name, z, a, b, dm):
    rows = z.shape[0]

    def fn(ga, gb, av, bv, dmv):
        _, vjp = jax.vjp(_f_merge, ga.astype(F32), gb.astype(F32), av.astype(F32), bv.astype(F32))
        dga, dgb, dav, dbv = vjp(dmv.astype(F32))
        return jnp.concatenate([dga, dgb], axis=1), dav, dbv

    return _tilemap(name, fn, [(z, D, Z_GA), (z, D, Z_GB), (a, D, 0), (b, D, 0), (dm, D, 0)], [],
                    [(D, D, BF16)] * 2, [], rows=rows, tm=_tile(rows, 256), slab=(None, IN_COLS, Z_GA * D, 2 * D))


def _f_swiglu(a, c):
    return a * jax.nn.sigmoid(a) * c


def _ffn_in_swiglu(name, h, wsh):
    rows = h.shape[0]
    tm = _tile(rows, 512)

    def body(h_ref, wa_ref, wc_ref, a_ref, c_ref, f_ref):
        hv = h_ref[...]
        a = jnp.dot(hv, wa_ref[...], preferred_element_type=F32)
        c = jnp.dot(hv, wc_ref[...], preferred_element_type=F32)
        a_ref[...] = a.astype(a_ref.dtype)
        c_ref[...] = c.astype(c_ref.dtype)
        f_ref[...] = _f_swiglu(a, c).astype(f_ref.dtype)

    out = pl.BlockSpec((tm, SH_FF), lambda j, i: (i, j))
    return pl.pallas_call(
        body, name=name, grid=(NDEV // 2, rows // tm),
        in_specs=[pl.BlockSpec((tm, D), lambda j, i: (i, 0)), pl.BlockSpec((None, D, SH_FF), lambda j, i: (j, 0, 0)),
                  pl.BlockSpec((None, D, SH_FF), lambda j, i: (j + NDEV // 2, 0, 0))],
        out_specs=[out, out, out], out_shape=[jax.ShapeDtypeStruct((rows, DFF), BF16)] * 3,
        compiler_params=_cparams(("parallel", "arbitrary"), 56),
    )(h, wsh, wsh)


def _swiglu_bwd(name, a_part, c_part, df):
    rows = a_part.shape[0]

    def fn(a, c, dfv):
        _, vjp = jax.vjp(_f_swiglu, a.astype(F32), c.astype(F32))
        return (jnp.concatenate(vjp(dfv.astype(F32)), axis=1),)

    return _tilemap(name, fn, [(a_part, DFF, 0), (c_part, DFF, 0), (df, DFF, 0)], [], [(2 * DFF, 2 * DFF, BF16)], [],
                    rows=rows, tm=_tile(rows, 128))[0]


def _loss_bwd(name, x, w, target):
    rows = x.shape[0]

    def fn(xv, tv, wv):
        y, vjp = jax.vjp(_f_rms, xv, wv)
        e = y - tv
        dx, dw = vjp(e * (1.0 / D))
        per_row = jnp.mean(e * e, axis=-1, keepdims=True)
        return dx, dw, jnp.broadcast_to(0.5 * jnp.sum(per_row, axis=0, keepdims=True), (1, 128))

    return _tilemap(name, fn, [(x, D, 0), (target, D, 0)], [w], [(D, D, F32)], [(1, D), (1, 128)],
                    rows=rows, tm=_tile(rows, 256))


def _gelu(v):
    return 0.5 * v * (1.0 + lax.erf(v * (1.0 / math.sqrt(2.0))))


def _f_sgu_pre(su, sv, lw, lb):
    zv = _gelu(sv)
    mu = jnp.mean(zv, axis=-1, keepdims=True)
    d = zv - mu
    var = jnp.mean(d * d, axis=-1, keepdims=True)
    return _gelu(su), d * lax.rsqrt(var + EPS) * lw + lb


def _sgu_blocks(tm):
    return [slice(r * SLEN, (r + 1) * SLEN) for r in range(tm // SLEN)]


def _sgu_mix(wm, vnb, bexp, tm):
    rows = []
    for r in _sgu_blocks(tm):
        cols = [jnp.dot(wm[g], vnb[r, g * SCH:(g + 1) * SCH], preferred_element_type=F32) for g in range(GROUPS)]
        rows.append(jnp.concatenate(cols, axis=1) + bexp)
    return jnp.concatenate(rows, axis=0)


def _sgu_fwd(name, z, lw, lb, wm, bexp):
    rows = z.shape[0]
    tm = _tile(rows, 256)

    def fn(su, sv, lwv, lbv, wmv, bev):
        zu, vn = _f_sgu_pre(su.astype(F32), sv.astype(F32), lwv, lbv)
        return (zu * _sgu_mix(wmv, vn.astype(BF16), bev, tm),)

    return _tilemap(name, fn, [(z, D, Z_SU), (z, D, Z_SV)], [lw, lb, wm, bexp], [(D, D, BF16)], [], rows=rows, tm=tm)[0]


def _sgu_bwd(name, z, lw, lb, wm, wmt, bexp, maskf, db, dz):
    rows = z.shape[0]
    tm = _tile(rows, 256)

    def fn(su, sv, dbv, lwv, lbv, wmv, wmtv, bev, mkv):
        (zu, vn), vjp = jax.vjp(_f_sgu_pre, su.astype(F32), sv.astype(F32), lwv, lbv)
        vnb = vn.astype(BF16)
        dbv = dbv.astype(F32)
        dmixed = dbv * zu
        dzu = dbv * _sgu_mix(wmv, vnb, bev, tm)
        dmb = dmixed.astype(BF16)
        dvn_rows, dws, dbf = [], [None] * GROUPS, None
        for r in _sgu_blocks(tm):
            cols = []
            for g in range(GROUPS):
                cs = slice(g * SCH, (g + 1) * SCH)
                cols.append(jnp.dot(wmtv[g], dmb[r, cs], preferred_element_type=F32))
                dw = lax.dot_general(dmb[r, cs], vnb[r, cs], (((1,), (1,)), ((), ())), preferred_element_type=F32)
                dws[g] = dw if dws[g] is None else dws[g] + dw
            dvn_rows.append(jnp.concatenate(cols, axis=1))
            dbf = dmixed[r] if dbf is None else dbf + dmixed[r]
        dsu, dsv, dlw, dlb = vjp((dzu, jnp.concatenate(dvn_rows, axis=0)))
        dws_all = jnp.concatenate([dw * mkv for dw in dws], axis=0)
        return jnp.concatenate([dsu, dsv], axis=1), dws_all, dbf, dlw, dlb

    return _tilemap(name, fn, [(z, D, Z_SU), (z, D, Z_SV), (db, D, 0)], [lw, lb, wm, wmt, bexp, maskf],
                    [], [(GROUPS * SLEN, SLEN), (SLEN, D), (1, D), (1, D)], rows=rows, tm=tm,
                    slab=(dz, IN_COLS, Z_SU * D, 2 * D))


def _group_sums(name, dbf):
    def body(e_ref, x_ref, o_ref):
        o_ref[...] = lax.dot_general(e_ref[...], x_ref[...], (((1,), (1,)), ((), ())),
                                     precision=lax.Precision.HIGHEST, preferred_element_type=F32)

    ind = (jnp.arange(D)[None, :] // SCH == jnp.arange(GROUPS)[:, None]).astype(F32)
    return pl.pallas_call(body, name=name, out_shape=jax.ShapeDtypeStruct((GROUPS, SLEN), F32))(ind, dbf)


def _ret_consts(rows):
    hh = jnp.arange(HEADS, dtype=F32)
    log_g = jnp.log1p(-(2.0 ** (-5.0 - hh)))
    idx = jnp.arange(RL, dtype=F32)
    ci = jnp.arange(RL) // CHUNK
    mask = ci[None, :] <= ci[:, None]
    dm = jnp.where(mask[None], jnp.exp(log_g[:, None, None] * jnp.abs(idx[:, None] - idx[None, :])), 0.0)
    qd = jnp.broadcast_to(jnp.exp(log_g[:, None] * (idx[None, :] + 1.0))[:, :, None], (HEADS, RL, DK))
    kd = jnp.broadcast_to(jnp.exp(log_g[:, None] * (RL - 1.0 - idx[None, :]))[:, :, None], (HEADS, RL, DK))
    cd = jnp.broadcast_to(jnp.exp(log_g * RL)[:, None, None], (HEADS, DK, DV))
    half = DK // 2
    inv = ROPE_BASE ** (-jnp.arange(half, dtype=F32) / half)
    ang = jnp.arange(rows, dtype=jnp.int32).astype(F32)[:, None] * inv[None, :]
    cos, sin = jnp.cos(ang), jnp.sin(ang)
    return dict(dm=dm.astype(F32), qd=qd, kd=kd, cd=cd, cos=jnp.concatenate([cos, cos], axis=1),
                sin=jnp.concatenate([-sin, sin], axis=1))


def _rope(x, cos, sin_signed):
    return x * cos + pltpu.roll(x, DK // 2, 1) * sin_signed


def _rope_t(d, cos, sin_signed):
    return d * cos + pltpu.roll(d * sin_signed, DK // 2, 1)


def _dot(a, b, ca, cb):
    return lax.dot_general(a, b, (((ca,), (cb,)), ((), ())), preferred_element_type=F32)


QKV = 2 * HEADS * DK + HEADS * DV


def _ret_in_specs(blk):
    whole = lambda shape: pl.BlockSpec(shape, lambda n: (0,) * len(shape))
    return [pl.BlockSpec((RL, QKV), lambda n: (blk(n), 0)), pl.BlockSpec((RL, DK), lambda n: (blk(n), 0)),
            pl.BlockSpec((RL, DK), lambda n: (blk(n), 0)), whole((HEADS, RL, RL)), whole((HEADS, RL, DK)),
            whole((HEADS, RL, DK)), whole((HEADS, DK, DV))]


def _ret_head(ref, h):
    q0, k0, v0 = h * DK, HEADS * DK + h * DK, 2 * HEADS * DK + h * DV
    return (slice(None), slice(q0, q0 + DK)), (slice(None), slice(k0, k0 + DK)), (slice(None), slice(v0, v0 + DV))


def _ret_fwd(name, z, rc):
    rows = z.shape[0]
    nb = rows // RL

    def body(qkv_ref, cos_ref, sin_ref, dm_ref, qd_ref, kd_ref, cd_ref, o_ref, st_ref, state):
        @pl.when(pl.program_id(0) == 0)
        def _():
            state[...] = jnp.zeros_like(state)

        cos, sin = cos_ref[...], sin_ref[...]
        for h in range(HEADS):
            qs, ks, vs = _ret_head(qkv_ref, h)
            qf = _rope(qkv_ref[qs].astype(F32), cos, sin) * (DK ** -0.5)
            kf = _rope(qkv_ref[ks].astype(F32), cos, sin)
            vb = qkv_ref[vs]
            s0 = state[h]
            s0b = s0.astype(BF16)
            st_ref[h] = s0b
            sc = _dot(qf.astype(BF16), kf.astype(BF16), 1, 1) * dm_ref[h]
            o = _dot(sc.astype(BF16), vb, 1, 0) + _dot((qf * qd_ref[h]).astype(BF16), s0b, 1, 0)
            o_ref[:, h * DV:(h + 1) * DV] = o.astype(o_ref.dtype)
            state[h] = cd_ref[h] * s0 + _dot((kf * kd_ref[h]).astype(BF16), vb, 0, 0)

    return pl.pallas_call(
        body, name=name, grid=(nb,), in_specs=_ret_in_specs(lambda n: n),
        out_specs=[pl.BlockSpec((RL, HEADS * DV), lambda n: (n, 0)),
                   pl.BlockSpec((None, HEADS, DK, DV), lambda n: (n, 0, 0, 0))],
        out_shape=[jax.ShapeDtypeStruct((rows, HEADS * DV), BF16), jax.ShapeDtypeStruct((nb, HEADS, DK, DV), BF16)],
        scratch_shapes=[pltpu.VMEM((HEADS, DK, DV), F32)],
        compiler_params=_cparams(("arbitrary",), 40),
    )(z, rc["cos"], rc["sin"], rc["dm"], rc["qd"], rc["kd"], rc["cd"])


def _ret_bwd(name, z, rc, states, do, dz):
    rows = z.shape[0]
    nb = rows // RL
    rev = lambda n: nb - 1 - n

    def body(qkv_ref, cos_ref, sin_ref, dm_ref, qd_ref, kd_ref, cd_ref, st_ref, do_ref, dz_in, dqkv_ref, dstate):
        @pl.when(pl.program_id(0) == 0)
        def _():
            dstate[...] = jnp.zeros_like(dstate)

        cos, sin = cos_ref[...], sin_ref[...]
        for h in range(HEADS):
            qs, ks, vs = _ret_head(qkv_ref, h)
            dm, qd, kd = dm_ref[h], qd_ref[h], kd_ref[h]
            qf = _rope(qkv_ref[qs].astype(F32), cos, sin) * (DK ** -0.5)
            kf = _rope(qkv_ref[ks].astype(F32), cos, sin)
            qb, kb, vb, s0b = qf.astype(BF16), kf.astype(BF16), qkv_ref[vs], st_ref[h]
            dob = do_ref[:, h * DV:(h + 1) * DV]
            qsb, ksb = (qf * qd).astype(BF16), (kf * kd).astype(BF16)
            ds1 = dstate[h]
            ds1b = ds1.astype(BF16)
            scb = (_dot(qb, kb, 1, 1) * dm).astype(BF16)
            dscb = (_dot(dob, vb, 1, 1) * dm).astype(BF16)
            dq = _dot(dscb, kb, 1, 0) + _dot(dob, s0b, 1, 1) * qd
            dk = _dot(dscb, qb, 0, 0) + _dot(vb, ds1b, 1, 1) * kd
            dv = _dot(scb, dob, 0, 0) + _dot(ksb, ds1b, 1, 0)
            dqkv_ref[qs] = _rope_t(dq * (DK ** -0.5), cos, sin).astype(dqkv_ref.dtype)
            dqkv_ref[ks] = _rope_t(dk, cos, sin).astype(dqkv_ref.dtype)
            dqkv_ref[vs] = dv.astype(dqkv_ref.dtype)
            dstate[h] = cd_ref[h] * ds1 + _dot(qsb, dob, 0, 0)

    return pl.pallas_call(
        body, name=name, grid=(nb,),
        in_specs=_ret_in_specs(rev) + [pl.BlockSpec((None, HEADS, DK, DV), lambda n: (rev(n), 0, 0, 0)),
                                       pl.BlockSpec((RL, HEADS * DV), lambda n: (rev(n), 0)), ANY],
        out_specs=pl.BlockSpec((RL, QKV), lambda n: (rev(n), 0)),
        out_shape=jax.ShapeDtypeStruct(dz.shape, dz.dtype), input_output_aliases={9: 0},
        scratch_shapes=[pltpu.VMEM((HEADS, DK, DV), F32)],
        compiler_params=_cparams(("arbitrary",), 40),
    )(z, rc["cos"], rc["sin"], rc["dm"], rc["qd"], rc["kd"], rc["cd"], states, do, dz)


def _layer_fwd(l, x, p, rc):
    n = f"l{l}_"
    h = _rms_fwd(n + "rms1", x, p["norm_mix_w"])
    z = _mm_nn_sh(n + "in_proj", h, p["w_in"], out_dtype=BF16)
    o, states = _ret_fwd(n + "ret", z, rc)
    a_in = _mixa_fwd(n + "mixa", z, o, p["ret_gn_w"])
    b_in = _sgu_fwd(n + "sgu", z, p["sgu_ln_w"], p["sgu_ln_b"], p["wm"], p["bexp"])
    if "late" in p:
        p.update(p.pop("late")(b_in))
    br_a = _mm_nn(n + "ret_proj", a_in, p["ret_proj"], out_dtype=BF16, after=p.get("late_token"))
    br_b = _mm_nn(n + "sgu_proj", b_in, p["sgu_proj"], out_dtype=BF16)
    merged = _merge_fwd(n + "merge", z, br_a, br_b)
    x1 = _mm_nn(n + "out_proj", merged, p["w_out"], out_dtype=F32, res=x)
    h2 = _rms_fwd(n + "rms2", x1, p["norm_ffn_w"])
    ffn_a, ffn_c, f = _ffn_in_swiglu(n + "ffn_in", h2, p["w_ffn_in"])
    x2 = _mm_nn(n + "ffn_out", f, p["w_ffn_out"], out_dtype=F32, res=x1, tk=2 * SH_FF)
    saved = dict(x=x, h=h, z=z, o=o, states=states, a_in=a_in, b_in=b_in, br_a=br_a, br_b=br_b, merged=merged,
                 x1=x1, h2=h2, ffn_a=ffn_a, ffn_c=ffn_c, f=f)
    return x2, saved


def _layer_bwd_ffn(l, dx2, p, s, after):
    n = f"l{l}b_"
    df = _mm_nt(n + "d_f", dx2, p["w_ffn_out"], out_dtype=BF16, tn=SH_FF, after=after)
    g_ffn_out = _mm_tn(n + "g_ffn_out", s["f"], dx2, out_dtype=BF16, tm=SH_FF)
    dac = _swiglu_bwd(n + "swiglu", s["ffn_a"], s["ffn_c"], df)
    dh2 = _mm_nt_sh(n + "d_h2", dac, p["w_ffn_in"], out_dtype=BF16)
    g_ffn_in = _mm_tn_sh(n + "g_ffn_in", s["h2"], dac, SH_FF, out_dtype=BF16)
    dx1, g_nffn = _rms_bwd(n + "rms2", s["x1"], p["norm_ffn_w"], dh2, dx2)
    return dx1, dict(w_ffn_in=g_ffn_in, w_ffn_out=g_ffn_out.reshape(NDEV, -1, D)), dict(norm_ffn_w=g_nffn)


def _layer_bwd_mix(l, dx1, p, s, rc, after, send):
    n = f"l{l}b_"
    dmerged = _mm_nt(n + "d_merged", dx1, p["w_out"], out_dtype=BF16, after=after)
    g_out = _mm_tn(n + "g_out", s["merged"], dx1, out_dtype=BF16)
    dz, dbr_a, dbr_b = _merge_bwd(n + "merge", s["z"], s["br_a"], s["br_b"], dmerged)
    da_in = _mm_nt(n + "d_a_in", dbr_a, p["ret_proj"], out_dtype=BF16)
    g_ret_proj = _mm_tn(n + "g_ret_proj", s["a_in"], dbr_a, out_dtype=BF16, tk=2048)
    db_in = _mm_nt(n + "d_b_in", dbr_b, p["sgu_proj"], out_dtype=BF16)
    g_sgu_proj = _mm_tn(n + "g_sgu_proj", s["b_in"], dbr_b, out_dtype=BF16, tk=2048)
    dz, do, g_gn = _mixa_bwd(n + "mixa", s["z"], s["o"], p["ret_gn_w"], da_in, dz)
    dz = _ret_bwd(n + "ret", s["z"], rc, s["states"], do, dz)
    dz, g_ws, dbf, g_lw, g_lb = _sgu_bwd(n + "sgu", s["z"], p["sgu_ln_w"], p["sgu_ln_b"], p["wm"], p["wmt"],
                                          p["bexp"], p["maskf"], db_in, dz)
    g_bs = _group_sums(n + "g_bs", dbf)
    g_in = _mm_tn_sh(n + "g_in", s["h"], dz, SH_IN, out_dtype=BF16)
    token = send(dict(w_in=g_in, ret_proj=g_ret_proj.reshape(NDEV, -1, D), sgu_proj=g_sgu_proj.reshape(NDEV, -1, D),
                      w_out=g_out.reshape(NDEV, -1, D)))
    dh = _mm_nt_sh(n + "d_h", dz, p["w_in"], out_dtype=BF16, after=token)
    dx, g_nmix = _rms_bwd(n + "rms1", s["x"], p["norm_mix_w"], dh, dx1)
    small = dict(norm_mix_w=g_nmix, ret_gn_w=g_gn, sgu_ln_w=g_lw, sgu_ln_b=g_lb, sgu_w_s=g_ws, sgu_b_s=g_bs)
    return dx, small


def _slot(ref, p):
    return ref.at[4 * p[0] + 2 * p[1] + p[2]]


def _allgather(name, shards):
    n = len(shards)

    def body(*refs):
        ins, outs = refs[:n], refs[n:2 * n]
        send_sems, recv_sems, local_sems = refs[2 * n:]
        x, y, c = lax.axis_index("x"), lax.axis_index("y"), lax.axis_index("c")
        me, sibling = (x, y, c), (x, y, 1 - c)
        chips = [(1 - x, y), (x, 1 - y), (1 - x, 1 - y)]

        def copy(a, k, block, to, src=None):
            return pltpu.make_async_remote_copy(
                src_ref=_slot(outs[a], block) if src is None else src, dst_ref=_slot(outs[a], block),
                send_sem=send_sems.at[a, k], recv_sem=recv_sems.at[a, k], device_id=to, device_id_type=MESH_T)

        mine = [pltpu.make_async_copy(ins[a], _slot(outs[a], me), local_sems.at[a]) for a in range(n)]
        for cp in mine:
            cp.start()
        first = []
        for a in range(n):
            first.append(copy(a, 0, me, sibling, src=ins[a]))
            first += [copy(a, 1 + j, me, (*chip, c), src=ins[a]) for j, chip in enumerate(chips)]
        for cp in first:
            cp.start()
        passed = []
        for j, chip in enumerate(chips):
            for a in range(n):
                copy(a, 1 + j, (*chip, c), me).wait_recv()
                cp = copy(a, 4 + j, (*chip, c), sibling)
                cp.start()
                passed.append(cp)
        for a in range(n):
            copy(a, 0, sibling, me).wait_recv()
            for j, chip in enumerate(chips):
                copy(a, 4 + j, (*chip, 1 - c), me).wait_recv()
        for cp in first + passed:
            cp.wait_send()
        for cp in mine:
            cp.wait()

    return pl.pallas_call(
        body, name=name, in_specs=[ANY] * n, out_specs=[ANY] * n,
        out_shape=[jax.ShapeDtypeStruct((NDEV,) + s.shape, s.dtype) for s in shards],
        scratch_shapes=[pltpu.SemaphoreType.DMA((n, 7)), pltpu.SemaphoreType.DMA((n, 7)), pltpu.SemaphoreType.DMA((n,))],
    )(*shards)


def _peers(x, y, c):
    return [((1 - x) if fx else x, (1 - y) if fy else y, (1 - c) if fc else c)
            for fx in (0, 1) for fy in (0, 1) for fc in (0, 1) if fx + fy + fc]


def _exchange_copy(whole, src_ref, land_ref, send_sems, recv_sems, a, k, peer, me, landed):
    return pltpu.make_async_remote_copy(
        src_ref=src_ref if whole else _slot(src_ref, peer), dst_ref=_slot(land_ref, peer if landed else me),
        send_sem=send_sems.at[7 * a + k], recv_sem=recv_sems.at[7 * a + k], device_id=peer, device_id_type=MESH_T)


def _exchange_start(name, srcs, lands, whole, after):
    n = len(srcs)

    def body(*refs):
        src_refs, land_refs = refs[:n], refs[n:2 * n]
        send_sems, recv_sems = refs[2 * n + 1], refs[2 * n + 2]
        token = refs[-1]
        x, y, c = lax.axis_index("x"), lax.axis_index("y"), lax.axis_index("c")
        for a in range(n):
            for k, peer in enumerate(_peers(x, y, c)):
                _exchange_copy(whole, src_refs[a], land_refs[a], send_sems, recv_sems, a, k, peer, (x, y, c), False).start()
        token[...] = jnp.zeros_like(token)

    hbm = lambda t: pltpu.HBM(t.shape, t.dtype)
    out = pl.pallas_call(
        body, name=name, in_specs=[HBM_SPEC] * (2 * n) + [ANY],
        out_specs=(SEM_SPEC, SEM_SPEC, *[HBM_SPEC] * (2 * n), pl.BlockSpec(memory_space=pltpu.VMEM)),
        out_shape=(pltpu.SemaphoreType.DMA((7 * n,)), pltpu.SemaphoreType.DMA((7 * n,)), *[hbm(t) for t in srcs],
                   *[hbm(t) for t in lands], jax.ShapeDtypeStruct((8, 128), F32)),
        input_output_aliases={i: 2 + i for i in range(2 * n)},
        compiler_params=pltpu.CompilerParams(has_side_effects=pltpu.SideEffectType.DATAFLOW_SIDE_EFFECTING),
    )(*[pltpu.with_memory_space_constraint(t, pltpu.HBM) for t in (*srcs, *lands)], after)
    return dict(send=out[0], recv=out[1], srcs=out[2:2 + n], lands=out[2 + n:2 + 2 * n], token=out[-1])


def _exchange_wait(name, started, whole, after):
    n = len(started["srcs"])

    def body(*refs):
        src_refs, land_refs = refs[:n], refs[n:2 * n]
        send_sems, recv_sems = refs[2 * n], refs[2 * n + 1]
        x, y, c = lax.axis_index("x"), lax.axis_index("y"), lax.axis_index("c")
        for a in range(n):
            for k, peer in enumerate(_peers(x, y, c)):
                cp = _exchange_copy(whole, src_refs[a], land_refs[a], send_sems, recv_sems, a, k, peer, (x, y, c), True)
                cp.wait_send()
                cp.wait_recv()

    hbm = lambda t: pltpu.HBM(t.shape, t.dtype)
    out = pl.pallas_call(
        body, name=name, in_specs=[HBM_SPEC] * (2 * n) + [SEM_SPEC, SEM_SPEC, ANY], out_specs=[HBM_SPEC] * (2 * n),
        out_shape=[hbm(t) for t in (*started["srcs"], *started["lands"])],
        input_output_aliases={i: i for i in range(2 * n)},
        compiler_params=pltpu.CompilerParams(has_side_effects=pltpu.SideEffectType.DATAFLOW_SIDE_EFFECTING),
    )(*started["srcs"], *started["lands"], started["send"], started["recv"], after)
    return out[n:]


def _adamw(w, g, m, v):
    m = ADAM_B1 * m + (1.0 - ADAM_B1) * g
    v = ADAM_B2 * v + (1.0 - ADAM_B2) * (g * g)
    m_hat = m / (1.0 - ADAM_B1 ** ADAM_STEP)
    v_hat = v / (1.0 - ADAM_B2 ** ADAM_STEP)
    return -ADAM_LR * (m_hat / (jnp.sqrt(v_hat) + ADAM_EPS) + ADAM_WD * w), m, v


def _sum_and_adamw(name, recv, w, m, v, layer, prev):
    _, r, c = recv.shape
    tr = max(t for t in range(16, r + 1, 16) if r % t == 0 and t * c <= ADAM_TILE_ELEMS)

    def body(recv_ref, w_ref, m_ref, v_ref, *rest):
        g_ref, d_ref, m_out, v_out = rest[4:]
        g = recv_ref[0].astype(F32)
        for s in range(1, NDEV):
            g = g + recv_ref[s].astype(F32)
        delta, m_new, v_new = _adamw(w_ref[...], g, m_ref[...], v_ref[...])
        g_ref[...] = g
        d_ref[...] = delta
        m_out[...] = m_new
        v_out[...] = v_new

    lay = pl.BlockSpec((None, tr, c), lambda i: (layer, i, 0))
    return pl.pallas_call(
        body, name=name, grid=(r // tr,),
        in_specs=[pl.BlockSpec((NDEV, tr, c), lambda i: (0, i, 0)), lay, lay, lay] + [ANY] * 4,
        out_specs=[lay] * 4, out_shape=[jax.ShapeDtypeStruct(w.shape, F32)] * 4,
        input_output_aliases={4: 0, 5: 1, 6: 2, 7: 3},
        compiler_params=_cparams(("arbitrary",), 48),
    )(recv, w, m, v, *prev)


def _small_sum_and_adamw(name, recv, w, m, v):
    r = w.shape[0]

    def body(recv_ref, w_ref, m_ref, v_ref, g_ref, d_ref, m_out, v_out):
        g = recv_ref[0]
        for s in range(1, NDEV):
            g = g + recv_ref[s]
        delta, m_new, v_new = _adamw(w_ref[...], g, m_ref[...], v_ref[...])
        g_ref[...] = g
        d_ref[...] = delta
        m_out[...] = m_new
        v_out[...] = v_new

    return pl.pallas_call(body, name=name, out_shape=[jax.ShapeDtypeStruct((r, 128), F32)] * 4,
                          compiler_params=_cparams(None, 48))(recv, w, m, v)


BIG = ("w_in", "ret_proj", "sgu_proj", "w_out", "w_ffn_in", "w_ffn_out")
SMALL = ("norm_mix_w", "ret_gn_w", "sgu_ln_w", "sgu_ln_b", "sgu_w_s", "sgu_b_s", "norm_ffn_w", "final_norm_w")
WEIGHTS = ("norm_mix_w", "w_in", "ret_gn_w", "ret_proj", "sgu_ln_w", "sgu_ln_b", "sgu_w_s", "sgu_b_s", "sgu_proj",
           "w_out", "norm_ffn_w", "w_ffn_in", "w_ffn_out", "final_norm_w")


def _pack_small(d):
    return jnp.concatenate([d[k].reshape(-1, 128) for k in SMALL], axis=0)


def _unpack_small(packed, like):
    out, r0 = {}, 0
    for k in SMALL:
        r = like[k].size // 128
        out[k] = packed[r0:r0 + r].reshape(like[k].shape)
        r0 += r
    return out


def _gathered(keys, arrays):
    return {k: a if k in ("w_in", "w_ffn_in") else a.reshape(-1, D) for k, a in zip(keys, arrays)}


def _layer_params(l, w, gathered, mask):
    p = _gathered(BIG, gathered)
    for k in ("norm_mix_w", "ret_gn_w", "sgu_ln_w", "sgu_ln_b", "norm_ffn_w"):
        p[k] = w[k][l][None, :]
    wm = jnp.where(mask[None], w["sgu_w_s"][l], 0.0)
    p["wm"], p["wmt"] = wm.astype(BF16), jnp.swapaxes(wm, 1, 2).astype(BF16)
    p["bexp"] = jnp.repeat(w["sgu_b_s"][l].T, SCH, axis=1)
    p["maskf"] = mask.astype(F32)
    return p


def _step(w, m, v, x, target):
    depth = w["w_in"].shape[0]
    rc = _ret_consts(x.shape[0])
    pos = jnp.arange(SLEN)
    mask = (pos[None, :] // CHUNK) <= (pos[:, None] // CHUNK)
    shards = [[w[k][l].astype(BF16) for k in BIG] for l in range(depth)]
    me = 4 * lax.axis_index("x") + 2 * lax.axis_index("y") + lax.axis_index("c")

    def lands_of(ts):
        return [lax.dynamic_update_index_in_dim(lax.empty((NDEV,) + t.shape, t.dtype), t, me, 0) for t in ts]

    w_in_now = _allgather("gather_l0_in", shards[0][:1])[0]
    rest = _exchange_start("gather_start_l0_rest", shards[0][1:], lands_of(shards[0][1:]), True, w_in_now)
    saved, layers, next_in = [], [], {}
    for l in range(depth):
        p = _layer_params(l, w, [w_in_now] + list(rest["lands"]), mask)
        p["norm_mix_w"] = p["norm_mix_w"] + rest["token"][0, 0]

        def late(after, l=l, rest=rest):
            got = _gathered(BIG[1:], _exchange_wait(f"gather_wait_l{l}_rest", rest, True, after))
            if l + 1 < depth:
                next_in[l + 1] = _exchange_start(f"gather_start_l{l + 1}_in", shards[l + 1][:1], lands_of(shards[l + 1][:1]),
                                                 True, got["ret_proj"])
                got["late_token"] = next_in[l + 1]["token"]
            return got

        p["late"] = late
        x, s = _layer_fwd(l, x, p, rc)
        if l + 1 < depth:
            w_in_now = _exchange_wait(f"gather_wait_l{l + 1}_in", next_in[l + 1], True, x)[0]
            rest = _exchange_start(f"gather_start_l{l + 1}_rest", shards[l + 1][1:], lands_of(shards[l + 1][1:]), True,
                                   w_in_now)
        saved.append(s)
        layers.append(p)

    dx, g_final, loss_row = _loss_bwd("loss", x, w["final_norm_w"][None, :], target)
    loss = lax.psum(loss_row[0, 0], ("x", "y", "c"))

    res = {k: [lax.empty(w[k].shape, F32) for _ in range(4)] for k in BIG}
    small = {k: [None] * depth for k in SMALL if k != "final_norm_w"}

    def scatter_start(tag, big, after):
        keys = [k for k in BIG if k in big]
        parts = [big[k] for k in keys]
        lands = [lax.dynamic_update_index_in_dim(lax.empty(t.shape, t.dtype), lax.dynamic_index_in_dim(t, me, 0), me, 0)
                 for t in parts]
        return keys, _exchange_start(f"scatter_start_{tag}", parts, lands, False, after)

    def finish(l, pending, after):
        for tag, (keys, started) in pending:
            recv = _exchange_wait(f"scatter_wait_{tag}", started, False, after)
            for k, rv in zip(keys, recv):
                res[k] = _sum_and_adamw(f"adamw_{k}_l{l}", rv, w[k], m[k], v[k], l, res[k])

    pending, token = None, loss_row
    for l in reversed(range(depth)):
        dx1, big_ffn, sm_ffn = _layer_bwd_ffn(l, dx, layers[l], saved[l], token)
        now = [(f"ffn_l{l}", scatter_start(f"ffn_l{l}", big_ffn, dx1))]

        def send(big_mix, l=l, now=now, dx1=dx1):
            now.append((f"mix_l{l}", scatter_start(f"mix_l{l}", big_mix, dx1)))
            return now[1][1][1]["token"]

        dx, sm_mix = _layer_bwd_mix(l, dx1, layers[l], saved[l], rc, now[0][1][1]["token"], send)
        token = now[1][1][1]["token"]
        if pending is not None:
            finish(l + 1, pending, dx)
        pending = now
        for k, g in {**sm_ffn, **sm_mix}.items():
            small[k][l] = g
    finish(0, pending, dx)

    g_small = {k: jnp.stack(small[k]).reshape(w[k].shape) for k in small}
    g_small["final_norm_w"] = g_final.reshape(w["final_norm_w"].shape)
    recv = _allgather("gather_small", [_pack_small(g_small)])[0]
    sres = _small_sum_and_adamw("adamw_small", recv, _pack_small(w), _pack_small(m), _pack_small(v))
    sres = [_unpack_small(a, w) for a in sres]

    outs = []
    for i in range(4):
        outs.append([sres[i][k] if k in SMALL else res[k][i] for k in WEIGHTS])
    return loss, dx, outs


def kernel(x, norm_mix_w, w_in, ret_gn_w, ret_proj, sgu_ln_w, sgu_ln_b, sgu_w_s, sgu_b_s, sgu_proj, w_out, norm_ffn_w, w_ffn_in, w_ffn_out, final_norm_w, loss_target, m_norm_mix_w, m_w_in, m_ret_gn_w, m_ret_proj, m_sgu_ln_w, m_sgu_ln_b, m_sgu_w_s, m_sgu_b_s, m_sgu_proj, m_w_out, m_norm_ffn_w, m_w_ffn_in, m_w_ffn_out, m_final_norm_w, v_norm_mix_w, v_w_in, v_ret_gn_w, v_ret_proj, v_sgu_ln_w, v_sgu_ln_b, v_sgu_w_s, v_sgu_b_s, v_sgu_proj, v_w_out, v_norm_ffn_w, v_w_ffn_in, v_w_ffn_out, v_final_norm_w):
    w = dict(zip(WEIGHTS, (norm_mix_w, w_in, ret_gn_w, ret_proj, sgu_ln_w, sgu_ln_b, sgu_w_s, sgu_b_s, sgu_proj, w_out,
                           norm_ffn_w, w_ffn_in, w_ffn_out, final_norm_w)))
    m = dict(zip(WEIGHTS, (m_norm_mix_w, m_w_in, m_ret_gn_w, m_ret_proj, m_sgu_ln_w, m_sgu_ln_b, m_sgu_w_s, m_sgu_b_s,
                           m_sgu_proj, m_w_out, m_norm_ffn_w, m_w_ffn_in, m_w_ffn_out, m_final_norm_w)))
    v = dict(zip(WEIGHTS, (v_norm_mix_w, v_w_in, v_ret_gn_w, v_ret_proj, v_sgu_ln_w, v_sgu_ln_b, v_sgu_w_s, v_sgu_b_s,
                           v_sgu_proj, v_w_out, v_norm_ffn_w, v_w_ffn_in, v_w_ffn_out, v_final_norm_w)))
    loss, dx, (grads, deltas, new_m, new_v) = _step(w, m, v, x.reshape(x.shape[1:]), loss_target.reshape(x.shape[1:]))
    return (loss, dx[None], *grads, *deltas, *new_m, *new_v)
```

```python
import functools
import math

import jax
import jax.numpy as jnp
from jax import lax
from jax.experimental import pallas as pl
from jax.experimental.pallas import tpu as pltpu

F32, BF16 = jnp.float32, jnp.bfloat16

D = 2048
HEADS, DK, DV = 8, 128, 256
GROUPS, SLEN = 8, 128
SCH = D // GROUPS
DFF = 5632
CHUNK = 64
IN_COLS = 14336
NDEV = 8
SH_IN = IN_COLS // NDEV
SH_FF = 2 * DFF // NDEV
EPS = 1e-6
ROPE_BASE = 10000.0
RL = 256
Z_G, Z_SU, Z_SV, Z_GA, Z_GB = 2, 3, 4, 5, 6

ADAM_TILE_ELEMS = 256 * 1024
ADAM_LR, ADAM_B1, ADAM_B2, ADAM_EPS, ADAM_WD, ADAM_STEP = 0.001, 0.9, 0.999, 1e-08, 0.01, 10

VMEM_BYTES_V7X = 64 << 20
MESH_T = pl.DeviceIdType.MESH
ANY = pl.BlockSpec(memory_space=pl.ANY)
HBM_SPEC = pl.BlockSpec(memory_space=pltpu.HBM)
SEM_SPEC = pl.BlockSpec(memory_space=pltpu.SEMAPHORE)


def _cparams(sem, vmem_mb):
    assert (vmem_mb << 20) < VMEM_BYTES_V7X
    return pltpu.CompilerParams(dimension_semantics=sem, vmem_limit_bytes=vmem_mb << 20)


def _mm(name, a, b, *, grid, a_spec, b_spec, o_spec, o_block, out_shape, out_dtype, contract, res=None, after=None,
        split=1, vmem_mb=48):
    nk = grid[2]
    dn = (contract, ((), ()))
    has_res = res is not None

    def body(*refs):
        a_ref, b_ref = refs[:2]
        r_ref = refs[2] if has_res else None
        o_ref, acc_ref = refs[-2:]
        if split == 1:
            part = lax.dot_general(a_ref[...].astype(BF16), b_ref[...].astype(BF16), dn, preferred_element_type=F32)
        else:
            kp = a_ref.shape[1] // split
            part = sum(lax.dot_general(a_ref[:, s * kp:(s + 1) * kp].astype(BF16), b_ref[s].astype(BF16), dn,
                                       preferred_element_type=F32) for s in range(split))

        def finish(val):
            if has_res:
                val = val + r_ref[...].astype(F32)
            o_ref[...] = val.astype(o_ref.dtype)

        if nk == 1:
            finish(part)
        else:
            k = pl.program_id(2)

            @pl.when(k == 0)
            def _():
                acc_ref[...] = part

            @pl.when(k > 0)
            def _():
                acc_ref[...] += part

            @pl.when(k == nk - 1)
            def _():
                finish(acc_ref[...])

    ins = [a, b] + ([res] if has_res else []) + ([after] if after is not None else [])
    in_specs = [a_spec, b_spec] + ([o_spec] if has_res else []) + ([ANY] if after is not None else [])
    return pl.pallas_call(
        body, name=name, grid=grid, in_specs=in_specs, out_specs=o_spec,
        out_shape=jax.ShapeDtypeStruct(out_shape, out_dtype),
        scratch_shapes=[pltpu.VMEM(o_block if nk > 1 else (8, 128), F32)],
        compiler_params=_cparams(("parallel", "parallel", "arbitrary"), vmem_mb),
    )(*ins)


def _tile(n, want):
    t = min(n, want)
    assert n % t == 0, (n, want)
    return t


def _mm_nn(name, a, b, *, out_dtype, tm=1024, tn=1024, tk=None, res=None, after=None):
    (m, k), n = a.shape, b.shape[1]
    tm, tn, tk = _tile(m, tm), _tile(n, tn), k if tk is None else tk
    return _mm(name, a, b, grid=(m // tm, n // tn, k // tk),
               a_spec=pl.BlockSpec((tm, tk), lambda i, j, kk: (i, kk)),
               b_spec=pl.BlockSpec((tk, tn), lambda i, j, kk: (kk, j)),
               o_spec=pl.BlockSpec((tm, tn), lambda i, j, kk: (i, j)), o_block=(tm, tn),
               out_shape=(m, n), out_dtype=out_dtype, contract=((1,), (0,)), res=res, after=after)


def _mm_nt(name, a, b, *, out_dtype, tm=1024, tn=1024, tk=None, after=None):
    (m, k), n = a.shape, b.shape[0]
    tm, tn, tk = _tile(m, tm), _tile(n, tn), k if tk is None else tk
    return _mm(name, a, b, grid=(m // tm, n // tn, k // tk),
               a_spec=pl.BlockSpec((tm, tk), lambda i, j, kk: (i, kk)),
               b_spec=pl.BlockSpec((tn, tk), lambda i, j, kk: (j, kk)),
               o_spec=pl.BlockSpec((tm, tn), lambda i, j, kk: (i, j)), o_block=(tm, tn),
               out_shape=(m, n), out_dtype=out_dtype, contract=((1,), (1,)), after=after)


def _mm_tn(name, a, b, *, out_dtype, tm=1024, tn=1024, tk=1024):
    (k, m), n = a.shape, b.shape[1]
    tm, tn, tk = _tile(m, tm), _tile(n, tn), _tile(k, tk)
    return _mm(name, a, b, grid=(m // tm, n // tn, k // tk),
               a_spec=pl.BlockSpec((tk, tm), lambda i, j, kk: (kk, i)),
               b_spec=pl.BlockSpec((tk, tn), lambda i, j, kk: (kk, j)),
               o_spec=pl.BlockSpec((tm, tn), lambda i, j, kk: (i, j)), o_block=(tm, tn),
               out_shape=(m, n), out_dtype=out_dtype, contract=((0,), (0,)), vmem_mb=56)


def _mm_nn_sh(name, a, bsh, *, out_dtype, tm=1024):
    (m, k), ns = a.shape, bsh.shape[2]
    tm = _tile(m, tm)
    return _mm(name, a, bsh, grid=(m // tm, NDEV, 1),
               a_spec=pl.BlockSpec((tm, k), lambda i, j, kk: (i, 0)),
               b_spec=pl.BlockSpec((None, k, ns), lambda i, j, kk: (j, 0, 0)),
               o_spec=pl.BlockSpec((tm, ns), lambda i, j, kk: (i, j)), o_block=(tm, ns),
               out_shape=(m, NDEV * ns), out_dtype=out_dtype, contract=((1,), (0,)), vmem_mb=56)


def _mm_nt_sh(name, a, bsh, *, out_dtype, tm=1024, tn=1024, after=None):
    m, (_, n, ns) = a.shape[0], bsh.shape
    tm, tn = _tile(m, tm), _tile(n, tn)
    return _mm(name, a, bsh, grid=(m // tm, n // tn, NDEV // 2),
               a_spec=pl.BlockSpec((tm, 2 * ns), lambda i, j, kk: (i, kk)),
               b_spec=pl.BlockSpec((2, tn, ns), lambda i, j, kk: (kk, j, 0)),
               o_spec=pl.BlockSpec((tm, tn), lambda i, j, kk: (i, j)), o_block=(tm, tn),
               out_shape=(m, n), out_dtype=out_dtype, contract=((1,), (1,)), after=after, split=2)


def _mm_tn_sh(name, a, b, ns, *, out_dtype, tm=1024, tk=2048):
    k, m = a.shape
    tm, tk = _tile(m, tm), _tile(k, tk)
    return _mm(name, a, b, grid=(m // tm, NDEV, k // tk),
               a_spec=pl.BlockSpec((tk, tm), lambda i, j, kk: (kk, i)),
               b_spec=pl.BlockSpec((tk, ns), lambda i, j, kk: (kk, j)),
               o_spec=pl.BlockSpec((None, tm, ns), lambda i, j, kk: (j, i, 0)), o_block=(tm, ns),
               out_shape=(NDEV, m, ns), out_dtype=out_dtype, contract=((0,), (0,)), vmem_mb=56)


def _tilemap(name, fn, tiled_ins, full_ins, tiled_outs, acc_shapes, *, rows, tm, ncol=1, vmem_mb=48, slab=None):
    n_ti, n_fi, n_to, n_ao = len(tiled_ins), len(full_ins), len(tiled_outs), len(acc_shapes)
    assert ncol == 1 or (n_ao == 0 and slab is None)
    if slab is not None:
        n_to += 1

    def body(*refs):
        t_in, f_in = refs[:n_ti], refs[n_ti:n_ti + n_fi]
        n_in = n_ti + n_fi + (slab is not None and slab[0] is not None)
        t_out, a_out = refs[n_in:n_in + n_to], refs[n_in + n_to:]
        outs = fn(*[r[...] for r in t_in], *[r[...] for r in f_in])
        for r, v in zip(t_out, outs[:n_to]):
            r[...] = v.astype(r.dtype)
        if n_ao:
            i = pl.program_id(1)

            @pl.when(i == 0)
            def _():
                for r, v in zip(a_out, outs[n_to:]):
                    r[...] = v

            @pl.when(i > 0)
            def _():
                for r, v in zip(a_out, outs[n_to:]):
                    r[...] += v

    in_specs = [pl.BlockSpec((tm, bw), lambda j, i, base=base: (i, base + j)) for (_, bw, base) in tiled_ins]
    in_specs += [pl.BlockSpec(a.shape, lambda j, i, nd=a.ndim: (0,) * nd) for a in full_ins]
    out_specs = [pl.BlockSpec((tm, bw), lambda j, i: (i, j)) for (_, bw, _) in tiled_outs]
    out_shape = [jax.ShapeDtypeStruct((rows, w), dt) for (w, _, dt) in tiled_outs]
    extra, aliases = [], {}
    if slab is not None:
        arr, total, col0, width = slab
        out_specs.insert(0, pl.BlockSpec((pl.Element(tm), pl.Element(width)), lambda j, i: (i * tm, col0)))
        out_shape.insert(0, jax.ShapeDtypeStruct((rows, total), BF16))
        if arr is not None:
            extra, aliases = [arr], {n_ti + n_fi: 0}
            in_specs.append(ANY)
    out_specs += [pl.BlockSpec(s, lambda j, i: (0, 0)) for s in acc_shapes]
    out_shape += [jax.ShapeDtypeStruct(s, F32) for s in acc_shapes]
    return pl.pallas_call(
        body, name=name, grid=(ncol, rows // tm), in_specs=in_specs, out_specs=out_specs, out_shape=out_shape,
        input_output_aliases=aliases, compiler_params=_cparams(("parallel", "arbitrary"), vmem_mb),
    )(*[t[0] for t in tiled_ins], *full_ins, *extra)


def _f_rms(x, w):
    return x * lax.rsqrt(jnp.mean(x * x, axis=-1, keepdims=True) + EPS) * w


def _rms_fwd(name, x, w):
    rows = x.shape[0]
    return _tilemap(name, lambda xv, wv: (_f_rms(xv, wv),), [(x, D, 0)], [w], [(D, D, BF16)], [],
                    rows=rows, tm=_tile(rows, 256))[0]


def _rms_bwd(name, x, w, dh, dres):
    rows = x.shape[0]

    def fn(xv, dhv, drv, wv):
        _, vjp = jax.vjp(_f_rms, xv, wv)
        dx, dw = vjp(dhv.astype(F32))
        return dx + drv, dw

    return _tilemap(name, fn, [(x, D, 0), (dh, D, 0), (dres, D, 0)], [w], [(D, D, F32)], [(1, D)],
                    rows=rows, tm=_tile(rows, 256))


def _f_mixa_head(g, o, gw):
    mu = jnp.mean(o, axis=-1, keepdims=True)
    d = o - mu
    var = jnp.mean(d * d, axis=-1, keepdims=True)
    return g * jax.nn.sigmoid(g) * (d * lax.rsqrt(var + EPS) * gw)


def _heads(width):
    return [slice(h * width, (h + 1) * width) for h in range(HEADS)]


def _mixa_fwd(name, z, o, gw):
    rows = z.shape[0]

    def fn(g, ov, gwv):
        g, ov = g.astype(F32), ov.astype(F32)
        return (jnp.concatenate([_f_mixa_head(g[:, s], ov[:, s], gwv[:, s]) for s in _heads(DV)], axis=1),)

    return _tilemap(name, fn, [(z, D, Z_G), (o, D, 0)], [gw], [(D, D, BF16)], [], rows=rows, tm=_tile(rows, 256))[0]


def _mixa_bwd(name, z, o, gw, da, dz):
    rows = z.shape[0]

    def fn(g, ov, dav, gwv):
        g, ov, dav = g.astype(F32), ov.astype(F32), dav.astype(F32)
        parts = []
        for s in _heads(DV):
            _, vjp = jax.vjp(_f_mixa_head, g[:, s], ov[:, s], gwv[:, s])
            parts.append(vjp(dav[:, s]))
        return tuple(jnp.concatenate([p[i] for p in parts], axis=1) for i in range(3))

    return _tilemap(name, fn, [(z, D, Z_G), (o, D, 0), (da, D, 0)], [gw], [(D, D, BF16)], [(1, D)],
                    rows=rows, tm=_tile(rows, 256), slab=(dz, IN_COLS, Z_G * D, D))


def _f_merge(ga, gb, a, b):
    return jax.nn.sigmoid(ga) * a + jax.nn.sigmoid(gb) * b


def _merge_fwd(name, z, a, b):
    rows, bw = z.shape[0], 1024

    def fn(ga, gb, av, bv):
        return (_f_merge(ga.astype(F32), gb.astype(F32), av.astype(F32), bv.astype(F32)),)

    return _tilemap(name, fn, [(z, bw, Z_GA * 2), (z, bw, Z_GB * 2), (a, bw, 0), (b, bw, 0)], [], [(D, bw, BF16)], [],
                    rows=rows, tm=_tile(rows, 512), ncol=D // bw)[0]


def _merge_bwd(name, z, a, b, dm):
    rows = z.shape[0]

    def fn(ga, gb, av, bv, dmv):
        _, vjp = jax.vjp(_f_merge, ga.astype(F32), gb.astype(F32), av.astype(F32), bv.astype(F32))
        dga, dgb, dav, dbv = vjp(dmv.astype(F32))
        return jnp.concatenate([dga, dgb], axis=1), dav, dbv

    return _tilemap(name, fn, [(z, D, Z_GA), (z, D, Z_GB), (a, D, 0), (b, D, 0), (dm, D, 0)], [],
                    [(D, D, BF16)] * 2, [], rows=rows, tm=_tile(rows, 256), slab=(None, IN_COLS, Z_GA * D, 2 * D))


def _f_swiglu(a, c):
    return a * jax.nn.sigmoid(a) * c


def _ffn_in_swiglu(name, h, wsh):
    rows = h.shape[0]
    tm = _tile(rows, 512)

    def body(h_ref, wa_ref, wc_ref, a_ref, c_ref, f_ref):
        hv = h_ref[...]
        a = jnp.dot(hv, wa_ref[...], preferred_element_type=F32)
        c = jnp.dot(hv, wc_ref[...], preferred_element_type=F32)
        a_ref[...] = a.astype(a_ref.dtype)
        c_ref[...] = c.astype(c_ref.dtype)
        f_ref[...] = _f_swiglu(a, c).astype(f_ref.dtype)

    out = pl.BlockSpec((tm, SH_FF), lambda j, i: (i, j))
    return pl.pallas_call(
        body, name=name, grid=(NDEV // 2, rows // tm),
        in_specs=[pl.BlockSpec((tm, D), lambda j, i: (i, 0)), pl.BlockSpec((None, D, SH_FF), lambda j, i: (j, 0, 0)),
                  pl.BlockSpec((None, D, SH_FF), lambda j, i: (j + NDEV // 2, 0, 0))],
        out_specs=[out, out, out], out_shape=[jax.ShapeDtypeStruct((rows, DFF), BF16)] * 3,
        compiler_params=_cparams(("parallel", "arbitrary"), 56),
    )(h, wsh, wsh)


def _swiglu_bwd(name, a_part, c_part, df):
    rows = a_part.shape[0]

    def fn(a, c, dfv):
        _, vjp = jax.vjp(_f_swiglu, a.astype(F32), c.astype(F32))
        return (jnp.concatenate(vjp(dfv.astype(F32)), axis=1),)

    return _tilemap(name, fn, [(a_part, DFF, 0), (c_part, DFF, 0), (df, DFF, 0)], [], [(2 * DFF, 2 * DFF, BF16)], [],
                    rows=rows, tm=_tile(rows, 128))[0]


def _loss_bwd(name, x, w, target):
    rows = x.shape[0]

    def fn(xv, tv, wv):
        y, vjp = jax.vjp(_f_rms, xv, wv)
        e = y - tv
        dx, dw = vjp(e * (1.0 / D))
        per_row = jnp.mean(e * e, axis=-1, keepdims=True)
        return dx, dw, jnp.broadcast_to(0.5 * jnp.sum(per_row, axis=0, keepdims=True), (1, 128))

    return _tilemap(name, fn, [(x, D, 0), (target, D, 0)], [w], [(D, D, F32)], [(1, D), (1, 128)],
                    rows=rows, tm=_tile(rows, 256))


def _gelu(v):
    return 0.5 * v * (1.0 + lax.erf(v * (1.0 / math.sqrt(2.0))))


def _f_sgu_pre(su, sv, lw, lb):
    zv = _gelu(sv)
    mu = jnp.mean(zv, axis=-1, keepdims=True)
    d = zv - mu
    var = jnp.mean(d * d, axis=-1, keepdims=True)
    return _gelu(su), d * lax.rsqrt(var + EPS) * lw + lb


def _sgu_blocks(tm):
    return [slice(r * SLEN, (r + 1) * SLEN) for r in range(tm // SLEN)]


def _sgu_mix(wm, vnb, bexp, tm):
    rows = []
    for r in _sgu_blocks(tm):
        cols = [jnp.dot(wm[g], vnb[r, g * SCH:(g + 1) * SCH], preferred_element_type=F32) for g in range(GROUPS)]
        rows.append(jnp.concatenate(cols, axis=1) + bexp)
    return jnp.concatenate(rows, axis=0)


def _sgu_fwd(name, z, lw, lb, wm, bexp):
    rows = z.shape[0]
    tm = _tile(rows, 256)

    def fn(su, sv, lwv, lbv, wmv, bev):
        zu, vn = _f_sgu_pre(su.astype(F32), sv.astype(F32), lwv, lbv)
        return (zu * _sgu_mix(wmv, vn.astype(BF16), bev, tm),)

    return _tilemap(name, fn, [(z, D, Z_SU), (z, D, Z_SV)], [lw, lb, wm, bexp], [(D, D, BF16)], [], rows=rows, tm=tm)[0]


def _sgu_bwd(name, z, lw, lb, wm, wmt, bexp, maskf, db, dz):
    rows = z.shape[0]
    tm = _tile(rows, 256)

    def fn(su, sv, dbv, lwv, lbv, wmv, wmtv, bev, mkv):
        (zu, vn), vjp = jax.vjp(_f_sgu_pre, su.astype(F32), sv.astype(F32), lwv, lbv)
        vnb = vn.astype(BF16)
        dbv = dbv.astype(F32)
        dmixed = dbv * zu
        dzu = dbv * _sgu_mix(wmv, vnb, bev, tm)
        dmb = dmixed.astype(BF16)
        dvn_rows, dws, dbf = [], [None] * GROUPS, None
        for r in _sgu_blocks(tm):
            cols = []
            for g in range(GROUPS):
                cs = slice(g * SCH, (g + 1) * SCH)
                cols.append(jnp.dot(wmtv[g], dmb[r, cs], preferred_element_type=F32))
                dw = lax.dot_general(dmb[r, cs], vnb[r, cs], (((1,), (1,)), ((), ())), preferred_element_type=F32)
                dws[g] = dw if dws[g] is None else dws[g] + dw
            dvn_rows.append(jnp.concatenate(cols, axis=1))
            dbf = dmixed[r] if dbf is None else dbf + dmixed[r]
        dsu, dsv, dlw, dlb = vjp((dzu, jnp.concatenate(dvn_rows, axis=0)))
        dws_all = jnp.concatenate([dw * mkv for dw in dws], axis=0)
        return jnp.concatenate([dsu, dsv], axis=1), dws_all, dbf, dlw, dlb

    return _tilemap(name, fn, [(z, D, Z_SU), (z, D, Z_SV), (db, D, 0)], [lw, lb, wm, wmt, bexp, maskf],
                    [], [(GROUPS * SLEN, SLEN), (SLEN, D), (1, D), (1, D)], rows=rows, tm=tm,
                    slab=(dz, IN_COLS, Z_SU * D, 2 * D))


def _group_sums(name, dbf):
    def body(e_ref, x_ref, o_ref):
        o_ref[...] = lax.dot_general(e_ref[...], x_ref[...], (((1,), (1,)), ((), ())),
                                     precision=lax.Precision.HIGHEST, preferred_element_type=F32)

    ind = (jnp.arange(D)[None, :] // SCH == jnp.arange(GROUPS)[:, None]).astype(F32)
    return pl.pallas_call(body, name=name, out_shape=jax.ShapeDtypeStruct((GROUPS, SLEN), F32))(ind, dbf)


def _ret_consts(rows):
    hh = jnp.arange(HEADS, dtype=F32)
    log_g = jnp.log1p(-(2.0 ** (-5.0 - hh)))
    idx = jnp.arange(RL, dtype=F32)
    ci = jnp.arange(RL) // CHUNK
    mask = ci[None, :] <= ci[:, None]
    dm = jnp.where(mask[None], jnp.exp(log_g[:, None, None] * jnp.abs(idx[:, None] - idx[None, :])), 0.0)
    qd = jnp.broadcast_to(jnp.exp(log_g[:, None] * (idx[None, :] + 1.0))[:, :, None], (HEADS, RL, DK))
    kd = jnp.broadcast_to(jnp.exp(log_g[:, None] * (RL - 1.0 - idx[None, :]))[:, :, None], (HEADS, RL, DK))
    cd = jnp.broadcast_to(jnp.exp(log_g * RL)[:, None, None], (HEADS, DK, DV))
    half = DK // 2
    inv = ROPE_BASE ** (-jnp.arange(half, dtype=F32) / half)
    ang = jnp.arange(rows, dtype=jnp.int32).astype(F32)[:, None] * inv[None, :]
    cos, sin = jnp.cos(ang), jnp.sin(ang)
    return dict(dm=dm.astype(F32), qd=qd, kd=kd, cd=cd, cos=jnp.concatenate([cos, cos], axis=1),
                sin=jnp.concatenate([-sin, sin], axis=1))


def _rope(x, cos, sin_signed):
    return x * cos + pltpu.roll(x, DK // 2, 1) * sin_signed


def _rope_t(d, cos, sin_signed):
    return d * cos + pltpu.roll(d * sin_signed, DK // 2, 1)


def _dot(a, b, ca, cb):
    return lax.dot_general(a, b, (((ca,), (cb,)), ((), ())), preferred_element_type=F32)


QKV = 2 * HEADS * DK + HEADS * DV


def _ret_in_specs(blk):
    whole = lambda shape: pl.BlockSpec(shape, lambda n: (0,) * len(shape))
    return [pl.BlockSpec((RL, QKV), lambda n: (blk(n), 0)), pl.BlockSpec((RL, DK), lambda n: (blk(n), 0)),
            pl.BlockSpec((RL, DK), lambda n: (blk(n), 0)), whole((HEADS, RL, RL)), whole((HEADS, RL, DK)),
            whole((HEADS, RL, DK)), whole((HEADS, DK, DV))]


def _ret_head(ref, h):
    q0, k0, v0 = h * DK, HEADS * DK + h * DK, 2 * HEADS * DK + h * DV
    return (slice(None), slice(q0, q0 + DK)), (slice(None), slice(k0, k0 + DK)), (slice(None), slice(v0, v0 + DV))


def _ret_fwd(name, z, rc):
    rows = z.shape[0]
    nb = rows // RL

    def body(qkv_ref, cos_ref, sin_ref, dm_ref, qd_ref, kd_ref, cd_ref, o_ref, st_ref, state):
        @pl.when(pl.program_id(0) == 0)
        def _():
            state[...] = jnp.zeros_like(state)

        cos, sin = cos_ref[...], sin_ref[...]
        for h in range(HEADS):
            qs, ks, vs = _ret_head(qkv_ref, h)
            qf = _rope(qkv_ref[qs].astype(F32), cos, sin) * (DK ** -0.5)
            kf = _rope(qkv_ref[ks].astype(F32), cos, sin)
            vb = qkv_ref[vs]
            s0 = state[h]
            s0b = s0.astype(BF16)
            st_ref[h] = s0b
            sc = _dot(qf.astype(BF16), kf.astype(BF16), 1, 1) * dm_ref[h]
            o = _dot(sc.astype(BF16), vb, 1, 0) + _dot((qf * qd_ref[h]).astype(BF16), s0b, 1, 0)
            o_ref[:, h * DV:(h + 1) * DV] = o.astype(o_ref.dtype)
            state[h] = cd_ref[h] * s0 + _dot((kf * kd_ref[h]).astype(BF16), vb, 0, 0)

    return pl.pallas_call(
        body, name=name, grid=(nb,), in_specs=_ret_in_specs(lambda n: n),
        out_specs=[pl.BlockSpec((RL, HEADS * DV), lambda n: (n, 0)),
                   pl.BlockSpec((None, HEADS, DK, DV), lambda n: (n, 0, 0, 0))],
        out_shape=[jax.ShapeDtypeStruct((rows, HEADS * DV), BF16), jax.ShapeDtypeStruct((nb, HEADS, DK, DV), BF16)],
        scratch_shapes=[pltpu.VMEM((HEADS, DK, DV), F32)],
        compiler_params=_cparams(("arbitrary",), 40),
    )(z, rc["cos"], rc["sin"], rc["dm"], rc["qd"], rc["kd"], rc["cd"])


def _ret_bwd(name, z, rc, states, do, dz):
    rows = z.shape[0]
    nb = rows // RL
    rev = lambda n: nb - 1 - n

    def body(qkv_ref, cos_ref, sin_ref, dm_ref, qd_ref, kd_ref, cd_ref, st_ref, do_ref, dz_in, dqkv_ref, dstate):
        @pl.when(pl.program_id(0) == 0)
        def _():
            dstate[...] = jnp.zeros_like(dstate)

        cos, sin = cos_ref[...], sin_ref[...]
        for h in range(HEADS):
            qs, ks, vs = _ret_head(qkv_ref, h)
            dm, qd, kd = dm_ref[h], qd_ref[h], kd_ref[h]
            qf = _rope(qkv_ref[qs].astype(F32), cos, sin) * (DK ** -0.5)
            kf = _rope(qkv_ref[ks].astype(F32), cos, sin)
            qb, kb, vb, s0b = qf.astype(BF16), kf.astype(BF16), qkv_ref[vs], st_ref[h]
            dob = do_ref[:, h * DV:(h + 1) * DV]
            qsb, ksb = (qf * qd).astype(BF16), (kf * kd).astype(BF16)
            ds1 = dstate[h]
            ds1b = ds1.astype(BF16)
            scb = (_dot(qb, kb, 1, 1) * dm).astype(BF16)
            dscb = (_dot(dob, vb, 1, 1) * dm).astype(BF16)
            dq = _dot(dscb, kb, 1, 0) + _dot(dob, s0b, 1, 1) * qd
            dk = _dot(dscb, qb, 0, 0) + _dot(vb, ds1b, 1, 1) * kd
            dv = _dot(scb, dob, 0, 0) + _dot(ksb, ds1b, 1, 0)
            dqkv_ref[qs] = _rope_t(dq * (DK ** -0.5), cos, sin).astype(dqkv_ref.dtype)
            dqkv_ref[ks] = _rope_t(dk, cos, sin).astype(dqkv_ref.dtype)
            dqkv_ref[vs] = dv.astype(dqkv_ref.dtype)
            dstate[h] = cd_ref[h] * ds1 + _dot(qsb, dob, 0, 0)

    return pl.pallas_call(
        body, name=name, grid=(nb,),
        in_specs=_ret_in_specs(rev) + [pl.BlockSpec((None, HEADS, DK, DV), lambda n: (rev(n), 0, 0, 0)),
                                       pl.BlockSpec((RL, HEADS * DV), lambda n: (rev(n), 0)), ANY],
        out_specs=pl.BlockSpec((RL, QKV), lambda n: (rev(n), 0)),
        out_shape=jax.ShapeDtypeStruct(dz.shape, dz.dtype), input_output_aliases={9: 0},
        scratch_shapes=[pltpu.VMEM((HEADS, DK, DV), F32)],
        compiler_params=_cparams(("arbitrary",), 40),
    )(z, rc["cos"], rc["sin"], rc["dm"], rc["qd"], rc["kd"], rc["cd"], states, do, dz)


def _layer_fwd(l, x, p, rc):
    n = f"l{l}_"
    h = _rms_fwd(n + "rms1", x, p["norm_mix_w"])
    z = _mm_nn_sh(n + "in_proj", h, p["w_in"], out_dtype=BF16)
    o, states = _ret_fwd(n + "ret", z, rc)
    a_in = _mixa_fwd(n + "mixa", z, o, p["ret_gn_w"])
    b_in = _sgu_fwd(n + "sgu", z, p["sgu_ln_w"], p["sgu_ln_b"], p["wm"], p["bexp"])
    if "late" in p:
        p.update(p.pop("late")(b_in))
    br_a = _mm_nn(n + "ret_proj", a_in, p["ret_proj"], out_dtype=BF16, after=p.get("late_token"))
    br_b = _mm_nn(n + "sgu_proj", b_in, p["sgu_proj"], out_dtype=BF16)
    merged = _merge_fwd(n + "merge", z, br_a, br_b)
    x1 = _mm_nn(n + "out_proj", merged, p["w_out"], out_dtype=F32, res=x)
    h2 = _rms_fwd(n + "rms2", x1, p["norm_ffn_w"])
    ffn_a, ffn_c, f = _ffn_in_swiglu(n + "ffn_in", h2, p["w_ffn_in"])
    x2 = _mm_nn(n + "ffn_out", f, p["w_ffn_out"], out_dtype=F32, res=x1, tk=2 * SH_FF)
    saved = dict(x=x, h=h, z=z, o=o, states=states, a_in=a_in, b_in=b_in, br_a=br_a, br_b=br_b, merged=merged,
                 x1=x1, h2=h2, ffn_a=ffn_a, ffn_c=ffn_c, f=f)
    return x2, saved


def _layer_bwd_ffn(l, dx2, p, s, after):
    n = f"l{l}b_"
    df = _mm_nt(n + "d_f", dx2, p["w_ffn_out"], out_dtype=BF16, tn=SH_FF, after=after)
    g_ffn_out = _mm_tn(n + "g_ffn_out", s["f"], dx2, out_dtype=BF16, tm=SH_FF, tk=2048)
    dac = _swiglu_bwd(n + "swiglu", s["ffn_a"], s["ffn_c"], df)
    dh2 = _mm_nt_sh(n + "d_h2", dac, p["w_ffn_in"], out_dtype=BF16)
    g_ffn_in = _mm_tn_sh(n + "g_ffn_in", s["h2"], dac, SH_FF, out_dtype=BF16)
    dx1, g_nffn = _rms_bwd(n + "rms2", s["x1"], p["norm_ffn_w"], dh2, dx2)
    return dx1, dict(w_ffn_in=g_ffn_in, w_ffn_out=g_ffn_out.reshape(NDEV, -1, D)), dict(norm_ffn_w=g_nffn)


def _layer_bwd_mix(l, dx1, p, s, rc, after, send):
    n = f"l{l}b_"
    dmerged = _mm_nt(n + "d_merged", dx1, p["w_out"], out_dtype=BF16, after=after)
    g_out = _mm_tn(n + "g_out", s["merged"], dx1, out_dtype=BF16, tk=2048)
    dz, dbr_a, dbr_b = _merge_bwd(n + "merge", s["z"], s["br_a"], s["br_b"], dmerged)
    da_in = _mm_nt(n + "d_a_in", dbr_a, p["ret_proj"], out_dtype=BF16)
    g_ret_proj = _mm_tn(n + "g_ret_proj", s["a_in"], dbr_a, out_dtype=BF16, tk=2048)
    db_in = _mm_nt(n + "d_b_in", dbr_b, p["sgu_proj"], out_dtype=BF16)
    g_sgu_proj = _mm_tn(n + "g_sgu_proj", s["b_in"], dbr_b, out_dtype=BF16, tk=2048)
    dz, do, g_gn = _mixa_bwd(n + "mixa", s["z"], s["o"], p["ret_gn_w"], da_in, dz)
    dz = _ret_bwd(n + "ret", s["z"], rc, s["states"], do, dz)
    dz, g_ws, dbf, g_lw, g_lb = _sgu_bwd(n + "sgu", s["z"], p["sgu_ln_w"], p["sgu_ln_b"], p["wm"], p["wmt"],
                                          p["bexp"], p["maskf"], db_in, dz)
    g_bs = _group_sums(n + "g_bs", dbf)
    g_in = _mm_tn_sh(n + "g_in", s["h"], dz, SH_IN, out_dtype=BF16)
    token = send(dict(w_in=g_in, ret_proj=g_ret_proj.reshape(NDEV, -1, D), sgu_proj=g_sgu_proj.reshape(NDEV, -1, D),
                      w_out=g_out.reshape(NDEV, -1, D)))
    dh = _mm_nt_sh(n + "d_h", dz, p["w_in"], out_dtype=BF16, after=token)
    dx, g_nmix = _rms_bwd(n + "rms1", s["x"], p["norm_mix_w"], dh, dx1)
    small = dict(norm_mix_w=g_nmix, ret_gn_w=g_gn, sgu_ln_w=g_lw, sgu_ln_b=g_lb, sgu_w_s=g_ws, sgu_b_s=g_bs)
    return dx, small


def _slot(ref, p):
    return ref.at[4 * p[0] + 2 * p[1] + p[2]]


def _allgather(name, shards):
    n = len(shards)

    def body(*refs):
        ins, outs = refs[:n], refs[n:2 * n]
        send_sems, recv_sems, local_sems = refs[2 * n:]
        x, y, c = lax.axis_index("x"), lax.axis_index("y"), lax.axis_index("c")
        me, sibling = (x, y, c), (x, y, 1 - c)
        chips = [(1 - x, y), (x, 1 - y), (1 - x, 1 - y)]

        def copy(a, k, block, to, src=None):
            return pltpu.make_async_remote_copy(
                src_ref=_slot(outs[a], block) if src is None else src, dst_ref=_slot(outs[a], block),
                send_sem=send_sems.at[a, k], recv_sem=recv_sems.at[a, k], device_id=to, device_id_type=MESH_T)

        mine = [pltpu.make_async_copy(ins[a], _slot(outs[a], me), local_sems.at[a]) for a in range(n)]
        for cp in mine:
            cp.start()
        first = []
        for a in range(n):
            first.append(copy(a, 0, me, sibling, src=ins[a]))
            first += [copy(a, 1 + j, me, (*chip, c), src=ins[a]) for j, chip in enumerate(chips)]
        for cp in first:
            cp.start()
        passed = []
        for j, chip in enumerate(chips):
            for a in range(n):
                copy(a, 1 + j, (*chip, c), me).wait_recv()
                cp = copy(a, 4 + j, (*chip, c), sibling)
                cp.start()
                passed.append(cp)
        for a in range(n):
            copy(a, 0, sibling, me).wait_recv()
            for j, chip in enumerate(chips):
                copy(a, 4 + j, (*chip, 1 - c), me).wait_recv()
        for cp in first + passed:
            cp.wait_send()
        for cp in mine:
            cp.wait()

    return pl.pallas_call(
        body, name=name, in_specs=[ANY] * n, out_specs=[ANY] * n,
        out_shape=[jax.ShapeDtypeStruct((NDEV,) + s.shape, s.dtype) for s in shards],
        scratch_shapes=[pltpu.SemaphoreType.DMA((n, 7)), pltpu.SemaphoreType.DMA((n, 7)), pltpu.SemaphoreType.DMA((n,))],
    )(*shards)


def _peers(x, y, c):
    return [((1 - x) if fx else x, (1 - y) if fy else y, (1 - c) if fc else c)
            for fx in (0, 1) for fy in (0, 1) for fc in (0, 1) if fx + fy + fc]


def _exchange_copy(whole, src_ref, land_ref, send_sems, recv_sems, a, k, peer, me, landed):
    return pltpu.make_async_remote_copy(
        src_ref=src_ref if whole else _slot(src_ref, peer), dst_ref=_slot(land_ref, peer if landed else me),
        send_sem=send_sems.at[7 * a + k], recv_sem=recv_sems.at[7 * a + k], device_id=peer, device_id_type=MESH_T)


def _exchange_start(name, srcs, lands, whole, after):
    n = len(srcs)

    def body(*refs):
        src_refs, land_refs = refs[:n], refs[n:2 * n]
        send_sems, recv_sems = refs[2 * n + 1], refs[2 * n + 2]
        token = refs[-1]
        x, y, c = lax.axis_index("x"), lax.axis_index("y"), lax.axis_index("c")
        for a in range(n):
            for k, peer in enumerate(_peers(x, y, c)):
                _exchange_copy(whole, src_refs[a], land_refs[a], send_sems, recv_sems, a, k, peer, (x, y, c), False).start()
        token[...] = jnp.zeros_like(token)

    hbm = lambda t: pltpu.HBM(t.shape, t.dtype)
    out = pl.pallas_call(
        body, name=name, in_specs=[HBM_SPEC] * (2 * n) + [ANY],
        out_specs=(SEM_SPEC, SEM_SPEC, *[HBM_SPEC] * (2 * n), pl.BlockSpec(memory_space=pltpu.VMEM)),
        out_shape=(pltpu.SemaphoreType.DMA((7 * n,)), pltpu.SemaphoreType.DMA((7 * n,)), *[hbm(t) for t in srcs],
                   *[hbm(t) for t in lands], jax.ShapeDtypeStruct((8, 128), F32)),
        input_output_aliases={i: 2 + i for i in range(2 * n)},
        compiler_params=pltpu.CompilerParams(has_side_effects=pltpu.SideEffectType.DATAFLOW_SIDE_EFFECTING),
    )(*[pltpu.with_memory_space_constraint(t, pltpu.HBM) for t in (*srcs, *lands)], after)
    return dict(send=out[0], recv=out[1], srcs=out[2:2 + n], lands=out[2 + n:2 + 2 * n], token=out[-1])


def _exchange_wait(name, started, whole, after):
    n = len(started["srcs"])

    def body(*refs):
        src_refs, land_refs = refs[:n], refs[n:2 * n]
        send_sems, recv_sems = refs[2 * n], refs[2 * n + 1]
        x, y, c = lax.axis_index("x"), lax.axis_index("y"), lax.axis_index("c")
        for a in range(n):
            for k, peer in enumerate(_peers(x, y, c)):
                cp = _exchange_copy(whole, src_refs[a], land_refs[a], send_sems, recv_sems, a, k, peer, (x, y, c), True)
                cp.wait_send()
                cp.wait_recv()

    hbm = lambda t: pltpu.HBM(t.shape, t.dtype)
    out = pl.pallas_call(
        body, name=name, in_specs=[HBM_SPEC] * (2 * n) + [SEM_SPEC, SEM_SPEC, ANY], out_specs=[HBM_SPEC] * (2 * n),
        out_shape=[hbm(t) for t in (*started["srcs"], *started["lands"])],
        input_output_aliases={i: i for i in range(2 * n)},
        compiler_params=pltpu.CompilerParams(has_side_effects=pltpu.SideEffectType.DATAFLOW_SIDE_EFFECTING),
    )(*started["srcs"], *started["lands"], started["send"], started["recv"], after)
    return out[n:]


def _adamw(w, g, m, v):
    m = ADAM_B1 * m + (1.0 - ADAM_B1) * g
    v = ADAM_B2 * v + (1.0 - ADAM_B2) * (g * g)
    m_hat = m / (1.0 - ADAM_B1 ** ADAM_STEP)
    v_hat = v / (1.0 - ADAM_B2 ** ADAM_STEP)
    return -ADAM_LR * (m_hat / (jnp.sqrt(v_hat) + ADAM_EPS) + ADAM_WD * w), m, v


def _sum_and_adamw(name, recv, w, m, v, layer, prev):
    _, r, c = recv.shape
    tr = max(t for t in range(16, r + 1, 16) if r % t == 0 and t * c <= ADAM_TILE_ELEMS)

    def body(recv_ref, w_ref, m_ref, v_ref, *rest):
        g_ref, d_ref, m_out, v_out = rest[4:]
        g = recv_ref[0].astype(F32)
        for s in range(1, NDEV):
            g = g + recv_ref[s].astype(F32)
        delta, m_new, v_new = _adamw(w_ref[...], g, m_ref[...], v_ref[...])
        g_ref[...] = g
        d_ref[...] = delta
        m_out[...] = m_new
        v_out[...] = v_new

    lay = pl.BlockSpec((None, tr, c), lambda i: (layer, i, 0))
    return pl.pallas_call(
        body, name=name, grid=(r // tr,),
        in_specs=[pl.BlockSpec((NDEV, tr, c), lambda i: (0, i, 0)), lay, lay, lay] + [ANY] * 4,
        out_specs=[lay] * 4, out_shape=[jax.ShapeDtypeStruct(w.shape, F32)] * 4,
        input_output_aliases={4: 0, 5: 1, 6: 2, 7: 3},
        compiler_params=_cparams(("arbitrary",), 48),
    )(recv, w, m, v, *prev)


def _small_sum_and_adamw(name, recv, w, m, v):
    r = w.shape[0]

    def body(recv_ref, w_ref, m_ref, v_ref, g_ref, d_ref, m_out, v_out):
        g = recv_ref[0]
        for s in range(1, NDEV):
            g = g + recv_ref[s]
        delta, m_new, v_new = _adamw(w_ref[...], g, m_ref[...], v_ref[...])
        g_ref[...] = g
        d_ref[...] = delta
        m_out[...] = m_new
        v_out[...] = v_new

    return pl.pallas_call(body, name=name, out_shape=[jax.ShapeDtypeStruct((r, 128), F32)] * 4,
                          compiler_params=_cparams(None, 48))(recv, w, m, v)


BIG = ("w_in", "ret_proj", "sgu_proj", "w_out", "w_ffn_in", "w_ffn_out")
SMALL = ("norm_mix_w", "ret_gn_w", "sgu_ln_w", "sgu_ln_b", "sgu_w_s", "sgu_b_s", "norm_ffn_w", "final_norm_w")
WEIGHTS = ("norm_mix_w", "w_in", "ret_gn_w", "ret_proj", "sgu_ln_w", "sgu_ln_b", "sgu_w_s", "sgu_b_s", "sgu_proj",
           "w_out", "norm_ffn_w", "w_ffn_in", "w_ffn_out", "final_norm_w")


def _pack_small(d):
    return jnp.concatenate([d[k].reshape(-1, 128) for k in SMALL], axis=0)


def _unpack_small(packed, like):
    out, r0 = {}, 0
    for k in SMALL:
        r = like[k].size // 128
        out[k] = packed[r0:r0 + r].reshape(like[k].shape)
        r0 += r
    return out


def _gathered(keys, arrays):
    return {k: a if k in ("w_in", "w_ffn_in") else a.reshape(-1, D) for k, a in zip(keys, arrays)}


def _layer_params(l, w, gathered, mask):
    p = _gathered(BIG, gathered)
    for k in ("norm_mix_w", "ret_gn_w", "sgu_ln_w", "sgu_ln_b", "norm_ffn_w"):
        p[k] = w[k][l][None, :]
    wm = jnp.where(mask[None], w["sgu_w_s"][l], 0.0)
    p["wm"], p["wmt"] = wm.astype(BF16), jnp.swapaxes(wm, 1, 2).astype(BF16)
    p["bexp"] = jnp.repeat(w["sgu_b_s"][l].T, SCH, axis=1)
    p["maskf"] = mask.astype(F32)
    return p


def _step(w, m, v, x, target):
    depth = w["w_in"].shape[0]
    rc = _ret_consts(x.shape[0])
    pos = jnp.arange(SLEN)
    mask = (pos[None, :] // CHUNK) <= (pos[:, None] // CHUNK)
    shards = [[w[k][l].astype(BF16) for k in BIG] for l in range(depth)]
    me = 4 * lax.axis_index("x") + 2 * lax.axis_index("y") + lax.axis_index("c")

    def lands_of(ts):
        return [lax.dynamic_update_index_in_dim(lax.empty((NDEV,) + t.shape, t.dtype), t, me, 0) for t in ts]

    w_in0 = _allgather("gather_l0_in", shards[0][:1])[0]
    rest0 = _exchange_start("gather_start_l0", shards[0][1:], lands_of(shards[0][1:]), True, w_in0)
    gathered = [w_in0] + list(rest0["lands"])
    saved, layers, next_gather = [], [], {}

    def late0(after):
        got = _gathered(BIG[1:], _exchange_wait("gather_wait_l0", rest0, True, after))
        next_gather[1] = _exchange_start("gather_start_l1", shards[1], lands_of(shards[1]), True, got["ret_proj"])
        return {**got, "late_token": next_gather[1]["token"]}

    for l in range(depth):
        p = _layer_params(l, w, gathered, mask)
        if l == 0:
            p["norm_mix_w"] = p["norm_mix_w"] + rest0["token"][0, 0]
            p["late"] = late0
        elif l + 1 < depth:
            next_gather[l + 1] = _exchange_start(f"gather_start_l{l + 1}", shards[l + 1], lands_of(shards[l + 1]), True,
                                                 gathered[0])
            p["norm_mix_w"] = p["norm_mix_w"] + next_gather[l + 1]["token"][0, 0]
        x, s = _layer_fwd(l, x, p, rc)
        started = next_gather.get(l + 1)
        if l + 1 < depth:
            gathered = _exchange_wait(f"gather_wait_l{l + 1}", started, True, x)
        saved.append(s)
        layers.append(p)

    dx, g_final, loss_row = _loss_bwd("loss", x, w["final_norm_w"][None, :], target)
    loss = lax.psum(loss_row[0, 0], ("x", "y", "c"))

    res = {k: [lax.empty(w[k].shape, F32) for _ in range(4)] for k in BIG}
    small = {k: [None] * depth for k in SMALL if k != "final_norm_w"}

    def scatter_start(tag, big, after):
        keys = [k for k in BIG if k in big]
        parts = [big[k] for k in keys]
        lands = [lax.dynamic_update_index_in_dim(lax.empty(t.shape, t.dtype), lax.dynamic_index_in_dim(t, me, 0), me, 0)
                 for t in parts]
        return keys, _exchange_start(f"scatter_start_{tag}", parts, lands, False, after)

    def finish(l, pending, after):
        for tag, (keys, started) in pending:
            recv = _exchange_wait(f"scatter_wait_{tag}", started, False, after)
            for k, rv in zip(keys, recv):
                res[k] = _sum_and_adamw(f"adamw_{k}_l{l}", rv, w[k], m[k], v[k], l, res[k])

    pending, token = None, loss_row
    for l in reversed(range(depth)):
        dx1, big_ffn, sm_ffn = _layer_bwd_ffn(l, dx, layers[l], saved[l], token)
        now = [(f"ffn_l{l}", scatter_start(f"ffn_l{l}", big_ffn, dx1))]

        def send(big_mix, l=l, now=now, dx1=dx1):
            now.append((f"mix_l{l}", scatter_start(f"mix_l{l}", big_mix, dx1)))
            return now[1][1][1]["token"]

        dx, sm_mix = _layer_bwd_mix(l, dx1, layers[l], saved[l], rc, now[0][1][1]["token"], send)
        token = now[1][1][1]["token"]
        if pending is not None:
            finish(l + 1, pending, dx)
        pending = now
        for k, g in {**sm_ffn, **sm_mix}.items():
            small[k][l] = g
    finish(0, pending, dx)

    g_small = {k: jnp.stack(small[k]).reshape(w[k].shape) for k in small}
    g_small["final_norm_w"] = g_final.reshape(w["final_norm_w"].shape)
    recv = _allgather("gather_small", [_pack_small(g_small)])[0]
    sres = _small_sum_and_adamw("adamw_small", recv, _pack_small(w), _pack_small(m), _pack_small(v))
    sres = [_unpack_small(a, w) for a in sres]

    outs = []
    for i in range(4):
        outs.append([sres[i][k] if k in SMALL else res[k][i] for k in WEIGHTS])
    return loss, dx, outs


def kernel(x, norm_mix_w, w_in, ret_gn_w, ret_proj, sgu_ln_w, sgu_ln_b, sgu_w_s, sgu_b_s, sgu_proj, w_out, norm_ffn_w, w_ffn_in, w_ffn_out, final_norm_w, loss_target, m_norm_mix_w, m_w_in, m_ret_gn_w, m_ret_proj, m_sgu_ln_w, m_sgu_ln_b, m_sgu_w_s, m_sgu_b_s, m_sgu_proj, m_w_out, m_norm_ffn_w, m_w_ffn_in, m_w_ffn_out, m_final_norm_w, v_norm_mix_w, v_w_in, v_ret_gn_w, v_ret_proj, v_sgu_ln_w, v_sgu_ln_b, v_sgu_w_s, v_sgu_b_s, v_sgu_proj, v_w_out, v_norm_ffn_w, v_w_ffn_in, v_w_ffn_out, v_final_norm_w):
    w = dict(zip(WEIGHTS, (norm_mix_w, w_in, ret_gn_w, ret_proj, sgu_ln_w, sgu_ln_b, sgu_w_s, sgu_b_s, sgu_proj, w_out,
                           norm_ffn_w, w_ffn_in, w_ffn_out, final_norm_w)))
    m = dict(zip(WEIGHTS, (m_norm_mix_w, m_w_in, m_ret_gn_w, m_ret_proj, m_sgu_ln_w, m_sgu_ln_b, m_sgu_w_s, m_sgu_b_s,
                           m_sgu_proj, m_w_out, m_norm_ffn_w, m_w_ffn_in, m_w_ffn_out, m_final_norm_w)))
    v = dict(zip(WEIGHTS, (v_norm_mix_w, v_w_in, v_ret_gn_w, v_ret_proj, v_sgu_ln_w, v_sgu_ln_b, v_sgu_w_s, v_sgu_b_s,
                           v_sgu_proj, v_w_out, v_norm_ffn_w, v_w_ffn_in, v_w_ffn_out, v_final_norm_w)))
    loss, dx, (grads, deltas, new_m, new_v) = _step(w, m, v, x.reshape(x.shape[1:]), loss_target.reshape(x.shape[1:]))
    return (loss, dx[None], *grads, *deltas, *new_m, *new_v)
```
